```python
import jax, jax.numpy as jnp
from jax import lax
import numpy as np

D_MODEL = 1024
BATCH = 16
SEQ = 4096
DEPTH = 1

CONV_W = D_MODEL
CONV_K = 3
M_HEADS = 8
M_V = D_MODEL
M_DV = M_V // M_HEADS
M_DK = M_DV // 2
M_QK = M_HEADS * M_DK
QK_CONV_K = 4
CHUNK = 64
P_HEADS = 8
N_KEYS = 128
N_EXPERTS = N_KEYS * N_KEYS
P_TOPK = 16
P_DQ = 256
P_DKH = P_DQ // 2
P_BLOCK = 128
EPS = 1e-6

_IN_SIZES = (CONV_W, CONV_W, CONV_W, M_QK, M_QK, M_V, M_V, M_HEADS, M_HEADS, D_MODEL, D_MODEL)
IN_WIDTH = sum(_IN_SIZES)
SPLIT_IDX = tuple(int(s) for s in np.cumsum(_IN_SIZES)[:-1])

kernel_name = "hybrid_conv_mlstm_peer_block"


def rms_norm(x, w):
    xf = x.astype(jnp.float32)
    y = xf * lax.rsqrt(jnp.mean(xf * xf, axis=-1, keepdims=True) + EPS)
    return (y * w.astype(jnp.float32)).astype(x.dtype)


def modulate(h, shift, scale):
    return h * (1 + scale[:, None, :]) + shift[:, None, :]


def causal_dwconv(u, w):
    K = w.shape[0]
    S = u.shape[1]
    up = jnp.pad(u, ((0, 0), (K - 1, 0), (0, 0)))
    return sum(up[:, j:j + S] * w[j] for j in range(K))


def mlstm_chunkwise(q, k, v, i_pre, f_pre):
    B, H, S, DK = q.shape
    DV = v.shape[-1]
    NC = S // CHUNK
    q = q * (DK ** -0.5)
    logf = jax.nn.log_sigmoid(f_pre)

    def chunks(t):
        return jnp.moveaxis(t.reshape(B, H, NC, CHUNK, *t.shape[3:]), 2, 0)

    xs = (chunks(q), chunks(k), chunks(v), chunks(i_pre), chunks(logf))
    mask = jnp.tril(jnp.ones((CHUNK, CHUNK), dtype=bool))

    def step(carry, inp):
        C, n, m = carry
        qb, kb, vb, ib, lfb = inp
        b = jnp.cumsum(lfb, axis=-1)
        dlog = b[..., :, None] - b[..., None, :] + ib[..., None, :]
        dlog = jnp.where(mask, dlog, -jnp.inf)
        inter = b + m[..., None]
        m_t = jnp.maximum(inter, jnp.max(dlog, axis=-1))
        w_ts = jnp.exp(dlog - m_t[..., None])
        s_qk = jnp.einsum('bhtd,bhsd->bhts', qb, kb) * w_ts
        a_inter = jnp.exp(inter - m_t)
        num = jnp.einsum('bhts,bhsv->bhtv', s_qk, vb) + a_inter[..., None] * jnp.einsum('bhtd,bhdv->bhtv', qb, C)
        den = jnp.sum(s_qk, axis=-1) + a_inter * jnp.einsum('bhtd,bhd->bht', qb, n)
        h = num / jnp.maximum(jnp.abs(den), jnp.exp(-m_t))[..., None]
        b_last = b[..., -1]
        g = b_last[..., None] - b + ib
        m_new = jnp.maximum(b_last + m, jnp.max(g, axis=-1))
        decay = jnp.exp(b_last + m - m_new)
        ws = jnp.exp(g - m_new[..., None])
        C_new = decay[..., None, None] * C + jnp.einsum('bhs,bhsd,bhsv->bhdv', ws, kb, vb)
        n_new = decay[..., None] * n + jnp.einsum('bhs,bhsd->bhd', ws, kb)
        return (C_new, n_new, m_new), h

    init = (jnp.zeros((B, H, DK, DV), jnp.float32), jnp.zeros((B, H, DK), jnp.float32), jnp.zeros((B, H), jnp.float32))
    _, hs = lax.scan(step, init, xs)
    return jnp.moveaxis(hs, 0, 2).reshape(B, H, S, DV)


def token_mixer(h, w_in, conv_a_w, conv_qk_w, b_igate, b_fgate, mh_norm_w, w_branch_a, w_branch_m, w_out):
    B, S, _ = h.shape
    proj = h @ w_in
    c_gate, b_gate, x_in, q, k, v, o, i_pre, f_pre, ga, gm = jnp.split(proj, SPLIT_IDX, axis=-1)
    ya = b_gate * causal_dwconv(c_gate * x_in, conv_a_w)
    qk = jax.nn.silu(causal_dwconv(jnp.concatenate([q, k], axis=-1), conv_qk_w))
    q, k = jnp.split(qk, 2, axis=-1)

    def to_heads(t, d):
        return t.reshape(B, S, M_HEADS, d).transpose(0, 2, 1, 3).astype(jnp.float32)

    ig = (i_pre + b_igate).transpose(0, 2, 1).astype(jnp.float32)
    fg = (f_pre + b_fgate).transpose(0, 2, 1).astype(jnp.float32)
    hm = mlstm_chunkwise(to_heads(q, M_DK), to_heads(k, M_DK), to_heads(v, M_DV), ig, fg)
    hm = hm * lax.rsqrt(jnp.mean(hm * hm, axis=-1, keepdims=True) + EPS)
    hm = hm.transpose(0, 2, 1, 3).reshape(B, S, M_V) * mh_norm_w.astype(jnp.float32)
    ym = jax.nn.sigmoid(o) * hm.astype(h.dtype)
    mix = jax.nn.sigmoid(ga) * (ya @ w_branch_a) + jax.nn.sigmoid(gm) * (ym @ w_branch_m)
    return mix @ w_out


def peer(h, wq, subkeys, u, v):
    B, S, D = h.shape
    T = B * S
    ht = h.reshape(T, D)
    q = (ht @ wq).reshape(T, P_HEADS, 2, P_DKH)
    s = jnp.einsum('thcd,hcnd->thcn', q, subkeys).astype(jnp.float32)
    s1, i1 = lax.top_k(s[:, :, 0], P_TOPK)
    s2, i2 = lax.top_k(s[:, :, 1], P_TOPK)
    cand = (s1[..., :, None] + s2[..., None, :]).reshape(T, P_HEADS, P_TOPK * P_TOPK)
    sc, ci = lax.top_k(cand, P_TOPK)
    e = jnp.take_along_axis(i1, ci // P_TOPK, axis=-1) * N_KEYS + jnp.take_along_axis(i2, ci % P_TOPK, axis=-1)
    g = jax.nn.softmax(sc, axis=-1)
    nb = T // P_BLOCK

    def block(args):
        hb, eb, gb = args
        ub = jnp.take(u, eb, axis=0)
        z = jnp.einsum('thkd,td->thk', ub, hb)
        a = (jax.nn.gelu(z.astype(jnp.float32)) * gb).astype(hb.dtype)
        return jnp.einsum('thk,thkd->td', a, jnp.take(v, eb, axis=0))

    out = lax.map(block, (ht.reshape(nb, P_BLOCK, D), e.reshape(nb, P_BLOCK, P_HEADS, P_TOPK), g.reshape(nb, P_BLOCK, P_HEADS, P_TOPK)))
    return out.reshape(B, S, D).astype(h.dtype)


def setup_inputs(seed: int = 0) -> dict:
    key = jax.random.key(seed)
    ks = jax.random.split(key, 24)
    L, D = DEPTH, D_MODEL

    def nrm(k, shape, scale):
        return jax.random.normal(k, shape, jnp.float32) * scale

    return {
        "x": nrm(ks[0], (BATCH, SEQ, D), 1.0),
        "c": nrm(ks[1], (BATCH, D), 1.0),
        "w_ada": nrm(ks[2], (L, D, 6 * D), 0.2 * D ** -0.5),
        "b_ada": nrm(ks[3], (L, 6 * D), 0.02),
        "norm1_pre": 1.0 + nrm(ks[4], (L, D), 0.05),
        "norm1_post": 1.0 + nrm(ks[5], (L, D), 0.05),
        "w_in": nrm(ks[6], (L, D, IN_WIDTH), D ** -0.5),
        "conv_a_w": nrm(ks[7], (L, CONV_K, CONV_W), CONV_K ** -0.5),
        "conv_qk_w": nrm(ks[8], (L, QK_CONV_K, 2 * M_QK), QK_CONV_K ** -0.5),
        "b_igate": nrm(ks[9], (L, M_HEADS), 0.1),
        "b_fgate": jnp.linspace(3.0, 6.0, M_HEADS, dtype=jnp.float32)[None, :] + nrm(ks[10], (L, M_HEADS), 0.1),
        "mh_norm_w": 1.0 + nrm(ks[11], (L, M_V), 0.05),
        "w_branch_a": nrm(ks[12], (L, CONV_W, D), CONV_W ** -0.5),
        "w_branch_m": nrm(ks[13], (L, M_V, D), M_V ** -0.5),
        "w_out": nrm(ks[14], (L, D, D), D ** -0.5),
        "norm2_pre": 1.0 + nrm(ks[15], (L, D), 0.05),
        "norm2_post": 1.0 + nrm(ks[16], (L, D), 0.05),
        "peer_wq": nrm(ks[17], (L, D, P_HEADS * P_DQ), D ** -0.5),
        "peer_subkeys": nrm(ks[18], (L, P_HEADS, 2, N_KEYS, P_DKH), P_DKH ** -0.5),
        "peer_u": nrm(ks[19], (L, N_EXPERTS, D), D ** -0.5),
        "peer_v": nrm(ks[20], (L, N_EXPERTS, D), D ** -0.5),
    }


def reference(x, c, w_ada, b_ada, norm1_pre, norm1_post, w_in, conv_a_w, conv_qk_w, b_igate, b_fgate, mh_norm_w, w_branch_a, w_branch_m, w_out, norm2_pre, norm2_post, peer_wq, peer_subkeys, peer_u, peer_v):
    for l in range(DEPTH):
        ada = jax.nn.silu(c) @ w_ada[l] + b_ada[l]
        sh1, sc1, g1, sh2, sc2, g2 = jnp.split(ada, 6, axis=-1)
        h = modulate(rms_norm(x, norm1_pre[l]), sh1, sc1)
        y = token_mixer(h, w_in[l], conv_a_w[l], conv_qk_w[l], b_igate[l], b_fgate[l], mh_norm_w[l], w_branch_a[l], w_branch_m[l], w_out[l])
        x = x + g1[:, None, :] * rms_norm(y, norm1_post[l])
        h = modulate(rms_norm(x, norm2_pre[l]), sh2, sc2)
        y = peer(h, peer_wq[l], peer_subkeys[l], peer_u[l], peer_v[l])
        x = x + g2[:, None, :] * rms_norm(y, norm2_post[l])
    return x
```

```python
import functools

import jax
import jax.numpy as jnp
from jax import lax
from jax.experimental import pallas as pl
from jax.experimental.pallas import tpu as pltpu

F32 = jnp.float32
BF16 = jnp.bfloat16

EPS = 1e-6
M_HEADS = 8
M_DK = 64
M_DV = 128
CONV_K = 3
QK_CONV_K = 4
P_HEADS = 8
N_KEYS = 128
P_TOPK = 16
HALO = 8
LANES = 128
NOT_RANKED = 99.0

VMEM_LIMIT_BYTES = 56 * 1024 * 1024

MLSTM_CHUNK = 64
TS_INPROJ = 256
TS_MIXER = 256
TM_ROUTE = 256
TM_PEER = 512
TE_PEER = 1024


def _resident(shape):
    nd = len(shape)
    return pl.BlockSpec(shape, lambda *_: (0,) * nd, pipeline_mode=pl.Buffered(1))


def _rms_norm(x, w):
    return x * lax.rsqrt(jnp.mean(x * x, axis=-1, keepdims=True) + EPS) * w


def _sigmoid(x):
    return 1.0 / (1.0 + jnp.exp(-x))


def _ada_kernel(c_ref, w_ref, b_ref, o_ref):
    c = c_ref[...]
    s = c * _sigmoid(c)
    o_ref[...] = jnp.dot(s.astype(BF16), w_ref[...].astype(BF16), preferred_element_type=F32) + b_ref[...]


def _ada(c, w_ada, b_ada):
    bsz, d = c.shape
    n = w_ada.shape[1]
    return pl.pallas_call(
        _ada_kernel,
        grid=(n // d,),
        in_specs=[pl.BlockSpec((bsz, d), lambda j: (0, 0)),
                  pl.BlockSpec((d, d), lambda j: (0, j)),
                  pl.BlockSpec((1, d), lambda j: (0, j))],
        out_specs=pl.BlockSpec((bsz, d), lambda j: (0, j)),
        out_shape=jax.ShapeDtypeStruct((bsz, n), F32),
        compiler_params=pltpu.CompilerParams(vmem_limit_bytes=VMEM_LIMIT_BYTES),
        name="ada",
    )(c, w_ada, b_ada.reshape(1, n))


def _inproj_kernel(x_ref, ada_ref, npre_ref, w_ref, wifc_ref, wifr_ref, bifc_ref, bifr_ref, cwa_ref, cwqk_ref,
                   ya_ref, qk_ref, v_ref, so_ref, sga_ref, sgm_ref, gcol_ref, grow_ref,
                   ubuf, qkbuf, *, ts, d):
    @pl.when(pl.program_id(1) == 0)
    def _():
        ubuf[0:HALO, :] = jnp.zeros((HALO, d), F32)
        qkbuf[0:HALO, :] = jnp.zeros((HALO, d), F32)

    x = x_ref[0]
    ada = ada_ref[0]
    h = _rms_norm(x, npre_ref[...]) * (1.0 + ada[1:2, :]) + ada[0:1, :]
    hb = h.astype(BF16)

    def proj(k):
        return jnp.dot(hb, w_ref[:, k * d:(k + 1) * d], preferred_element_type=F32)

    u = proj(0) * proj(2)
    ubuf[HALO:HALO + ts, :] = u
    conv = u * cwa_ref[CONV_K - 1:CONV_K, :]
    for j in range(CONV_K - 1):
        off = HALO - (CONV_K - 1) + j
        conv = conv + ubuf[off:off + ts, :] * cwa_ref[j:j + 1, :]
    ya_ref[0] = (proj(1) * conv).astype(BF16)
    ubuf[0:HALO, :] = ubuf[ts:ts + HALO, :]

    qk = proj(3)
    qkbuf[HALO:HALO + ts, :] = qk
    cq = qk * cwqk_ref[QK_CONV_K - 1:QK_CONV_K, :]
    for j in range(QK_CONV_K - 1):
        off = HALO - (QK_CONV_K - 1) + j
        cq = cq + qkbuf[off:off + ts, :] * cwqk_ref[j:j + 1, :]
    qkbuf[0:HALO, :] = qkbuf[ts:ts + HALO, :]
    cq = cq * _sigmoid(cq)
    is_q = lax.broadcasted_iota(jnp.int32, (1, d), 1) < (M_HEADS * M_DK)
    qk_ref[0] = (cq * jnp.where(is_q, M_DK ** -0.5, 1.0)).astype(BF16)

    v_ref[0] = proj(4).astype(BF16)
    so_ref[0] = _sigmoid(proj(5)).astype(BF16)
    sga_ref[0] = _sigmoid(proj(6)).astype(BF16)
    sgm_ref[0] = _sigmoid(proj(7)).astype(BF16)

    gcol_ref[0] = jnp.dot(hb, wifc_ref[...], preferred_element_type=F32) + bifc_ref[...]
    grow_ref[0] = lax.dot_general(wifr_ref[...], hb, (((1,), (1,)), ((), ())),
                                  preferred_element_type=F32) + bifr_ref[...]


def _inproj(x, ada3, norm_pre, w_main, wif_col, wif_row, bif_col, bif_row, conv_a_w, conv_qk_w):
    bsz, s, d = x.shape
    ts = min(TS_INPROJ, s)
    ng = 2 * M_HEADS
    tok = lambda dt, w=d: jax.ShapeDtypeStruct((bsz, s, w), dt)
    tok_spec = lambda w=d: pl.BlockSpec((1, ts, w), lambda b, i: (b, i, 0))
    return pl.pallas_call(
        functools.partial(_inproj_kernel, ts=ts, d=d),
        grid=(bsz, s // ts),
        in_specs=[tok_spec(),
                  pl.BlockSpec((1, 6, d), lambda b, i: (b, 0, 0)),
                  _resident((1, d)),
                  _resident(w_main.shape),
                  _resident(wif_col.shape),
                  _resident(wif_row.shape),
                  _resident(bif_col.shape),
                  _resident(bif_row.shape),
                  _resident(conv_a_w.shape),
                  _resident(conv_qk_w.shape)],
        out_specs=[tok_spec(), tok_spec(), tok_spec(), tok_spec(), tok_spec(), tok_spec(),
                   tok_spec(LANES),
                   pl.BlockSpec((1, ng, ts), lambda b, i: (b, 0, i))],
        out_shape=[tok(BF16), tok(BF16), tok(BF16), tok(BF16), tok(BF16), tok(BF16),
                   tok(F32, LANES),
                   jax.ShapeDtypeStruct((bsz, ng, s), F32)],
        scratch_shapes=[pltpu.VMEM((HALO + ts, d), F32), pltpu.VMEM((HALO + ts, d), F32)],
        compiler_params=pltpu.CompilerParams(dimension_semantics=("arbitrary", "arbitrary"),
                                             vmem_limit_bytes=VMEM_LIMIT_BYTES),
        name="inproj",
    )(x, ada3, norm_pre, w_main, wif_col, wif_row, bif_col, bif_row, conv_a_w, conv_qk_w)


def _log_sigmoid(x):
    return jnp.minimum(x, 0.0) - jnp.log(1.0 + jnp.exp(-jnp.abs(x)))


def _chunk_cumsum(x, axis, chunk):
    pos = lax.broadcasted_iota(jnp.int32, x.shape, axis) % chunk
    k = 1
    while k < chunk:
        x = x + jnp.where(pos >= k, pltpu.roll(x, k, axis), 0.0)
        k *= 2
    return x


def _mlstm_chunk(q, k, v, i_col, i_row, b_col, b_row, m_prev, c_aug, tril, e0):
    L = q.shape[0]
    dlog = jnp.where(tril, b_col - b_row + i_row, -jnp.inf)
    inter = b_col + m_prev
    m_t = jnp.maximum(inter, jnp.max(dlog, axis=1, keepdims=True))
    w = jnp.exp(dlog - m_t)
    s_qk = lax.dot_general(q, k, (((1,), (1,)), ((), ())), preferred_element_type=F32) * w
    a_inter = jnp.exp(inter - m_t)
    intra = jnp.dot(s_qk.astype(BF16), v, preferred_element_type=F32)
    qc = jnp.dot(q, c_aug.astype(BF16), preferred_element_type=F32)
    num = intra + a_inter * qc[:, :M_DV]
    den = jnp.sum(s_qk, axis=1, keepdims=True) + a_inter * qc[:, M_DV:M_DV + 1]
    h = num / jnp.maximum(jnp.abs(den), jnp.exp(-m_t))
    b_last = b_col[L - 1:L, :]
    g = b_last - b_col + i_col
    m_new = jnp.maximum(b_last + m_prev, jnp.max(g, axis=0, keepdims=True))
    decay = jnp.exp(b_last + m_prev - m_new)
    ws = jnp.exp(g - m_new)
    kw = (k.astype(F32) * ws).astype(BF16)
    v_aug = jnp.concatenate([v, e0], axis=1)
    upd = lax.dot_general(kw, v_aug, (((0,), (0,)), ((), ())), preferred_element_type=F32)
    return h, decay * c_aug + upd, m_new


def _mixer_kernel(x_ref, ada_ref, qk_ref, v_ref, so_ref, sga_ref, sgm_ref, ya_ref, gcol_ref, grow_ref,
                  mhw_ref, npost_ref, wa_ref, wm_ref, wo_ref, o_ref, c_st, m_st, ym_buf, *, ts, chunk):
    @pl.when(pl.program_id(1) == 0)
    def _():
        c_st[...] = jnp.zeros(c_st.shape, F32)
        m_st[...] = jnp.zeros(m_st.shape, F32)

    gcol = gcol_ref[0]
    grow = grow_ref[0]
    b_col_all = _chunk_cumsum(_log_sigmoid(gcol), 0, chunk)
    b_row_all = _chunk_cumsum(_log_sigmoid(grow), 1, chunk)

    r_i = lax.broadcasted_iota(jnp.int32, (chunk, chunk), 0)
    c_i = lax.broadcasted_iota(jnp.int32, (chunk, chunk), 1)
    tril = r_i >= c_i
    e0 = (lax.broadcasted_iota(jnp.int32, (chunk, M_DV), 1) == 0).astype(BF16)
    nqk = M_HEADS * M_DK

    for hd in range(M_HEADS):
        c_aug = c_st[hd]
        m_prev = m_st[hd][0:1, 0:1]
        for ck in range(ts // chunk):
            r0 = ck * chunk
            q = qk_ref[0, r0:r0 + chunk, hd * M_DK:(hd + 1) * M_DK]
            k = qk_ref[0, r0:r0 + chunk, nqk + hd * M_DK:nqk + (hd + 1) * M_DK]
            v = v_ref[0, r0:r0 + chunk, hd * M_DV:(hd + 1) * M_DV]
            h, c_aug, m_prev = _mlstm_chunk(
                q, k, v,
                gcol[r0:r0 + chunk, hd:hd + 1],
                grow[hd:hd + 1, r0:r0 + chunk],
                b_col_all[r0:r0 + chunk, M_HEADS + hd:M_HEADS + hd + 1],
                b_row_all[M_HEADS + hd:M_HEADS + hd + 1, r0:r0 + chunk],
                m_prev, c_aug, tril, e0)
            hn = _rms_norm(h, mhw_ref[:, hd * M_DV:(hd + 1) * M_DV])
            so = so_ref[0, r0:r0 + chunk, hd * M_DV:(hd + 1) * M_DV].astype(F32)
            ym_buf[r0:r0 + chunk, hd * M_DV:(hd + 1) * M_DV] = (so * hn).astype(BF16)
        c_st[hd] = c_aug
        m_st[hd] = jnp.broadcast_to(m_prev, m_st.shape[1:])

    mix = (sga_ref[0].astype(F32) * jnp.dot(ya_ref[0], wa_ref[...], preferred_element_type=F32)
           + sgm_ref[0].astype(F32) * jnp.dot(ym_buf[...], wm_ref[...], preferred_element_type=F32))
    y = jnp.dot(mix.astype(BF16), wo_ref[...], preferred_element_type=F32)
    o_ref[0] = x_ref[0] + ada_ref[0][2:3, :] * _rms_norm(y, npost_ref[...])


def _mixer(x, ada3, qk, v, so, sga, sgm, ya, gcol, grow, mh_norm_w, norm_post, wa, wm, wo):
    bsz, s, d = x.shape
    ts = min(TS_MIXER, s)
    ng = 2 * M_HEADS
    tok_spec = lambda w=d: pl.BlockSpec((1, ts, w), lambda b, i: (b, i, 0))
    return pl.pallas_call(
        functools.partial(_mixer_kernel, ts=ts, chunk=MLSTM_CHUNK),
        grid=(bsz, s // ts),
        in_specs=[tok_spec(),
                  pl.BlockSpec((1, 6, d), lambda b, i: (b, 0, 0)),
                  tok_spec(), tok_spec(), tok_spec(), tok_spec(), tok_spec(), tok_spec(),
                  tok_spec(LANES),
                  pl.BlockSpec((1, ng, ts), lambda b, i: (b, 0, i)),
                  _resident((1, d)), _resident((1, d)),
                  _resident(wa.shape), _resident(wm.shape), _resident(wo.shape)],
        out_specs=tok_spec(),
        out_shape=jax.ShapeDtypeStruct((bsz, s, d), F32),
        scratch_shapes=[pltpu.VMEM((M_HEADS, M_DK, 2 * M_DV), F32),
                        pltpu.VMEM((M_HEADS, 8, LANES), F32),
                        pltpu.VMEM((ts, d), BF16)],
        compiler_params=pltpu.CompilerParams(dimension_semantics=("arbitrary", "arbitrary"),
                                             vmem_limit_bytes=VMEM_LIMIT_BYTES),
        name="mixer",
    )(x, ada3, qk, v, so, sga, sgm, ya, gcol, grow, mh_norm_w, norm_post, wa, wm, wo)


def _topk_rows(s, k):
    n = s.shape[0]
    row = lax.broadcasted_iota(jnp.int32, s.shape, 0)
    rank = jnp.full(s.shape, NOT_RANKED, F32)
    vals = []
    for r in range(k):
        m = jnp.max(s, axis=0, keepdims=True)
        first = jnp.min(jnp.where(s == m, row, n), axis=0, keepdims=True)
        hit = row == first
        rank = jnp.where(hit, float(r), rank)
        s = jnp.where(hit, -jnp.inf, s)
        vals.append(m)
    return vals, rank


def _route_tables(s1, s2):
    v1, rank1 = _topk_rows(s1, P_TOPK)
    v2, rank2 = _topk_rows(s2, P_TOPK)
    v2_lo = jnp.concatenate(v2[0:8], axis=0)
    v2_all = jnp.concatenate(v2, axis=0)
    v1_hi = jnp.concatenate(v1[8:16], axis=0)
    cand = jnp.concatenate([v1[0] + v2_all] + [v1[j] + v2_lo for j in range(1, 8)] + [v1_hi + v2[0]], axis=0)
    n = cand.shape[0]
    row = lax.broadcasted_iota(jnp.int32, cand.shape, 0)
    sel = jnp.zeros(cand.shape, F32)
    cur = cand
    for _ in range(P_TOPK):
        m = jnp.max(cur, axis=0, keepdims=True)
        first = jnp.min(jnp.where(cur == m, row, n), axis=0, keepdims=True)
        hit = row == first
        sel = jnp.where(hit, 1.0, sel)
        cur = jnp.where(hit, -jnp.inf, cur)
    top = v1[0] + v2[0]
    z = jnp.sum(sel * jnp.exp(cand - top), axis=0, keepdims=True)
    counts = [jnp.sum(sel[0:16], axis=0, keepdims=True)]
    counts += [jnp.sum(sel[8 + 8 * j:16 + 8 * j], axis=0, keepdims=True) for j in range(1, 8)]
    counts += [sel[72 + j:73 + j] for j in range(8)]
    cnt = jnp.zeros(s1.shape, F32)
    for j in range(P_TOPK):
        cnt = jnp.where(rank1 == float(j), counts[j], cnt)
    p1 = jnp.exp(s1 - v1[0])
    p2 = jnp.exp(s2 - v2[0]) * (1.0 / z)
    return cnt, rank2, p1, p2


def _route_kernel(x_ref, ada_ref, npre_ref, wqt_ref, sk_ref, h2t_ref, cnt_ref, rk2_ref, p1_ref, p2_ref, qt_buf,
                  *, tm):
    ada = ada_ref[0]
    h2 = _rms_norm(x_ref[...], npre_ref[...]) * (1.0 + ada[4:5, :]) + ada[3:4, :]
    h2t = h2.T.astype(BF16)
    h2t_ref[...] = h2t
    qt_buf[...] = jnp.dot(wqt_ref[...], h2t, preferred_element_type=F32).astype(BF16)

    def head(hd, carry):
        r0 = pl.multiple_of(hd * (2 * N_KEYS), 2 * N_KEYS)
        s1 = jnp.dot(sk_ref[hd, 0], qt_buf[pl.ds(r0, N_KEYS), :], preferred_element_type=F32)
        s2 = jnp.dot(sk_ref[hd, 1], qt_buf[pl.ds(r0 + N_KEYS, N_KEYS), :], preferred_element_type=F32)
        for lc in range(tm // LANES):
            sl = slice(lc * LANES, (lc + 1) * LANES)
            cnt, rk2, p1, p2 = _route_tables(s1[:, sl], s2[:, sl])
            cnt_ref[hd, :, sl] = cnt
            rk2_ref[hd, :, sl] = rk2
            p1_ref[hd, :, sl] = p1
            p2_ref[hd, :, sl] = p2
        return carry

    lax.fori_loop(0, P_HEADS, head, 0)


def _route(x1, ada3, norm_pre, wqt, subkeys):
    t, d = x1.shape
    s = t // ada3.shape[0]
    tm = min(TM_ROUTE, s)
    tab = jax.ShapeDtypeStruct((P_HEADS, N_KEYS, t), F32)
    tab_spec = pl.BlockSpec((P_HEADS, N_KEYS, tm), lambda i: (0, 0, i))
    return pl.pallas_call(
        functools.partial(_route_kernel, tm=tm),
        grid=(t // tm,),
        in_specs=[pl.BlockSpec((tm, d), lambda i: (i, 0)),
                  pl.BlockSpec((1, 6, d), lambda i: ((i * tm) // s, 0, 0)),
                  _resident((1, d)),
                  _resident(wqt.shape),
                  _resident(subkeys.shape)],
        out_specs=[pl.BlockSpec((d, tm), lambda i: (0, i)), tab_spec, tab_spec, tab_spec, tab_spec],
        out_shape=[jax.ShapeDtypeStruct((d, t), BF16), tab, tab, tab, tab],
        scratch_shapes=[pltpu.VMEM((wqt.shape[0], tm), BF16)],
        compiler_params=pltpu.CompilerParams(dimension_semantics=("arbitrary",),
                                             vmem_limit_bytes=VMEM_LIMIT_BYTES),
        name="route",
    )(x1, ada3, norm_pre, wqt, subkeys)


def _gelu_tanh(z):
    return 0.5 * z * (1.0 + jnp.tanh(0.7978845608028654 * (z + 0.044715 * (z * z * z))))


def _peer_kernel(h2t_ref, u_ref, vt_ref, cnt_ref, rk2_ref, p1_ref, p2_ref, x_ref, ada_ref, npost_ref, o_ref,
                 acc, z_buf, a_buf, *, tm, te):
    j = pl.program_id(1)

    @pl.when(j == 0)
    def _():
        acc[...] = jnp.zeros(acc.shape, F32)

    z_buf[...] = jnp.dot(u_ref[...], h2t_ref[...], preferred_element_type=F32)
    na = te // N_KEYS

    def lane_chunk(lc, carry):
        l0 = pl.multiple_of(lc * LANES, LANES)
        for al in range(na):
            g8 = pl.multiple_of(j * na + (al // 8) * 8, 8)
            gate = jnp.zeros((N_KEYS, LANES), F32)
            for hd in range(P_HEADS):
                cnt_a = cnt_ref[hd, pl.ds(g8, 8), pl.ds(l0, LANES)][al % 8:al % 8 + 1]
                p1_a = p1_ref[hd, pl.ds(g8, 8), pl.ds(l0, LANES)][al % 8:al % 8 + 1]
                rk2 = rk2_ref[hd, :, pl.ds(l0, LANES)]
                p2 = p2_ref[hd, :, pl.ds(l0, LANES)]
                gate = gate + jnp.where(rk2 < cnt_a, p2 * p1_a, 0.0)
            z = z_buf[al * N_KEYS:(al + 1) * N_KEYS, pl.ds(l0, LANES)]
            a_buf[al * N_KEYS:(al + 1) * N_KEYS, pl.ds(l0, LANES)] = (_gelu_tanh(z) * gate).astype(BF16)
        return carry

    lax.fori_loop(0, tm // LANES, lane_chunk, 0)
    acc[...] += jnp.dot(vt_ref[...], a_buf[...], preferred_element_type=F32)

    @pl.when(j == pl.num_programs(1) - 1)
    def _():
        y = acc[...].T
        o_ref[...] = x_ref[...] + ada_ref[0][5:6, :] * _rms_norm(y, npost_ref[...])


def _peer(h2t, u_bf, vt_bf, cnt, rk2, p1, p2, x1, ada3, norm_post):
    t, d = x1.shape
    s = t // ada3.shape[0]
    ne = u_bf.shape[0]
    tm = min(TM_PEER, s)
    te = TE_PEER
    tab_spec = pl.BlockSpec((P_HEADS, N_KEYS, tm), lambda i, j: (0, 0, i))
    return pl.pallas_call(
        functools.partial(_peer_kernel, tm=tm, te=te),
        grid=(t // tm, ne // te),
        in_specs=[pl.BlockSpec((d, tm), lambda i, j: (0, i)),
                  pl.BlockSpec((te, d), lambda i, j: (j, 0)),
                  pl.BlockSpec((d, te), lambda i, j: (0, j)),
                  tab_spec, tab_spec, tab_spec, tab_spec,
                  pl.BlockSpec((tm, d), lambda i, j: (i, 0)),
                  pl.BlockSpec((1, 6, d), lambda i, j: ((i * tm) // s, 0, 0)),
                  _resident((1, d))],
        out_specs=pl.BlockSpec((tm, d), lambda i, j: (i, 0)),
        out_shape=jax.ShapeDtypeStruct((t, d), F32),
        scratch_shapes=[pltpu.VMEM((d, tm), F32), pltpu.VMEM((te, tm), F32), pltpu.VMEM((te, tm), BF16)],
        compiler_params=pltpu.CompilerParams(dimension_semantics=("arbitrary", "arbitrary"),
                                             vmem_limit_bytes=VMEM_LIMIT_BYTES),
        name="peer",
    )(h2t, u_bf, vt_bf, cnt, rk2, p1, p2, x1, ada3, norm_post)


def _layer(x, c, w_ada, b_ada, norm1_pre, norm1_post, w_in, conv_a_w, conv_qk_w, b_igate, b_fgate, mh_norm_w,
           w_branch_a, w_branch_m, w_out, norm2_pre, norm2_post, peer_wq, peer_subkeys, peer_u, peer_v):
    bsz, s, d = x.shape
    ng = 2 * M_HEADS
    ada3 = _ada(c, w_ada, b_ada).reshape(bsz, 6, d)

    o_if = 3 * d + 2 * M_HEADS * M_DK + 2 * d
    w_main = jnp.concatenate([w_in[:, :o_if], w_in[:, o_if + ng:]], axis=1).astype(BF16)
    w_if = w_in[:, o_if:o_if + ng]
    wif_col = jnp.pad(w_if, ((0, 0), (0, LANES - ng))).astype(BF16)
    wif_row = w_if.T.astype(BF16)
    b_if = jnp.concatenate([b_igate, b_fgate])
    bif_col = jnp.pad(b_if, (0, LANES - ng)).reshape(1, LANES)
    bif_row = b_if.reshape(ng, 1)

    ya, qk, v, so, sga, sgm, gcol, grow = _inproj(
        x, ada3, norm1_pre.reshape(1, d), w_main, wif_col, wif_row, bif_col, bif_row, conv_a_w, conv_qk_w)
    x1 = _mixer(x, ada3, qk, v, so, sga, sgm, ya, gcol, grow, mh_norm_w.reshape(1, d), norm1_post.reshape(1, d),
                w_branch_a.astype(BF16), w_branch_m.astype(BF16), w_out.astype(BF16))

    x1f = x1.reshape(bsz * s, d)
    h2t, cnt, rk2, p1, p2 = _route(x1f, ada3, norm2_pre.reshape(1, d), peer_wq.T.astype(BF16),
                                   peer_subkeys.astype(BF16))
    out = _peer(h2t, peer_u.astype(BF16), peer_v.T.astype(BF16), cnt, rk2, p1, p2, x1f, ada3,
                norm2_post.reshape(1, d))
    return out.reshape(bsz, s, d)


def kernel(x, c, w_ada, b_ada, norm1_pre, norm1_post, w_in, conv_a_w, conv_qk_w, b_igate, b_fgate, mh_norm_w, w_branch_a, w_branch_m, w_out, norm2_pre, norm2_post, peer_wq, peer_subkeys, peer_u, peer_v):
    for l in range(w_ada.shape[0]):
        x = _layer(x, c, w_ada[l], b_ada[l], norm1_pre[l], norm1_post[l], w_in[l], conv_a_w[l], conv_qk_w[l],
                   b_igate[l], b_fgate[l], mh_norm_w[l], w_branch_a[l], w_branch_m[l], w_out[l], norm2_pre[l],
                   norm2_post[l], peer_wq[l], peer_subkeys[l], peer_u[l], peer_v[l])
    return x
```

```python
import functools

import jax
import jax.numpy as jnp
from jax import lax
from jax.experimental import pallas as pl
from jax.experimental.pallas import tpu as pltpu

F32 = jnp.float32
BF16 = jnp.bfloat16

EPS = 1e-6
M_HEADS = 8
M_DK = 64
M_DV = 128
CONV_K = 3
QK_CONV_K = 4
P_HEADS = 8
N_KEYS = 128
P_TOPK = 16
HALO = 8
LANES = 128
MXU_K = 256
MXU_N = 256
NOT_RANKED = 99.0

VMEM_LIMIT_BYTES = 56 * 1024 * 1024

MLSTM_CHUNK = 64
TS_INPROJ = 256
TS_MIXER = 256
TM_ROUTE = 256
TM_PEER = 512
TE_PEER = 1024


def _resident(shape):
    nd = len(shape)
    return pl.BlockSpec(shape, lambda *_: (0,) * nd, pipeline_mode=pl.Buffered(1))


def _rms_norm(x, w):
    return x * lax.rsqrt(jnp.mean(x * x, axis=-1, keepdims=True) + EPS) * w


def _sigmoid(x):
    return 1.0 / (1.0 + jnp.exp(-x))


def _ada_kernel(c_ref, w_ref, b_ref, o_ref):
    c = c_ref[...]
    s = c * _sigmoid(c)
    o_ref[...] = jnp.dot(s.astype(BF16), w_ref[...].astype(BF16), preferred_element_type=F32) + b_ref[...]


def _ada(c, w_ada, b_ada):
    bsz, d = c.shape
    n = w_ada.shape[1]
    return pl.pallas_call(
        _ada_kernel,
        grid=(n // d,),
        in_specs=[pl.BlockSpec((bsz, d), lambda j: (0, 0)),
                  pl.BlockSpec((d, d), lambda j: (0, j)),
                  pl.BlockSpec((1, d), lambda j: (0, j))],
        out_specs=pl.BlockSpec((bsz, d), lambda j: (0, j)),
        out_shape=jax.ShapeDtypeStruct((bsz, n), F32),
        compiler_params=pltpu.CompilerParams(vmem_limit_bytes=VMEM_LIMIT_BYTES),
        name="ada",
    )(c, w_ada, b_ada.reshape(1, n))


def _inproj_kernel(x_ref, ada_ref, npre_ref, w_ref, wifc_ref, wifr_ref, bifc_ref, bifr_ref, cwa_ref, cwqk_ref,
                   ya_ref, qk_ref, v_ref, so_ref, sga_ref, sgm_ref, gcol_ref, grow_ref,
                   ubuf, qkbuf, *, ts, d):
    @pl.when(pl.program_id(1) == 0)
    def _():
        ubuf[0:HALO, :] = jnp.zeros((HALO, d), F32)
        qkbuf[0:HALO, :] = jnp.zeros((HALO, d), F32)

    x = x_ref[0]
    ada = ada_ref[0]
    h = _rms_norm(x, npre_ref[...]) * (1.0 + ada[1:2, :]) + ada[0:1, :]
    hb = h.astype(BF16)

    def proj(k):
        return jnp.dot(hb, w_ref[:, k * d:(k + 1) * d], preferred_element_type=F32)

    u = proj(0) * proj(2)
    ubuf[HALO:HALO + ts, :] = u
    conv = u * cwa_ref[CONV_K - 1:CONV_K, :]
    for j in range(CONV_K - 1):
        off = HALO - (CONV_K - 1) + j
        conv = conv + ubuf[off:off + ts, :] * cwa_ref[j:j + 1, :]
    ya_ref[0] = (proj(1) * conv).astype(BF16)
    ubuf[0:HALO, :] = ubuf[ts:ts + HALO, :]

    qk = proj(3)
    qkbuf[HALO:HALO + ts, :] = qk
    cq = qk * cwqk_ref[QK_CONV_K - 1:QK_CONV_K, :]
    for j in range(QK_CONV_K - 1):
        off = HALO - (QK_CONV_K - 1) + j
        cq = cq + qkbuf[off:off + ts, :] * cwqk_ref[j:j + 1, :]
    qkbuf[0:HALO, :] = qkbuf[ts:ts + HALO, :]
    cq = cq * _sigmoid(cq)
    is_q = lax.broadcasted_iota(jnp.int32, (1, d), 1) < (M_HEADS * M_DK)
    qk_ref[0] = (cq * jnp.where(is_q, M_DK ** -0.5, 1.0)).astype(BF16)

    v_ref[0] = proj(4).astype(BF16)
    so_ref[0] = _sigmoid(proj(5)).astype(BF16)
    sga_ref[0] = _sigmoid(proj(6)).astype(BF16)
    sgm_ref[0] = _sigmoid(proj(7)).astype(BF16)

    gcol_ref[0] = jnp.dot(hb, wifc_ref[...], preferred_element_type=F32) + bifc_ref[...]
    grow_ref[0] = lax.dot_general(wifr_ref[...], hb, (((1,), (1,)), ((), ())),
                                  preferred_element_type=F32) + bifr_ref[...]


def _inproj(x, ada3, norm_pre, w_main, wif_col, wif_row, bif_col, bif_row, conv_a_w, conv_qk_w):
    bsz, s, d = x.shape
    ts = min(TS_INPROJ, s)
    ng = 2 * M_HEADS
    tok = lambda dt, w=d: jax.ShapeDtypeStruct((bsz, s, w), dt)
    tok_spec = lambda w=d: pl.BlockSpec((1, ts, w), lambda b, i: (b, i, 0))
    return pl.pallas_call(
        functools.partial(_inproj_kernel, ts=ts, d=d),
        grid=(bsz, s // ts),
        in_specs=[tok_spec(),
                  pl.BlockSpec((1, 6, d), lambda b, i: (b, 0, 0)),
                  _resident((1, d)),
                  _resident(w_main.shape),
                  _resident(wif_col.shape),
                  _resident(wif_row.shape),
                  _resident(bif_col.shape),
                  _resident(bif_row.shape),
                  _resident(conv_a_w.shape),
                  _resident(conv_qk_w.shape)],
        out_specs=[tok_spec(), tok_spec(), tok_spec(), tok_spec(), tok_spec(), tok_spec(),
                   tok_spec(LANES),
                   pl.BlockSpec((1, ng, ts), lambda b, i: (b, 0, i))],
        out_shape=[tok(BF16), tok(BF16), tok(BF16), tok(BF16), tok(BF16), tok(BF16),
                   tok(F32, LANES),
                   jax.ShapeDtypeStruct((bsz, ng, s), F32)],
        scratch_shapes=[pltpu.VMEM((HALO + ts, d), F32), pltpu.VMEM((HALO + ts, d), F32)],
        compiler_params=pltpu.CompilerParams(dimension_semantics=("arbitrary", "arbitrary"),
                                             vmem_limit_bytes=VMEM_LIMIT_BYTES),
        name="inproj",
    )(x, ada3, norm_pre, w_main, wif_col, wif_row, bif_col, bif_row, conv_a_w, conv_qk_w)


def _log_sigmoid(x):
    return jnp.minimum(x, 0.0) - jnp.log(1.0 + jnp.exp(-jnp.abs(x)))


def _chunk_cumsum(x, axis, chunk):
    pos = lax.broadcasted_iota(jnp.int32, x.shape, axis) % chunk
    k = 1
    while k < chunk:
        x = x + jnp.where(pos >= k, pltpu.roll(x, k, axis), 0.0)
        k *= 2
    return x


def _mlstm_chunk(q, k, v, i_col, i_row, b_col, b_row, m_prev, c_aug, tril, e0):
    L = q.shape[0]
    dlog = jnp.where(tril, b_col - b_row + i_row, -jnp.inf)
    inter = b_col + m_prev
    m_t = jnp.maximum(inter, jnp.max(dlog, axis=1, keepdims=True))
    w = jnp.exp(dlog - m_t)
    s_qk = lax.dot_general(q, k, (((1,), (1,)), ((), ())), preferred_element_type=F32) * w
    a_inter = jnp.exp(inter - m_t)
    intra = jnp.dot(s_qk.astype(BF16), v, preferred_element_type=F32)
    qc = jnp.dot(q, c_aug.astype(BF16), preferred_element_type=F32)
    num = intra + a_inter * qc[:, :M_DV]
    den = jnp.sum(s_qk, axis=1, keepdims=True) + a_inter * qc[:, M_DV:M_DV + 1]
    h = num / jnp.maximum(jnp.abs(den), jnp.exp(-m_t))
    b_last = b_col[L - 1:L, :]
    g = b_last - b_col + i_col
    m_new = jnp.maximum(b_last + m_prev, jnp.max(g, axis=0, keepdims=True))
    decay = jnp.exp(b_last + m_prev - m_new)
    ws = jnp.exp(g - m_new)
    kw = (k.astype(F32) * ws).astype(BF16)
    v_aug = jnp.concatenate([v, e0], axis=1)
    upd = lax.dot_general(kw, v_aug, (((0,), (0,)), ((), ())), preferred_element_type=F32)
    return h, decay * c_aug + upd, m_new


def _mixer_kernel(x_ref, ada_ref, qk_ref, v_ref, so_ref, sga_ref, sgm_ref, ya_ref, gcol_ref, grow_ref,
                  mhw_ref, npost_ref, wa_ref, wm_ref, wo_ref, o_ref, c_st, m_st, ym_buf, *, ts, chunk):
    @pl.when(pl.program_id(1) == 0)
    def _():
        c_st[...] = jnp.zeros(c_st.shape, F32)
        m_st[...] = jnp.zeros(m_st.shape, F32)

    gcol = gcol_ref[0]
    grow = grow_ref[0]
    b_col_all = _chunk_cumsum(_log_sigmoid(gcol), 0, chunk)
    b_row_all = _chunk_cumsum(_log_sigmoid(grow), 1, chunk)

    r_i = lax.broadcasted_iota(jnp.int32, (chunk, chunk), 0)
    c_i = lax.broadcasted_iota(jnp.int32, (chunk, chunk), 1)
    tril = r_i >= c_i
    e0 = (lax.broadcasted_iota(jnp.int32, (chunk, M_DV), 1) == 0).astype(BF16)
    nqk = M_HEADS * M_DK

    for hd in range(M_HEADS):
        c_aug = c_st[hd]
        m_prev = m_st[hd][0:1, 0:1]
        for ck in range(ts // chunk):
            r0 = ck * chunk
            q = qk_ref[0, r0:r0 + chunk, hd * M_DK:(hd + 1) * M_DK]
            k = qk_ref[0, r0:r0 + chunk, nqk + hd * M_DK:nqk + (hd + 1) * M_DK]
            v = v_ref[0, r0:r0 + chunk, hd * M_DV:(hd + 1) * M_DV]
            h, c_aug, m_prev = _mlstm_chunk(
                q, k, v,
                gcol[r0:r0 + chunk, hd:hd + 1],
                grow[hd:hd + 1, r0:r0 + chunk],
                b_col_all[r0:r0 + chunk, M_HEADS + hd:M_HEADS + hd + 1],
                b_row_all[M_HEADS + hd:M_HEADS + hd + 1, r0:r0 + chunk],
                m_prev, c_aug, tril, e0)
            hn = _rms_norm(h, mhw_ref[:, hd * M_DV:(hd + 1) * M_DV])
            so = so_ref[0, r0:r0 + chunk, hd * M_DV:(hd + 1) * M_DV].astype(F32)
            ym_buf[r0:r0 + chunk, hd * M_DV:(hd + 1) * M_DV] = (so * hn).astype(BF16)
        c_st[hd] = c_aug
        m_st[hd] = jnp.broadcast_to(m_prev, m_st.shape[1:])

    mix = (sga_ref[0].astype(F32) * jnp.dot(ya_ref[0], wa_ref[...], preferred_element_type=F32)
           + sgm_ref[0].astype(F32) * jnp.dot(ym_buf[...], wm_ref[...], preferred_element_type=F32))
    y = jnp.dot(mix.astype(BF16), wo_ref[...], preferred_element_type=F32)
    o_ref[0] = x_ref[0] + ada_ref[0][2:3, :] * _rms_norm(y, npost_ref[...])


def _mixer(x, ada3, qk, v, so, sga, sgm, ya, gcol, grow, mh_norm_w, norm_post, wa, wm, wo):
    bsz, s, d = x.shape
    ts = min(TS_MIXER, s)
    ng = 2 * M_HEADS
    tok_spec = lambda w=d: pl.BlockSpec((1, ts, w), lambda b, i: (b, i, 0))
    return pl.pallas_call(
        functools.partial(_mixer_kernel, ts=ts, chunk=MLSTM_CHUNK),
        grid=(bsz, s // ts),
        in_specs=[tok_spec(),
                  pl.BlockSpec((1, 6, d), lambda b, i: (b, 0, 0)),
                  tok_spec(), tok_spec(), tok_spec(), tok_spec(), tok_spec(), tok_spec(),
                  tok_spec(LANES),
                  pl.BlockSpec((1, ng, ts), lambda b, i: (b, 0, i)),
                  _resident((1, d)), _resident((1, d)),
                  _resident(wa.shape), _resident(wm.shape), _resident(wo.shape)],
        out_specs=tok_spec(),
        out_shape=jax.ShapeDtypeStruct((bsz, s, d), F32),
        scratch_shapes=[pltpu.VMEM((M_HEADS, M_DK, 2 * M_DV), F32),
                        pltpu.VMEM((M_HEADS, 8, LANES), F32),
                        pltpu.VMEM((ts, d), BF16)],
        compiler_params=pltpu.CompilerParams(dimension_semantics=("arbitrary", "arbitrary"),
                                             vmem_limit_bytes=VMEM_LIMIT_BYTES),
        name="mixer",
    )(x, ada3, qk, v, so, sga, sgm, ya, gcol, grow, mh_norm_w, norm_post, wa, wm, wo)


def _topk_rows(s, k):
    n = s.shape[0]
    row = lax.broadcasted_iota(jnp.int32, s.shape, 0)
    rank = jnp.full(s.shape, NOT_RANKED, F32)
    vals = []
    for r in range(k):
        m = jnp.max(s, axis=0, keepdims=True)
        first = jnp.min(jnp.where(s == m, row, n), axis=0, keepdims=True)
        hit = row == first
        rank = jnp.where(hit, float(r), rank)
        s = jnp.where(hit, -jnp.inf, s)
        vals.append(m)
    return vals, rank


def _route_tables(s1, s2):
    v1, rank1 = _topk_rows(s1, P_TOPK)
    v2, rank2 = _topk_rows(s2, P_TOPK)
    v2_lo = jnp.concatenate(v2[0:8], axis=0)
    v2_all = jnp.concatenate(v2, axis=0)
    v1_hi = jnp.concatenate(v1[8:16], axis=0)
    cand = jnp.concatenate([v1[0] + v2_all] + [v1[j] + v2_lo for j in range(1, 8)] + [v1_hi + v2[0]], axis=0)
    n = cand.shape[0]
    row = lax.broadcasted_iota(jnp.int32, cand.shape, 0)
    sel = jnp.zeros(cand.shape, F32)
    cur = cand
    for _ in range(P_TOPK):
        m = jnp.max(cur, axis=0, keepdims=True)
        first = jnp.min(jnp.where(cur == m, row, n), axis=0, keepdims=True)
        hit = row == first
        sel = jnp.where(hit, 1.0, sel)
        cur = jnp.where(hit, -jnp.inf, cur)
    top = v1[0] + v2[0]
    z = jnp.sum(sel * jnp.exp(cand - top), axis=0, keepdims=True)
    counts = [jnp.sum(sel[0:16], axis=0, keepdims=True)]
    counts += [jnp.sum(sel[8 + 8 * j:16 + 8 * j], axis=0, keepdims=True) for j in range(1, 8)]
    counts += [sel[72 + j:73 + j] for j in range(8)]
    cnt = jnp.zeros(s1.shape, F32)
    for j in range(P_TOPK):
        cnt = jnp.where(rank1 == float(j), counts[j], cnt)
    p1 = jnp.exp(s1 - v1[0])
    p2 = jnp.exp(s2 - v2[0]) * (1.0 / z)
    return cnt, rank2, p1, p2


def _route_kernel(x_ref, ada_ref, npre_ref, wqt_ref, sk_ref, h2t_ref, cnt_ref, rk2_ref, p1_ref, p2_ref, qt_buf,
                  *, tm):
    ada = ada_ref[0]
    h2 = _rms_norm(x_ref[...], npre_ref[...]) * (1.0 + ada[4:5, :]) + ada[3:4, :]
    h2t = h2.T.astype(BF16)
    h2t_ref[...] = h2t
    qt_buf[...] = jnp.dot(wqt_ref[...], h2t, preferred_element_type=F32).astype(BF16)

    def head(hd, carry):
        r0 = pl.multiple_of(hd * (2 * N_KEYS), 2 * N_KEYS)
        s1 = jnp.dot(sk_ref[hd, 0], qt_buf[pl.ds(r0, N_KEYS), :], preferred_element_type=F32)
        s2 = jnp.dot(sk_ref[hd, 1], qt_buf[pl.ds(r0 + N_KEYS, N_KEYS), :], preferred_element_type=F32)
        for lc in range(tm // LANES):
            sl = slice(lc * LANES, (lc + 1) * LANES)
            cnt, rk2, p1, p2 = _route_tables(s1[:, sl], s2[:, sl])
            cnt_ref[hd, lc] = cnt
            rk2_ref[hd, lc] = rk2
            p1_ref[hd, lc] = p1
            p2_ref[hd, lc] = p2
        return carry

    lax.fori_loop(0, P_HEADS, head, 0)


def _route(x1, ada3, norm_pre, wqt, subkeys):
    t, d = x1.shape
    s = t // ada3.shape[0]
    tm = min(TM_ROUTE, s)
    tab = jax.ShapeDtypeStruct((P_HEADS, t // LANES, N_KEYS, LANES), F32)
    tab_spec = pl.BlockSpec((P_HEADS, tm // LANES, N_KEYS, LANES), lambda i: (0, i, 0, 0))
    return pl.pallas_call(
        functools.partial(_route_kernel, tm=tm),
        grid=(t // tm,),
        in_specs=[pl.BlockSpec((tm, d), lambda i: (i, 0)),
                  pl.BlockSpec((1, 6, d), lambda i: ((i * tm) // s, 0, 0)),
                  _resident((1, d)),
                  _resident(wqt.shape),
                  _resident(subkeys.shape)],
        out_specs=[pl.BlockSpec((d, tm), lambda i: (0, i)), tab_spec, tab_spec, tab_spec, tab_spec],
        out_shape=[jax.ShapeDtypeStruct((d, t), BF16), tab, tab, tab, tab],
        scratch_shapes=[pltpu.VMEM((wqt.shape[0], tm), BF16)],
        compiler_params=pltpu.CompilerParams(dimension_semantics=("arbitrary",),
                                             vmem_limit_bytes=VMEM_LIMIT_BYTES),
        name="route",
    )(x1, ada3, norm_pre, wqt, subkeys)


def _gelu_tanh(z):
    return 0.5 * z * (1.0 + jnp.tanh(0.7978845608028654 * (z + 0.044715 * (z * z * z))))


def _peer_kernel(h2t_ref, u_ref, vt_ref, cnt_ref, rk2_ref, p1_ref, p2_ref, x_ref, ada_ref, npost_ref, o_ref,
                 acc, h2t_s, g_buf, z_buf, a_buf, *, tm, te):
    j = pl.program_id(1)

    @pl.when(j == 0)
    def _():
        acc[...] = jnp.zeros(acc.shape, F32)
        h2t_s[...] = h2t_ref[...]

    na = te // N_KEYS
    apc = MXU_K // N_KEYS
    nkc = te // MXU_K
    nlc = tm // LANES
    nlh = tm // MXU_N
    d = acc.shape[0]
    rh = d // 2

    def gate_block(kc, ac, lc):
        al = kc * apc + ac
        g8 = pl.multiple_of(j * na + (al // 8) * 8, 8)
        gate = jnp.zeros((N_KEYS, LANES), F32)
        for hd in range(P_HEADS):
            cnt_a = cnt_ref[hd, lc, pl.ds(g8, 8), :][al % 8:al % 8 + 1]
            p1_a = p1_ref[hd, lc, pl.ds(g8, 8), :][al % 8:al % 8 + 1]
            hit = rk2_ref[hd, lc] < cnt_a
            gate = gate + jnp.where(hit, p2_ref[hd, lc] * p1_a, 0.0)
        g_buf[kc % 2, ac, lc] = gate

    def score_piece(kc, ac, lh, after=None):
        r0 = kc * MXU_K + ac * N_KEYS
        if after is not None:
            ga, gl = after
            bits = pltpu.bitcast(g_buf[(kc % 2), ga, gl, 0:16, :], jnp.uint32)
            zero = pltpu.bitcast(lax.shift_right_logical(bits, jnp.uint32(32)), F32)
            h2t_s[0:16, lh * MXU_N:lh * MXU_N + LANES] += zero.astype(BF16)
        z_buf[kc % 2, ac * N_KEYS:(ac + 1) * N_KEYS, lh * MXU_N:(lh + 1) * MXU_N] = jnp.dot(
            u_ref[r0:r0 + N_KEYS, :], h2t_s[:, lh * MXU_N:(lh + 1) * MXU_N], preferred_element_type=F32)

    def act_block(kc, ac, lc):
        r0 = kc * MXU_K + ac * N_KEYS
        zb = z_buf[kc % 2, ac * N_KEYS:(ac + 1) * N_KEYS, lc * LANES:(lc + 1) * LANES]
        a_buf[r0:r0 + N_KEYS, lc * LANES:(lc + 1) * LANES] = (_gelu_tanh(zb) * g_buf[kc % 2, ac, lc]).astype(BF16)

    outs = {}

    def out_piece(kc, r, lh):
        e0 = kc * MXU_K
        part = jnp.dot(vt_ref[r * rh:(r + 1) * rh, e0:e0 + MXU_K],
                       a_buf[e0:e0 + MXU_K, lh * MXU_N:(lh + 1) * MXU_N], preferred_element_type=F32)
        outs[(r, lh)] = part if (r, lh) not in outs else outs[(r, lh)] + part

    blocks = [(ac, lc) for lc in range(nlc) for ac in range(apc)]
    pieces = [(ac, lh) for lh in range(nlh) for ac in range(apc)]
    for s, blk in enumerate(blocks):
        gate_block(0, *blk)
        if s % 2 == 1:
            score_piece(0, *pieces[s // 2], after=blk)
    pending = []
    for kc in range(nkc):
        mxu = list(pending)
        pending = []
        if kc + 1 < nkc:
            mxu += [functools.partial(score_piece, kc + 1, ac, lh) for ac, lh in pieces]
        for s, (ac, lc) in enumerate(blocks):
            if kc + 1 < nkc:
                gate_block(kc + 1, ac, lc)
            act_block(kc, ac, lc)
            if mxu:
                f = mxu.pop(0)
                f(after=(ac, lc)) if f.func is score_piece else f()
            if ac == apc - 1 and (lc + 1) % (MXU_N // LANES) == 0:
                lh = lc // (MXU_N // LANES)
                ready = [functools.partial(out_piece, kc, r, lh) for r in range(d // rh)]
                if lh + 1 < nlh:
                    mxu += ready
                else:
                    pending = ready
        for f in mxu:
            f()
    for f in pending:
        f()
    for (r, lh), val in outs.items():
        acc[r * rh:(r + 1) * rh, lh * MXU_N:(lh + 1) * MXU_N] += val

    @pl.when(j == pl.num_programs(1) - 1)
    def _():
        y = acc[...].T
        o_ref[...] = x_ref[...] + ada_ref[0][5:6, :] * _rms_norm(y, npost_ref[...])


def _peer(h2t, u_bf, vt_bf, cnt, rk2, p1, p2, x1, ada3, norm_post):
    t, d = x1.shape
    s = t // ada3.shape[0]
    ne = u_bf.shape[0]
    tm = min(TM_PEER, s)
    te = TE_PEER
    tab_spec = pl.BlockSpec((P_HEADS, tm // LANES, N_KEYS, LANES), lambda i, j: (0, i, 0, 0))
    return pl.pallas_call(
        functools.partial(_peer_kernel, tm=tm, te=te),
        grid=(t // tm, ne // te),
        in_specs=[pl.BlockSpec((d, tm), lambda i, j: (0, i)),
                  pl.BlockSpec((te, d), lambda i, j: (j, 0)),
                  pl.BlockSpec((d, te), lambda i, j: (0, j)),
                  tab_spec, tab_spec, tab_spec, tab_spec,
                  pl.BlockSpec((tm, d), lambda i, j: (i, 0)),
                  pl.BlockSpec((1, 6, d), lambda i, j: ((i * tm) // s, 0, 0)),
                  _resident((1, d))],
        out_specs=pl.BlockSpec((tm, d), lambda i, j: (i, 0)),
        out_shape=jax.ShapeDtypeStruct((t, d), F32),
        scratch_shapes=[pltpu.VMEM((d, tm), F32),
                        pltpu.VMEM((d, tm), BF16),
                        pltpu.VMEM((2, MXU_K // N_KEYS, tm // LANES, N_KEYS, LANES), F32),
                        pltpu.VMEM((2, MXU_K, tm), F32),
                        pltpu.VMEM((te, tm), BF16)],
        compiler_params=pltpu.CompilerParams(dimension_semantics=("arbitrary", "arbitrary"),
                                             vmem_limit_bytes=VMEM_LIMIT_BYTES),
        name="peer",
    )(h2t, u_bf, vt_bf, cnt, rk2, p1, p2, x1, ada3, norm_post)


def _layer(x, c, w_ada, b_ada, norm1_pre, norm1_post, w_in, conv_a_w, conv_qk_w, b_igate, b_fgate, mh_norm_w,
           w_branch_a, w_branch_m, w_out, norm2_pre, norm2_post, peer_wq, peer_subkeys, peer_u, peer_v):
    bsz, s, d = x.shape
    ng = 2 * M_HEADS
    ada3 = _ada(c, w_ada, b_ada).reshape(bsz, 6, d)

    o_if = 3 * d + 2 * M_HEADS * M_DK + 2 * d
    w_main = jnp.concatenate([w_in[:, :o_if], w_in[:, o_if + ng:]], axis=1).astype(BF16)
    w_if = w_in[:, o_if:o_if + ng]
    wif_col = jnp.pad(w_if, ((0, 0), (0, LANES - ng))).astype(BF16)
    wif_row = w_if.T.astype(BF16)
    b_if = jnp.concatenate([b_igate, b_fgate])
    bif_col = jnp.pad(b_if, (0, LANES - ng)).reshape(1, LANES)
    bif_row = b_if.reshape(ng, 1)

    ya, qk, v, so, sga, sgm, gcol, grow = _inproj(
        x, ada3, norm1_pre.reshape(1, d), w_main, wif_col, wif_row, bif_col, bif_row, conv_a_w, conv_qk_w)
    x1 = _mixer(x, ada3, qk, v, so, sga, sgm, ya, gcol, grow, mh_norm_w.reshape(1, d), norm1_post.reshape(1, d),
                w_branch_a.astype(BF16), w_branch_m.astype(BF16), w_out.astype(BF16))

    x1f = x1.reshape(bsz * s, d)
    h2t, cnt, rk2, p1, p2 = _route(x1f, ada3, norm2_pre.reshape(1, d), peer_wq.T.astype(BF16),
                                   peer_subkeys.astype(BF16))
    out = _peer(h2t, peer_u.astype(BF16), peer_v.T.astype(BF16), cnt, rk2, p1, p2, x1f, ada3,
                norm2_post.reshape(1, d))
    return out.reshape(bsz, s, d)


def kernel(x, c, w_ada, b_ada, norm1_pre, norm1_post, w_in, conv_a_w, conv_qk_w, b_igate, b_fgate, mh_norm_w, w_branch_a, w_branch_m, w_out, norm2_pre, norm2_post, peer_wq, peer_subkeys, peer_u, peer_v):
    for l in range(w_ada.shape[0]):
        x = _layer(x, c, w_ada[l], b_ada[l], norm1_pre[l], norm1_post[l], w_in[l], conv_a_w[l], conv_qk_w[l],
                   b_igate[l], b_fgate[l], mh_norm_w[l], w_branch_a[l], w_branch_m[l], w_out[l], norm2_pre[l],
                   norm2_post[l], peer_wq[l], peer_subkeys[l], peer_u[l], peer_v[l])
    return x
```

```python
import functools

import jax
import jax.numpy as jnp
from jax import lax
from jax.experimental import pallas as pl
from jax.experimental.pallas import tpu as pltpu

F32 = jnp.float32
BF16 = jnp.bfloat16

EPS = 1e-6
M_HEADS = 8
M_DK = 64
M_DV = 128
CONV_K = 3
QK_CONV_K = 4
P_HEADS = 8
N_KEYS = 128
P_TOPK = 16
HALO = 8
LANES = 128
MXU_K = 256
MXU_N = 256
BF16_ROWS = 16
NOT_RANKED = 99.0

VMEM_LIMIT_BYTES = 56 * 1024 * 1024

MLSTM_CHUNK = 64
TS_INPROJ = 256
TS_MIXER = 256
TM_ROUTE = 256
TM_PEER = 1024
TE_PEER = 1024


def _resident(shape):
    nd = len(shape)
    return pl.BlockSpec(shape, lambda *_: (0,) * nd, pipeline_mode=pl.Buffered(1))


def _rms_norm(x, w):
    return x * lax.rsqrt(jnp.mean(x * x, axis=-1, keepdims=True) + EPS) * w


def _sigmoid(x):
    return 1.0 / (1.0 + jnp.exp(-x))


def _ada_kernel(c_ref, w_ref, b_ref, o_ref):
    c = c_ref[...]
    s = c * _sigmoid(c)
    o_ref[...] = jnp.dot(s.astype(BF16), w_ref[...].astype(BF16), preferred_element_type=F32) + b_ref[...]


def _ada(c, w_ada, b_ada):
    bsz, d = c.shape
    n = w_ada.shape[1]
    return pl.pallas_call(
        _ada_kernel,
        grid=(n // d,),
        in_specs=[pl.BlockSpec((bsz, d), lambda j: (0, 0)),
                  pl.BlockSpec((d, d), lambda j: (0, j)),
                  pl.BlockSpec((1, d), lambda j: (0, j))],
        out_specs=pl.BlockSpec((bsz, d), lambda j: (0, j)),
        out_shape=jax.ShapeDtypeStruct((bsz, n), F32),
        compiler_params=pltpu.CompilerParams(vmem_limit_bytes=VMEM_LIMIT_BYTES),
        name="ada",
    )(c, w_ada, b_ada.reshape(1, n))


def _inproj_kernel(x_ref, ada_ref, npre_ref, w_ref, wifc_ref, wifr_ref, bifc_ref, bifr_ref, cwa_ref, cwqk_ref,
                   ya_ref, qk_ref, v_ref, so_ref, sga_ref, sgm_ref, gcol_ref, grow_ref,
                   ubuf, qkbuf, *, ts, d):
    @pl.when(pl.program_id(1) == 0)
    def _():
        ubuf[0:HALO, :] = jnp.zeros((HALO, d), F32)
        qkbuf[0:HALO, :] = jnp.zeros((HALO, d), F32)

    x = x_ref[0]
    ada = ada_ref[0]
    h = _rms_norm(x, npre_ref[...]) * (1.0 + ada[1:2, :]) + ada[0:1, :]
    hb = h.astype(BF16)

    def proj(k):
        return jnp.dot(hb, w_ref[:, k * d:(k + 1) * d], preferred_element_type=F32)

    u = proj(0) * proj(2)
    ubuf[HALO:HALO + ts, :] = u
    conv = u * cwa_ref[CONV_K - 1:CONV_K, :]
    for j in range(CONV_K - 1):
        off = HALO - (CONV_K - 1) + j
        conv = conv + ubuf[off:off + ts, :] * cwa_ref[j:j + 1, :]
    ya_ref[0] = (proj(1) * conv).astype(BF16)
    ubuf[0:HALO, :] = ubuf[ts:ts + HALO, :]

    qk = proj(3)
    qkbuf[HALO:HALO + ts, :] = qk
    cq = qk * cwqk_ref[QK_CONV_K - 1:QK_CONV_K, :]
    for j in range(QK_CONV_K - 1):
        off = HALO - (QK_CONV_K - 1) + j
        cq = cq + qkbuf[off:off + ts, :] * cwqk_ref[j:j + 1, :]
    qkbuf[0:HALO, :] = qkbuf[ts:ts + HALO, :]
    cq = cq * _sigmoid(cq)
    is_q = lax.broadcasted_iota(jnp.int32, (1, d), 1) < (M_HEADS * M_DK)
    qk_ref[0] = (cq * jnp.where(is_q, M_DK ** -0.5, 1.0)).astype(BF16)

    v_ref[0] = proj(4).astype(BF16)
    so_ref[0] = _sigmoid(proj(5)).astype(BF16)
    sga_ref[0] = _sigmoid(proj(6)).astype(BF16)
    sgm_ref[0] = _sigmoid(proj(7)).astype(BF16)

    gcol_ref[0] = jnp.dot(hb, wifc_ref[...], preferred_element_type=F32) + bifc_ref[...]
    grow_ref[0] = lax.dot_general(wifr_ref[...], hb, (((1,), (1,)), ((), ())),
                                  preferred_element_type=F32) + bifr_ref[...]


def _inproj(x, ada3, norm_pre, w_main, wif_col, wif_row, bif_col, bif_row, conv_a_w, conv_qk_w):
    bsz, s, d = x.shape
    ts = min(TS_INPROJ, s)
    ng = 2 * M_HEADS
    tok = lambda dt, w=d: jax.ShapeDtypeStruct((bsz, s, w), dt)
    tok_spec = lambda w=d: pl.BlockSpec((1, ts, w), lambda b, i: (b, i, 0))
    return pl.pallas_call(
        functools.partial(_inproj_kernel, ts=ts, d=d),
        grid=(bsz, s // ts),
        in_specs=[tok_spec(),
                  pl.BlockSpec((1, 6, d), lambda b, i: (b, 0, 0)),
                  _resident((1, d)),
                  _resident(w_main.shape),
                  _resident(wif_col.shape),
                  _resident(wif_row.shape),
                  _resident(bif_col.shape),
                  _resident(bif_row.shape),
                  _resident(conv_a_w.shape),
                  _resident(conv_qk_w.shape)],
        out_specs=[tok_spec(), tok_spec(), tok_spec(), tok_spec(), tok_spec(), tok_spec(),
                   tok_spec(LANES),
                   pl.BlockSpec((1, ng, ts), lambda b, i: (b, 0, i))],
        out_shape=[tok(BF16), tok(BF16), tok(BF16), tok(BF16), tok(BF16), tok(BF16),
                   tok(F32, LANES),
                   jax.ShapeDtypeStruct((bsz, ng, s), F32)],
        scratch_shapes=[pltpu.VMEM((HALO + ts, d), F32), pltpu.VMEM((HALO + ts, d), F32)],
        compiler_params=pltpu.CompilerParams(dimension_semantics=("arbitrary", "arbitrary"),
                                             vmem_limit_bytes=VMEM_LIMIT_BYTES),
        name="inproj",
    )(x, ada3, norm_pre, w_main, wif_col, wif_row, bif_col, bif_row, conv_a_w, conv_qk_w)


def _log_sigmoid(x):
    return jnp.minimum(x, 0.0) - jnp.log(1.0 + jnp.exp(-jnp.abs(x)))


def _chunk_cumsum(x, axis, chunk):
    pos = lax.broadcasted_iota(jnp.int32, x.shape, axis) % chunk
    k = 1
    while k < chunk:
        x = x + jnp.where(pos >= k, pltpu.roll(x, k, axis), 0.0)
        k *= 2
    return x


def _mlstm_chunk(q, k, v, i_col, i_row, b_col, b_row, m_prev, c_aug, tril, e0):
    L = q.shape[0]
    dlog = jnp.where(tril, b_col - b_row + i_row, -jnp.inf)
    inter = b_col + m_prev
    m_t = jnp.maximum(inter, jnp.max(dlog, axis=1, keepdims=True))
    w = jnp.exp(dlog - m_t)
    s_qk = lax.dot_general(q, k, (((1,), (1,)), ((), ())), preferred_element_type=F32) * w
    a_inter = jnp.exp(inter - m_t)
    intra = jnp.dot(s_qk.astype(BF16), v, preferred_element_type=F32)
    qc = jnp.dot(q, c_aug.astype(BF16), preferred_element_type=F32)
    num = intra + a_inter * qc[:, :M_DV]
    den = jnp.sum(s_qk, axis=1, keepdims=True) + a_inter * qc[:, M_DV:M_DV + 1]
    h = num / jnp.maximum(jnp.abs(den), jnp.exp(-m_t))
    b_last = b_col[L - 1:L, :]
    g = b_last - b_col + i_col
    m_new = jnp.maximum(b_last + m_prev, jnp.max(g, axis=0, keepdims=True))
    decay = jnp.exp(b_last + m_prev - m_new)
    ws = jnp.exp(g - m_new)
    kw = (k.astype(F32) * ws).astype(BF16)
    v_aug = jnp.concatenate([v, e0], axis=1)
    upd = lax.dot_general(kw, v_aug, (((0,), (0,)), ((), ())), preferred_element_type=F32)
    return h, decay * c_aug + upd, m_new


def _mixer_kernel(x_ref, ada_ref, qk_ref, v_ref, so_ref, sga_ref, sgm_ref, ya_ref, gcol_ref, grow_ref,
                  mhw_ref, npost_ref, wa_ref, wm_ref, wo_ref, o_ref, c_st, m_st, ym_buf, *, ts, chunk):
    @pl.when(pl.program_id(1) == 0)
    def _():
        c_st[...] = jnp.zeros(c_st.shape, F32)
        m_st[...] = jnp.zeros(m_st.shape, F32)

    gcol = gcol_ref[0]
    grow = grow_ref[0]
    b_col_all = _chunk_cumsum(_log_sigmoid(gcol), 0, chunk)
    b_row_all = _chunk_cumsum(_log_sigmoid(grow), 1, chunk)

    r_i = lax.broadcasted_iota(jnp.int32, (chunk, chunk), 0)
    c_i = lax.broadcasted_iota(jnp.int32, (chunk, chunk), 1)
    tril = r_i >= c_i
    e0 = (lax.broadcasted_iota(jnp.int32, (chunk, M_DV), 1) == 0).astype(BF16)
    nqk = M_HEADS * M_DK

    for hd in range(M_HEADS):
        c_aug = c_st[hd]
        m_prev = m_st[hd][0:1, 0:1]
        for ck in range(ts // chunk):
            r0 = ck * chunk
            q = qk_ref[0, r0:r0 + chunk, hd * M_DK:(hd + 1) * M_DK]
            k = qk_ref[0, r0:r0 + chunk, nqk + hd * M_DK:nqk + (hd + 1) * M_DK]
            v = v_ref[0, r0:r0 + chunk, hd * M_DV:(hd + 1) * M_DV]
            h, c_aug, m_prev = _mlstm_chunk(
                q, k, v,
                gcol[r0:r0 + chunk, hd:hd + 1],
                grow[hd:hd + 1, r0:r0 + chunk],
                b_col_all[r0:r0 + chunk, M_HEADS + hd:M_HEADS + hd + 1],
                b_row_all[M_HEADS + hd:M_HEADS + hd + 1, r0:r0 + chunk],
                m_prev, c_aug, tril, e0)
            hn = _rms_norm(h, mhw_ref[:, hd * M_DV:(hd + 1) * M_DV])
            so = so_ref[0, r0:r0 + chunk, hd * M_DV:(hd + 1) * M_DV].astype(F32)
            ym_buf[r0:r0 + chunk, hd * M_DV:(hd + 1) * M_DV] = (so * hn).astype(BF16)
        c_st[hd] = c_aug
        m_st[hd] = jnp.broadcast_to(m_prev, m_st.shape[1:])

    mix = (sga_ref[0].astype(F32) * jnp.dot(ya_ref[0], wa_ref[...], preferred_element_type=F32)
           + sgm_ref[0].astype(F32) * jnp.dot(ym_buf[...], wm_ref[...], preferred_element_type=F32))
    y = jnp.dot(mix.astype(BF16), wo_ref[...], preferred_element_type=F32)
    o_ref[0] = x_ref[0] + ada_ref[0][2:3, :] * _rms_norm(y, npost_ref[...])


def _mixer(x, ada3, qk, v, so, sga, sgm, ya, gcol, grow, mh_norm_w, norm_post, wa, wm, wo):
    bsz, s, d = x.shape
    ts = min(TS_MIXER, s)
    ng = 2 * M_HEADS
    tok_spec = lambda w=d: pl.BlockSpec((1, ts, w), lambda b, i: (b, i, 0))
    return pl.pallas_call(
        functools.partial(_mixer_kernel, ts=ts, chunk=MLSTM_CHUNK),
        grid=(bsz, s // ts),
        in_specs=[tok_spec(),
                  pl.BlockSpec((1, 6, d), lambda b, i: (b, 0, 0)),
                  tok_spec(), tok_spec(), tok_spec(), tok_spec(), tok_spec(), tok_spec(),
                  tok_spec(LANES),
                  pl.BlockSpec((1, ng, ts), lambda b, i: (b, 0, i)),
                  _resident((1, d)), _resident((1, d)),
                  _resident(wa.shape), _resident(wm.shape), _resident(wo.shape)],
        out_specs=tok_spec(),
        out_shape=jax.ShapeDtypeStruct((bsz, s, d), F32),
        scratch_shapes=[pltpu.VMEM((M_HEADS, M_DK, 2 * M_DV), F32),
                        pltpu.VMEM((M_HEADS, 8, LANES), F32),
                        pltpu.VMEM((ts, d), BF16)],
        compiler_params=pltpu.CompilerParams(dimension_semantics=("arbitrary", "arbitrary"),
                                             vmem_limit_bytes=VMEM_LIMIT_BYTES),
        name="mixer",
    )(x, ada3, qk, v, so, sga, sgm, ya, gcol, grow, mh_norm_w, norm_post, wa, wm, wo)


def _topk_rows(s, k):
    n = s.shape[0]
    row = lax.broadcasted_iota(jnp.int32, s.shape, 0)
    rank = jnp.full(s.shape, NOT_RANKED, F32)
    vals = []
    for r in range(k):
        m = jnp.max(s, axis=0, keepdims=True)
        first = jnp.min(jnp.where(s == m, row, n), axis=0, keepdims=True)
        hit = row == first
        rank = jnp.where(hit, float(r), rank)
        s = jnp.where(hit, -jnp.inf, s)
        vals.append(m)
    return vals, rank


def _route_tables(s1, s2):
    v1, rank1 = _topk_rows(s1, P_TOPK)
    v2, rank2 = _topk_rows(s2, P_TOPK)
    v2_lo = jnp.concatenate(v2[0:8], axis=0)
    v2_all = jnp.concatenate(v2, axis=0)
    v1_hi = jnp.concatenate(v1[8:16], axis=0)
    cand = jnp.concatenate([v1[0] + v2_all] + [v1[j] + v2_lo for j in range(1, 8)] + [v1_hi + v2[0]], axis=0)
    n = cand.shape[0]
    row = lax.broadcasted_iota(jnp.int32, cand.shape, 0)
    sel = jnp.zeros(cand.shape, F32)
    cur = cand
    for _ in range(P_TOPK):
        m = jnp.max(cur, axis=0, keepdims=True)
        first = jnp.min(jnp.where(cur == m, row, n), axis=0, keepdims=True)
        hit = row == first
        sel = jnp.where(hit, 1.0, sel)
        cur = jnp.where(hit, -jnp.inf, cur)
    top = v1[0] + v2[0]
    z = jnp.sum(sel * jnp.exp(cand - top), axis=0, keepdims=True)
    counts = [jnp.sum(sel[0:16], axis=0, keepdims=True)]
    counts += [jnp.sum(sel[8 + 8 * j:16 + 8 * j], axis=0, keepdims=True) for j in range(1, 8)]
    counts += [sel[72 + j:73 + j] for j in range(8)]
    cnt = jnp.zeros(s1.shape, F32)
    for j in range(P_TOPK):
        cnt = jnp.where(rank1 == float(j), counts[j], cnt)
    p1 = jnp.exp(s1 - v1[0])
    p2 = jnp.exp(s2 - v2[0]) * (1.0 / z)
    return cnt, rank2, p1, p2


def _packed_words(x):
    return pltpu.bitcast(x, jnp.uint32)


def _unpacked(words):
    return pltpu.bitcast(words, BF16)


def _packed_words_xla(x):
    r, c = x.shape
    return lax.bitcast_convert_type(x.reshape(r // 2, 2, c).swapaxes(1, 2), jnp.uint32)


def _bf16_pair_words(x):
    hi = pltpu.bitcast(x.astype(BF16).astype(F32), jnp.uint32)
    return hi | lax.shift_right_logical(hi, jnp.uint32(16))


def _route_kernel(x_ref, ada_ref, npre_ref, wqt_ref, sk_ref, h2t_ref, cnt_ref, rk2_ref, p1_ref, p2_ref, qt_buf,
                  *, tm):
    ada = ada_ref[0]
    h2 = _rms_norm(x_ref[...], npre_ref[...]) * (1.0 + ada[4:5, :]) + ada[3:4, :]
    h2t = h2.T.astype(BF16)
    h2t_ref[...] = _packed_words(h2t)
    qt_buf[...] = jnp.dot(wqt_ref[...], h2t, preferred_element_type=F32).astype(BF16)

    def head(hd, carry):
        r0 = pl.multiple_of(hd * (2 * N_KEYS), 2 * N_KEYS)
        s1 = jnp.dot(sk_ref[hd, 0], qt_buf[pl.ds(r0, N_KEYS), :], preferred_element_type=F32)
        s2 = jnp.dot(sk_ref[hd, 1], qt_buf[pl.ds(r0 + N_KEYS, N_KEYS), :], preferred_element_type=F32)
        for lc in range(tm // LANES):
            sl = slice(lc * LANES, (lc + 1) * LANES)
            cnt, rk2, p1, p2 = _route_tables(s1[:, sl], s2[:, sl])
            cnt_ref[hd, lc] = _bf16_pair_words(cnt)
            rk2_ref[hd, lc] = _packed_words(rk2.astype(BF16))
            p1_ref[hd, lc] = _bf16_pair_words(p1)
            p2_ref[hd, lc] = _packed_words(p2.astype(BF16))
        return carry

    lax.fori_loop(0, P_HEADS, head, 0)


def _route(x1, ada3, norm_pre, wqt, subkeys):
    t, d = x1.shape
    s = t // ada3.shape[0]
    tm = min(TM_ROUTE, s)
    tab = lambda rows: jax.ShapeDtypeStruct((P_HEADS, t // LANES, rows, LANES), jnp.uint32)
    tab_spec = lambda rows: pl.BlockSpec((P_HEADS, tm // LANES, rows, LANES), lambda i: (0, i, 0, 0))
    k1, k2 = N_KEYS, N_KEYS // 2
    return pl.pallas_call(
        functools.partial(_route_kernel, tm=tm),
        grid=(t // tm,),
        in_specs=[pl.BlockSpec((tm, d), lambda i: (i, 0)),
                  pl.BlockSpec((1, 6, d), lambda i: ((i * tm) // s, 0, 0)),
                  _resident((1, d)),
                  _resident(wqt.shape),
                  _resident(subkeys.shape)],
        out_specs=[pl.BlockSpec((d // 2, tm), lambda i: (0, i)),
                   tab_spec(k1), tab_spec(k2), tab_spec(k1), tab_spec(k2)],
        out_shape=[jax.ShapeDtypeStruct((d // 2, t), jnp.uint32), tab(k1), tab(k2), tab(k1), tab(k2)],
        scratch_shapes=[pltpu.VMEM((wqt.shape[0], tm), BF16)],
        compiler_params=pltpu.CompilerParams(dimension_semantics=("arbitrary",),
                                             vmem_limit_bytes=VMEM_LIMIT_BYTES),
        name="route",
    )(x1, ada3, norm_pre, wqt, subkeys)


def _gelu_tanh(z):
    return 0.5 * z * (1.0 + jnp.tanh(0.7978845608028654 * (z + 0.044715 * (z * z * z))))


def _peer_kernel(h2t_ref, u_ref, vt_ref, cnt_ref, rk2_ref, p1_ref, p2_ref, x_ref, ada_ref, npost_ref, o_ref,
                 acc, g_buf, z_buf, a_buf, *, tm, te):
    j = pl.program_id(1)

    @pl.when(j == 0)
    def _():
        acc[...] = jnp.zeros(acc.shape, F32)

    na = te // N_KEYS
    apc = MXU_K // N_KEYS
    nkc = te // MXU_K
    nlc = tm // LANES
    nlh = tm // MXU_N
    d = acc.shape[0]
    rh = d // 2

    def gate_block(kc, ac, lc):
        al = kc * apc + ac
        pk = (N_KEYS // BF16_ROWS, BF16_ROWS, LANES)
        gate = None
        for hd in range(P_HEADS):
            cnt_a = pltpu.bitcast(jnp.broadcast_to(cnt_ref[hd, lc, al:al + 1, :], (8, LANES)), BF16)
            p1_a = pltpu.bitcast(jnp.broadcast_to(p1_ref[hd, lc, al:al + 1, :], (8, LANES)), BF16)
            hit = _unpacked(rk2_ref[hd, lc]).reshape(pk) < cnt_a[None]
            term = jnp.where(hit, _unpacked(p2_ref[hd, lc]).reshape(pk) * p1_a[None], jnp.zeros((), BF16))
            gate = term if gate is None else gate + term
        g_buf[kc % 2, ac, lc] = gate.reshape(N_KEYS, LANES)

    def score_piece(kc, ac, lh):
        r0 = kc * MXU_K + ac * N_KEYS
        z_buf[kc % 2, ac * N_KEYS:(ac + 1) * N_KEYS, lh * MXU_N:(lh + 1) * MXU_N] = jnp.dot(
            _unpacked(u_ref[r0 // 2:(r0 + N_KEYS) // 2, :]),
            _unpacked(h2t_ref[:, lh * MXU_N:(lh + 1) * MXU_N]), preferred_element_type=F32)

    def act_block(kc, ac, lc):
        r0 = kc * MXU_K + ac * N_KEYS
        zb = z_buf[kc % 2, ac * N_KEYS:(ac + 1) * N_KEYS, lc * LANES:(lc + 1) * LANES]
        a_buf[r0:r0 + N_KEYS, lc * LANES:(lc + 1) * LANES] = _gelu_tanh(zb).astype(BF16) * g_buf[kc % 2, ac, lc]

    outs = {}

    def out_piece(kc, r, lh):
        e0 = kc * MXU_K
        part = jnp.dot(_unpacked(vt_ref[r * rh // 2:(r + 1) * rh // 2, e0:e0 + MXU_K]),
                       a_buf[e0:e0 + MXU_K, lh * MXU_N:(lh + 1) * MXU_N], preferred_element_type=F32)
        outs[(r, lh)] = part if (r, lh) not in outs else outs[(r, lh)] + part

    blocks = [(ac, lc) for lc in range(nlc) for ac in range(apc)]
    pieces = [(ac, lh) for lh in range(nlh) for ac in range(apc)]
    for s, blk in enumerate(blocks):
        gate_block(0, *blk)
        if s % 2 == 1:
            score_piece(0, *pieces[s // 2])
    pending = []
    for kc in range(nkc):
        mxu = list(pending)
        pending = []
        if kc + 1 < nkc:
            mxu += [functools.partial(score_piece, kc + 1, ac, lh) for ac, lh in pieces]
        for s, (ac, lc) in enumerate(blocks):
            if kc + 1 < nkc:
                gate_block(kc + 1, ac, lc)
            act_block(kc, ac, lc)
            if mxu:
                mxu.pop(0)()
            if ac == apc - 1 and (lc + 1) % (MXU_N // LANES) == 0:
                lh = lc // (MXU_N // LANES)
                ready = [functools.partial(out_piece, kc, r, lh) for r in range(d // rh)]
                if lh + 1 < nlh:
                    mxu += ready
                else:
                    pending = ready
        for f in mxu:
            f()
    for f in pending:
        f()
    for (r, lh), val in outs.items():
        acc[r * rh:(r + 1) * rh, lh * MXU_N:(lh + 1) * MXU_N] += val

    @pl.when(j == pl.num_programs(1) - 1)
    def _():
        y = acc[...].T
        o_ref[...] = x_ref[...] + ada_ref[0][5:6, :] * _rms_norm(y, npost_ref[...])


def _peer(h2t, u_bf, vt_bf, cnt, rk2, p1, p2, x1, ada3, norm_post):
    t, d = x1.shape
    s = t // ada3.shape[0]
    ne = 2 * u_bf.shape[0]
    tm = min(TM_PEER, s)
    te = TE_PEER
    once = pl.Buffered(1)
    tab2_spec = pl.BlockSpec((P_HEADS, tm // LANES, N_KEYS // 2, LANES), lambda i, j: (0, i, 0, 0))
    tab1_spec = pl.BlockSpec((P_HEADS, tm // LANES, te // N_KEYS, LANES), lambda i, j: (0, i, j, 0))
    return pl.pallas_call(
        functools.partial(_peer_kernel, tm=tm, te=te),
        grid=(t // tm, ne // te),
        in_specs=[pl.BlockSpec((d // 2, tm), lambda i, j: (0, i)),
                  pl.BlockSpec((te // 2, d), lambda i, j: (j, 0)),
                  pl.BlockSpec((d // 2, te), lambda i, j: (0, j)),
                  tab1_spec, tab2_spec, tab1_spec, tab2_spec,
                  pl.BlockSpec((tm, d), lambda i, j: (i, 0), pipeline_mode=once),
                  pl.BlockSpec((1, 6, d), lambda i, j: ((i * tm) // s, 0, 0)),
                  _resident((1, d))],
        out_specs=pl.BlockSpec((tm, d), lambda i, j: (i, 0), pipeline_mode=once),
        out_shape=jax.ShapeDtypeStruct((t, d), F32),
        scratch_shapes=[pltpu.VMEM((d, tm), F32),
                        pltpu.VMEM((2, MXU_K // N_KEYS, tm // LANES, N_KEYS, LANES), BF16),
                        pltpu.VMEM((2, MXU_K, tm), F32),
                        pltpu.VMEM((te, tm), BF16)],
        compiler_params=pltpu.CompilerParams(dimension_semantics=("arbitrary", "arbitrary"),
                                             vmem_limit_bytes=VMEM_LIMIT_BYTES),
        name="peer",
    )(h2t, u_bf, vt_bf, cnt, rk2, p1, p2, x1, ada3, norm_post)


def _layer(x, c, w_ada, b_ada, norm1_pre, norm1_post, w_in, conv_a_w, conv_qk_w, b_igate, b_fgate, mh_norm_w,
           w_branch_a, w_branch_m, w_out, norm2_pre, norm2_post, peer_wq, peer_subkeys, peer_u, peer_v):
    bsz, s, d = x.shape
    ng = 2 * M_HEADS
    ada3 = _ada(c, w_ada, b_ada).reshape(bsz, 6, d)

    o_if = 3 * d + 2 * M_HEADS * M_DK + 2 * d
    w_main = jnp.concatenate([w_in[:, :o_if], w_in[:, o_if + ng:]], axis=1).astype(BF16)
    w_if = w_in[:, o_if:o_if + ng]
    wif_col = jnp.pad(w_if, ((0, 0), (0, LANES - ng))).astype(BF16)
    wif_row = w_if.T.astype(BF16)
    b_if = jnp.concatenate([b_igate, b_fgate])
    bif_col = jnp.pad(b_if, (0, LANES - ng)).reshape(1, LANES)
    bif_row = b_if.reshape(ng, 1)

    ya, qk, v, so, sga, sgm, gcol, grow = _inproj(
        x, ada3, norm1_pre.reshape(1, d), w_main, wif_col, wif_row, bif_col, bif_row, conv_a_w, conv_qk_w)
    x1 = _mixer(x, ada3, qk, v, so, sga, sgm, ya, gcol, grow, mh_norm_w.reshape(1, d), norm1_post.reshape(1, d),
                w_branch_a.astype(BF16), w_branch_m.astype(BF16), w_out.astype(BF16))

    x1f = x1.reshape(bsz * s, d)
    h2t, cnt, rk2, p1, p2 = _route(x1f, ada3, norm2_pre.reshape(1, d), peer_wq.T.astype(BF16),
                                   peer_subkeys.astype(BF16))
    out = _peer(h2t, _packed_words_xla(peer_u.astype(BF16)), _packed_words_xla(peer_v.T.astype(BF16)),
                cnt, rk2, p1, p2, x1f, ada3, norm2_post.reshape(1, d))
    return out.reshape(bsz, s, d)


def kernel(x, c, w_ada, b_ada, norm1_pre, norm1_post, w_in, conv_a_w, conv_qk_w, b_igate, b_fgate, mh_norm_w, w_branch_a, w_branch_m, w_out, norm2_pre, norm2_post, peer_wq, peer_subkeys, peer_u, peer_v):
    for l in range(w_ada.shape[0]):
        x = _layer(x, c, w_ada[l], b_ada[l], norm1_pre[l], norm1_post[l], w_in[l], conv_a_w[l], conv_qk_w[l],
                   b_igate[l], b_fgate[l], mh_norm_w[l], w_branch_a[l], w_branch_m[l], w_out[l], norm2_pre[l],
                   norm2_post[l], peer_wq[l], peer_subkeys[l], peer_u[l], peer_v[l])
    return x
```

```python
import functools

import jax
import jax.numpy as jnp
from jax import lax
from jax.experimental import pallas as pl
from jax.experimental.pallas import tpu as pltpu

F32 = jnp.float32
BF16 = jnp.bfloat16

EPS = 1e-6
M_HEADS = 8
M_DK = 64
M_DV = 128
CONV_K = 3
QK_CONV_K = 4
P_HEADS = 8
N_KEYS = 128
P_TOPK = 16
HALO = 8
LANES = 128
MXU_K = 256
MXU_N = 256
BF16_ROWS = 16
NOT_RANKED = 99.0

VMEM_LIMIT_BYTES = 56 * 1024 * 1024

MLSTM_CHUNK = 64
TS_INPROJ = 256
TS_MIXER = 256
TM_ROUTE = 256
TM_PEER = 1024
TE_PEER = 1024


def _resident(shape):
    nd = len(shape)
    return pl.BlockSpec(shape, lambda *_: (0,) * nd, pipeline_mode=pl.Buffered(1))


def _rms_norm(x, w):
    return x * lax.rsqrt(jnp.mean(x * x, axis=-1, keepdims=True) + EPS) * w


def _sigmoid(x):
    return 1.0 / (1.0 + jnp.exp(-x))


def _ada_kernel(c_ref, w_ref, b_ref, o_ref):
    c = c_ref[...]
    s = c * _sigmoid(c)
    o_ref[...] = jnp.dot(s.astype(BF16), w_ref[...].astype(BF16), preferred_element_type=F32) + b_ref[...]


def _ada(c, w_ada, b_ada):
    bsz, d = c.shape
    n = w_ada.shape[1]
    return pl.pallas_call(
        _ada_kernel,
        grid=(n // d,),
        in_specs=[pl.BlockSpec((bsz, d), lambda j: (0, 0)),
                  pl.BlockSpec((d, d), lambda j: (0, j)),
                  pl.BlockSpec((1, d), lambda j: (0, j))],
        out_specs=pl.BlockSpec((bsz, d), lambda j: (0, j)),
        out_shape=jax.ShapeDtypeStruct((bsz, n), F32),
        compiler_params=pltpu.CompilerParams(vmem_limit_bytes=VMEM_LIMIT_BYTES),
        name="ada",
    )(c, w_ada, b_ada.reshape(1, n))


def _inproj_kernel(x_ref, ada_ref, npre_ref, w_ref, wifc_ref, wifr_ref, bifc_ref, bifr_ref, cwa_ref, cwqk_ref,
                   ya_ref, qk_ref, v_ref, so_ref, sga_ref, sgm_ref, gcol_ref, grow_ref,
                   ubuf, qkbuf, *, ts, d):
    @pl.when(pl.program_id(1) == 0)
    def _():
        ubuf[0:HALO, :] = jnp.zeros((HALO, d), F32)
        qkbuf[0:HALO, :] = jnp.zeros((HALO, d), F32)

    x = x_ref[0]
    ada = ada_ref[0]
    h = _rms_norm(x, npre_ref[...]) * (1.0 + ada[1:2, :]) + ada[0:1, :]
    hb = h.astype(BF16)

    def proj(k):
        return jnp.dot(hb, w_ref[:, k * d:(k + 1) * d], preferred_element_type=F32)

    u = proj(0) * proj(2)
    ubuf[HALO:HALO + ts, :] = u
    conv = u * cwa_ref[CONV_K - 1:CONV_K, :]
    for j in range(CONV_K - 1):
        off = HALO - (CONV_K - 1) + j
        conv = conv + ubuf[off:off + ts, :] * cwa_ref[j:j + 1, :]
    ya_ref[0] = (proj(1) * conv).astype(BF16)
    ubuf[0:HALO, :] = ubuf[ts:ts + HALO, :]

    qk = proj(3)
    qkbuf[HALO:HALO + ts, :] = qk
    cq = qk * cwqk_ref[QK_CONV_K - 1:QK_CONV_K, :]
    for j in range(QK_CONV_K - 1):
        off = HALO - (QK_CONV_K - 1) + j
        cq = cq + qkbuf[off:off + ts, :] * cwqk_ref[j:j + 1, :]
    qkbuf[0:HALO, :] = qkbuf[ts:ts + HALO, :]
    cq = cq * _sigmoid(cq)
    is_q = lax.broadcasted_iota(jnp.int32, (1, d), 1) < (M_HEADS * M_DK)
    qk_ref[0] = (cq * jnp.where(is_q, M_DK ** -0.5, 1.0)).astype(BF16)

    v_ref[0] = proj(4).astype(BF16)
    so_ref[0] = _sigmoid(proj(5)).astype(BF16)
    sga_ref[0] = _sigmoid(proj(6)).astype(BF16)
    sgm_ref[0] = _sigmoid(proj(7)).astype(BF16)

    gcol_ref[0] = jnp.dot(hb, wifc_ref[...], preferred_element_type=F32) + bifc_ref[...]
    grow_ref[0] = lax.dot_general(wifr_ref[...], hb, (((1,), (1,)), ((), ())),
                                  preferred_element_type=F32) + bifr_ref[...]


def _inproj(x, ada3, norm_pre, w_main, wif_col, wif_row, bif_col, bif_row, conv_a_w, conv_qk_w):
    bsz, s, d = x.shape
    ts = min(TS_INPROJ, s)
    ng = 2 * M_HEADS
    tok = lambda dt, w=d: jax.ShapeDtypeStruct((bsz, s, w), dt)
    tok_spec = lambda w=d: pl.BlockSpec((1, ts, w), lambda b, i: (b, i, 0))
    return pl.pallas_call(
        functools.partial(_inproj_kernel, ts=ts, d=d),
        grid=(bsz, s // ts),
        in_specs=[tok_spec(),
                  pl.BlockSpec((1, 6, d), lambda b, i: (b, 0, 0)),
                  _resident((1, d)),
                  _resident(w_main.shape),
                  _resident(wif_col.shape),
                  _resident(wif_row.shape),
                  _resident(bif_col.shape),
                  _resident(bif_row.shape),
                  _resident(conv_a_w.shape),
                  _resident(conv_qk_w.shape)],
        out_specs=[tok_spec(), tok_spec(), tok_spec(), tok_spec(), tok_spec(), tok_spec(),
                   tok_spec(LANES),
                   pl.BlockSpec((1, ng, ts), lambda b, i: (b, 0, i))],
        out_shape=[tok(BF16), tok(BF16), tok(BF16), tok(BF16), tok(BF16), tok(BF16),
                   tok(F32, LANES),
                   jax.ShapeDtypeStruct((bsz, ng, s), F32)],
        scratch_shapes=[pltpu.VMEM((HALO + ts, d), F32), pltpu.VMEM((HALO + ts, d), F32)],
        compiler_params=pltpu.CompilerParams(dimension_semantics=("arbitrary", "arbitrary"),
                                             vmem_limit_bytes=VMEM_LIMIT_BYTES),
        name="inproj",
    )(x, ada3, norm_pre, w_main, wif_col, wif_row, bif_col, bif_row, conv_a_w, conv_qk_w)


def _log_sigmoid(x):
    return jnp.minimum(x, 0.0) - jnp.log(1.0 + jnp.exp(-jnp.abs(x)))


def _chunk_cumsum(x, axis, chunk):
    pos = lax.broadcasted_iota(jnp.int32, x.shape, axis) % chunk
    k = 1
    while k < chunk:
        x = x + jnp.where(pos >= k, pltpu.roll(x, k, axis), 0.0)
        k *= 2
    return x


def _mlstm_chunk(q, k, v, i_col, i_row, b_col, b_row, m_prev, c_aug, tril, e0):
    L = q.shape[0]
    dlog = jnp.where(tril, b_col - b_row + i_row, -jnp.inf)
    inter = b_col + m_prev
    m_t = jnp.maximum(inter, jnp.max(dlog, axis=1, keepdims=True))
    w = jnp.exp(dlog - m_t)
    s_qk = lax.dot_general(q, k, (((1,), (1,)), ((), ())), preferred_element_type=F32) * w
    a_inter = jnp.exp(inter - m_t)
    intra = jnp.dot(s_qk.astype(BF16), v, preferred_element_type=F32)
    qc = jnp.dot(q, c_aug.astype(BF16), preferred_element_type=F32)
    num = intra + a_inter * qc[:, :M_DV]
    den = jnp.sum(s_qk, axis=1, keepdims=True) + a_inter * qc[:, M_DV:M_DV + 1]
    h = num / jnp.maximum(jnp.abs(den), jnp.exp(-m_t))
    b_last = b_col[L - 1:L, :]
    g = b_last - b_col + i_col
    m_new = jnp.maximum(b_last + m_prev, jnp.max(g, axis=0, keepdims=True))
    decay = jnp.exp(b_last + m_prev - m_new)
    ws = jnp.exp(g - m_new)
    kw = (k.astype(F32) * ws).astype(BF16)
    v_aug = jnp.concatenate([v, e0], axis=1)
    upd = lax.dot_general(kw, v_aug, (((0,), (0,)), ((), ())), preferred_element_type=F32)
    return h, decay * c_aug + upd, m_new


def _mixer_kernel(x_ref, ada_ref, qk_ref, v_ref, so_ref, sga_ref, sgm_ref, ya_ref, gcol_ref, grow_ref,
                  mhw_ref, npost_ref, wa_ref, wm_ref, wo_ref, o_ref, c_st, m_st, ym_buf, *, ts, chunk):
    @pl.when(pl.program_id(1) == 0)
    def _():
        c_st[...] = jnp.zeros(c_st.shape, F32)
        m_st[...] = jnp.zeros(m_st.shape, F32)

    gcol = gcol_ref[0]
    grow = grow_ref[0]
    b_col_all = _chunk_cumsum(_log_sigmoid(gcol), 0, chunk)
    b_row_all = _chunk_cumsum(_log_sigmoid(grow), 1, chunk)

    r_i = lax.broadcasted_iota(jnp.int32, (chunk, chunk), 0)
    c_i = lax.broadcasted_iota(jnp.int32, (chunk, chunk), 1)
    tril = r_i >= c_i
    e0 = (lax.broadcasted_iota(jnp.int32, (chunk, M_DV), 1) == 0).astype(BF16)
    nqk = M_HEADS * M_DK

    for hd in range(M_HEADS):
        c_aug = c_st[hd]
        m_prev = m_st[hd][0:1, 0:1]
        for ck in range(ts // chunk):
            r0 = ck * chunk
            q = qk_ref[0, r0:r0 + chunk, hd * M_DK:(hd + 1) * M_DK]
            k = qk_ref[0, r0:r0 + chunk, nqk + hd * M_DK:nqk + (hd + 1) * M_DK]
            v = v_ref[0, r0:r0 + chunk, hd * M_DV:(hd + 1) * M_DV]
            h, c_aug, m_prev = _mlstm_chunk(
                q, k, v,
                gcol[r0:r0 + chunk, hd:hd + 1],
                grow[hd:hd + 1, r0:r0 + chunk],
                b_col_all[r0:r0 + chunk, M_HEADS + hd:M_HEADS + hd + 1],
                b_row_all[M_HEADS + hd:M_HEADS + hd + 1, r0:r0 + chunk],
                m_prev, c_aug, tril, e0)
            hn = _rms_norm(h, mhw_ref[:, hd * M_DV:(hd + 1) * M_DV])
            so = so_ref[0, r0:r0 + chunk, hd * M_DV:(hd + 1) * M_DV].astype(F32)
            ym_buf[r0:r0 + chunk, hd * M_DV:(hd + 1) * M_DV] = (so * hn).astype(BF16)
        c_st[hd] = c_aug
        m_st[hd] = jnp.broadcast_to(m_prev, m_st.shape[1:])

    mix = (sga_ref[0].astype(F32) * jnp.dot(ya_ref[0], wa_ref[...], preferred_element_type=F32)
           + sgm_ref[0].astype(F32) * jnp.dot(ym_buf[...], wm_ref[...], preferred_element_type=F32))
    y = jnp.dot(mix.astype(BF16), wo_ref[...], preferred_element_type=F32)
    o_ref[0] = x_ref[0] + ada_ref[0][2:3, :] * _rms_norm(y, npost_ref[...])


def _mixer(x, ada3, qk, v, so, sga, sgm, ya, gcol, grow, mh_norm_w, norm_post, wa, wm, wo):
    bsz, s, d = x.shape
    ts = min(TS_MIXER, s)
    ng = 2 * M_HEADS
    tok_spec = lambda w=d: pl.BlockSpec((1, ts, w), lambda b, i: (b, i, 0))
    return pl.pallas_call(
        functools.partial(_mixer_kernel, ts=ts, chunk=MLSTM_CHUNK),
        grid=(bsz, s // ts),
        in_specs=[tok_spec(),
                  pl.BlockSpec((1, 6, d), lambda b, i: (b, 0, 0)),
                  tok_spec(), tok_spec(), tok_spec(), tok_spec(), tok_spec(), tok_spec(),
                  tok_spec(LANES),
                  pl.BlockSpec((1, ng, ts), lambda b, i: (b, 0, i)),
                  _resident((1, d)), _resident((1, d)),
                  _resident(wa.shape), _resident(wm.shape), _resident(wo.shape)],
        out_specs=tok_spec(),
        out_shape=jax.ShapeDtypeStruct((bsz, s, d), F32),
        scratch_shapes=[pltpu.VMEM((M_HEADS, M_DK, 2 * M_DV), F32),
                        pltpu.VMEM((M_HEADS, 8, LANES), F32),
                        pltpu.VMEM((ts, d), BF16)],
        compiler_params=pltpu.CompilerParams(dimension_semantics=("arbitrary", "arbitrary"),
                                             vmem_limit_bytes=VMEM_LIMIT_BYTES),
        name="mixer",
    )(x, ada3, qk, v, so, sga, sgm, ya, gcol, grow, mh_norm_w, norm_post, wa, wm, wo)


def _argmax_rows(s, exact):
    m = jnp.max(s, axis=0, keepdims=True)
    hit = s == m
    if exact:
        row = lax.broadcasted_iota(jnp.int32, s.shape, 0)
        hit = row == jnp.min(jnp.where(hit, row, s.shape[0]), axis=0, keepdims=True)
    return hit, m


def _topk_rows(s, k, exact):
    rank = jnp.full(s.shape, NOT_RANKED, F32)
    vals = []
    for r in range(k):
        hit, m = _argmax_rows(s, exact)
        rank = jnp.where(hit, float(r), rank)
        s = jnp.where(hit, -jnp.inf, s)
        vals.append(m)
    return vals, rank


def _route_tables(s1, s2, exact):
    v1, rank1 = _topk_rows(s1, P_TOPK, exact)
    v2, rank2 = _topk_rows(s2, P_TOPK, exact)
    v2_lo = jnp.concatenate(v2[0:8], axis=0)
    v2_all = jnp.concatenate(v2, axis=0)
    v1_hi = jnp.concatenate(v1[8:16], axis=0)
    cand = jnp.concatenate([v1[0] + v2_all] + [v1[j] + v2_lo for j in range(1, 8)] + [v1_hi + v2[0]], axis=0)
    sel = jnp.zeros(cand.shape, F32)
    cur = cand
    for _ in range(P_TOPK):
        hit, _ = _argmax_rows(cur, exact)
        sel = jnp.where(hit, 1.0, sel)
        cur = jnp.where(hit, -jnp.inf, cur)
    ranked = lambda rk: jnp.sum(jnp.where(rk < NOT_RANKED, 1.0, 0.0), axis=0, keepdims=True)
    picked = jnp.maximum(jnp.maximum(ranked(rank1), ranked(rank2)), jnp.sum(sel, axis=0, keepdims=True))
    tied = jnp.where(picked > float(P_TOPK), 1.0, 0.0)
    top = v1[0] + v2[0]
    z = jnp.sum(sel * jnp.exp(cand - top), axis=0, keepdims=True)
    counts = [jnp.sum(sel[0:16], axis=0, keepdims=True)]
    counts += [jnp.sum(sel[8 + 8 * j:16 + 8 * j], axis=0, keepdims=True) for j in range(1, 8)]
    counts += [sel[72 + j:73 + j] for j in range(8)]
    cnt = jnp.zeros(s1.shape, F32)
    for j in range(P_TOPK):
        cnt = jnp.where(rank1 == float(j), counts[j], cnt)
    p1 = jnp.exp(s1 - v1[0])
    p2 = jnp.exp(s2 - v2[0]) * (1.0 / z)
    return cnt, rank2, p1, p2, tied


def _packed_words(x):
    return pltpu.bitcast(x, jnp.uint32)


def _unpacked(words):
    return pltpu.bitcast(words, BF16)


def _packed_words_xla(x):
    r, c = x.shape
    return lax.bitcast_convert_type(x.reshape(r // 2, 2, c).swapaxes(1, 2), jnp.uint32)


def _bf16_pair_words(x):
    hi = pltpu.bitcast(x.astype(BF16).astype(F32), jnp.uint32)
    return hi | lax.shift_right_logical(hi, jnp.uint32(16))


def _route_kernel(x_ref, ada_ref, npre_ref, wqt_ref, sk_ref, h2t_ref, cnt_ref, rk2_ref, p1_ref, p2_ref, qt_buf,
                  *, tm):
    ada = ada_ref[0]
    h2 = _rms_norm(x_ref[...], npre_ref[...]) * (1.0 + ada[4:5, :]) + ada[3:4, :]
    h2t = h2.T.astype(BF16)
    h2t_ref[...] = _packed_words(h2t)
    qt_buf[...] = jnp.dot(wqt_ref[...], h2t, preferred_element_type=F32).astype(BF16)

    def head(hd, carry):
        r0 = pl.multiple_of(hd * (2 * N_KEYS), 2 * N_KEYS)
        s1 = jnp.dot(sk_ref[hd, 0], qt_buf[pl.ds(r0, N_KEYS), :], preferred_element_type=F32)
        s2 = jnp.dot(sk_ref[hd, 1], qt_buf[pl.ds(r0 + N_KEYS, N_KEYS), :], preferred_element_type=F32)
        def tables(lc, exact):
            sl = slice(lc * LANES, (lc + 1) * LANES)
            cnt, rk2, p1, p2, tied = _route_tables(s1[:, sl], s2[:, sl], exact)
            cnt_ref[hd, lc] = _bf16_pair_words(cnt)
            rk2_ref[hd, lc] = _packed_words(rk2.astype(BF16))
            p1_ref[hd, lc] = _bf16_pair_words(p1)
            p2_ref[hd, lc] = _packed_words(p2.astype(BF16))
            return tied

        for lc in range(tm // LANES):
            tied = tables(lc, exact=False)

            @pl.when(jnp.max(tied) > 0.0)
            def _():
                tables(lc, exact=True)
        return carry

    lax.fori_loop(0, P_HEADS, head, 0)


def _route(x1, ada3, norm_pre, wqt, subkeys):
    t, d = x1.shape
    s = t // ada3.shape[0]
    tm = min(TM_ROUTE, s)
    tab = lambda rows: jax.ShapeDtypeStruct((P_HEADS, t // LANES, rows, LANES), jnp.uint32)
    tab_spec = lambda rows: pl.BlockSpec((P_HEADS, tm // LANES, rows, LANES), lambda i: (0, i, 0, 0))
    k1, k2 = N_KEYS, N_KEYS // 2
    return pl.pallas_call(
        functools.partial(_route_kernel, tm=tm),
        grid=(t // tm,),
        in_specs=[pl.BlockSpec((tm, d), lambda i: (i, 0)),
                  pl.BlockSpec((1, 6, d), lambda i: ((i * tm) // s, 0, 0)),
                  _resident((1, d)),
                  _resident(wqt.shape),
                  _resident(subkeys.shape)],
        out_specs=[pl.BlockSpec((d // 2, tm), lambda i: (0, i)),
                   tab_spec(k1), tab_spec(k2), tab_spec(k1), tab_spec(k2)],
        out_shape=[jax.ShapeDtypeStruct((d // 2, t), jnp.uint32), tab(k1), tab(k2), tab(k1), tab(k2)],
        scratch_shapes=[pltpu.VMEM((wqt.shape[0], tm), BF16)],
        compiler_params=pltpu.CompilerParams(dimension_semantics=("arbitrary",),
                                             vmem_limit_bytes=VMEM_LIMIT_BYTES),
        name="route",
    )(x1, ada3, norm_pre, wqt, subkeys)


def _gelu_tanh(z):
    return 0.5 * z * (1.0 + jnp.tanh(0.7978845608028654 * (z + 0.044715 * (z * z * z))))


def _peer_kernel(h2t_ref, u_ref, vt_ref, cnt_ref, rk2_ref, p1_ref, p2_ref, x_ref, ada_ref, npost_ref, o_ref,
                 acc, g_buf, z_buf, a_buf, *, tm, te):
    j = pl.program_id(1)

    @pl.when(j == 0)
    def _():
        acc[...] = jnp.zeros(acc.shape, F32)

    na = te // N_KEYS
    apc = MXU_K // N_KEYS
    nkc = te // MXU_K
    nlc = tm // LANES
    nlh = tm // MXU_N
    d = acc.shape[0]
    rh = d // 2

    def gate_block(kc, ac, lc):
        al = kc * apc + ac
        pk = (N_KEYS // BF16_ROWS, BF16_ROWS, LANES)
        gate = None
        for hd in range(P_HEADS):
            cnt_a = pltpu.bitcast(jnp.broadcast_to(cnt_ref[hd, lc, al:al + 1, :], (8, LANES)), BF16)
            p1_a = pltpu.bitcast(jnp.broadcast_to(p1_ref[hd, lc, al:al + 1, :], (8, LANES)), BF16)
            hit = _unpacked(rk2_ref[hd, lc]).reshape(pk) < cnt_a[None]
            term = jnp.where(hit, _unpacked(p2_ref[hd, lc]).reshape(pk) * p1_a[None], jnp.zeros((), BF16))
            gate = term if gate is None else gate + term
        g_buf[kc % 2, ac, lc] = gate.reshape(N_KEYS, LANES)

    def score_piece(kc, ac, lh):
        r0 = kc * MXU_K + ac * N_KEYS
        z_buf[kc % 2, ac * N_KEYS:(ac + 1) * N_KEYS, lh * MXU_N:(lh + 1) * MXU_N] = jnp.dot(
            _unpacked(u_ref[r0 // 2:(r0 + N_KEYS) // 2, :]),
            _unpacked(h2t_ref[:, lh * MXU_N:(lh + 1) * MXU_N]), preferred_element_type=F32)

    def act_block(kc, ac, lc):
        r0 = kc * MXU_K + ac * N_KEYS
        zb = z_buf[kc % 2, ac * N_KEYS:(ac + 1) * N_KEYS, lc * LANES:(lc + 1) * LANES]
        a_buf[r0:r0 + N_KEYS, lc * LANES:(lc + 1) * LANES] = _gelu_tanh(zb).astype(BF16) * g_buf[kc % 2, ac, lc]

    outs = {}

    def out_piece(kc, r, lh):
        e0 = kc * MXU_K
        part = jnp.dot(_unpacked(vt_ref[r * rh // 2:(r + 1) * rh // 2, e0:e0 + MXU_K]),
                       a_buf[e0:e0 + MXU_K, lh * MXU_N:(lh + 1) * MXU_N], preferred_element_type=F32)
        outs[(r, lh)] = part if (r, lh) not in outs else outs[(r, lh)] + part

    blocks = [(ac, lc) for lc in range(nlc) for ac in range(apc)]
    pieces = [(ac, lh) for lh in range(nlh) for ac in range(apc)]
    for s, blk in enumerate(blocks):
        gate_block(0, *blk)
        if s % 2 == 1:
            score_piece(0, *pieces[s // 2])
    pending = []
    for kc in range(nkc):
        mxu = list(pending)
        pending = []
        if kc + 1 < nkc:
            mxu += [functools.partial(score_piece, kc + 1, ac, lh) for ac, lh in pieces]
        for s, (ac, lc) in enumerate(blocks):
            if kc + 1 < nkc:
                gate_block(kc + 1, ac, lc)
            act_block(kc, ac, lc)
            if mxu:
                mxu.pop(0)()
            if ac == apc - 1 and (lc + 1) % (MXU_N // LANES) == 0:
                lh = lc // (MXU_N // LANES)
                ready = [functools.partial(out_piece, kc, r, lh) for r in range(d // rh)]
                if lh + 1 < nlh:
                    mxu += ready
                else:
                    pending = ready
        for f in mxu:
            f()
    for f in pending:
        f()
    for (r, lh), val in outs.items():
        acc[r * rh:(r + 1) * rh, lh * MXU_N:(lh + 1) * MXU_N] += val

    @pl.when(j == pl.num_programs(1) - 1)
    def _():
        y = acc[...].T
        o_ref[...] = x_ref[...] + ada_ref[0][5:6, :] * _rms_norm(y, npost_ref[...])


def _peer(h2t, u_bf, vt_bf, cnt, rk2, p1, p2, x1, ada3, norm_post):
    t, d = x1.shape
    s = t // ada3.shape[0]
    ne = 2 * u_bf.shape[0]
    tm = min(TM_PEER, s)
    te = TE_PEER
    once = pl.Buffered(1)
    tab2_spec = pl.BlockSpec((P_HEADS, tm // LANES, N_KEYS // 2, LANES), lambda i, j: (0, i, 0, 0))
    tab1_spec = pl.BlockSpec((P_HEADS, tm // LANES, te // N_KEYS, LANES), lambda i, j: (0, i, j, 0))
    return pl.pallas_call(
        functools.partial(_peer_kernel, tm=tm, te=te),
        grid=(t // tm, ne // te),
        in_specs=[pl.BlockSpec((d // 2, tm), lambda i, j: (0, i)),
                  pl.BlockSpec((te // 2, d), lambda i, j: (j, 0)),
                  pl.BlockSpec((d // 2, te), lambda i, j: (0, j)),
                  tab1_spec, tab2_spec, tab1_spec, tab2_spec,
                  pl.BlockSpec((tm, d), lambda i, j: (i, 0), pipeline_mode=once),
                  pl.BlockSpec((1, 6, d), lambda i, j: ((i * tm) // s, 0, 0)),
                  _resident((1, d))],
        out_specs=pl.BlockSpec((tm, d), lambda i, j: (i, 0), pipeline_mode=once),
        out_shape=jax.ShapeDtypeStruct((t, d), F32),
        scratch_shapes=[pltpu.VMEM((d, tm), F32),
                        pltpu.VMEM((2, MXU_K // N_KEYS, tm // LANES, N_KEYS, LANES), BF16),
                        pltpu.VMEM((2, MXU_K, tm), F32),
                        pltpu.VMEM((te, tm), BF16)],
        compiler_params=pltpu.CompilerParams(dimension_semantics=("arbitrary", "arbitrary"),
                                             vmem_limit_bytes=VMEM_LIMIT_BYTES),
        name="peer",
    )(h2t, u_bf, vt_bf, cnt, rk2, p1, p2, x1, ada3, norm_post)


def _layer(x, c, w_ada, b_ada, norm1_pre, norm1_post, w_in, conv_a_w, conv_qk_w, b_igate, b_fgate, mh_norm_w,
           w_branch_a, w_branch_m, w_out, norm2_pre, norm2_post, peer_wq, peer_subkeys, peer_u, peer_v):
    bsz, s, d = x.shape
    ng = 2 * M_HEADS
    ada3 = _ada(c, w_ada, b_ada).reshape(bsz, 6, d)

    o_if = 3 * d + 2 * M_HEADS * M_DK + 2 * d
    w_main = jnp.concatenate([w_in[:, :o_if], w_in[:, o_if + ng:]], axis=1).astype(BF16)
    w_if = w_in[:, o_if:o_if + ng]
    wif_col = jnp.pad(w_if, ((0, 0), (0, LANES - ng))).astype(BF16)
    wif_row = w_if.T.astype(BF16)
    b_if = jnp.concatenate([b_igate, b_fgate])
    bif_col = jnp.pad(b_if, (0, LANES - ng)).reshape(1, LANES)
    bif_row = b_if.reshape(ng, 1)

    ya, qk, v, so, sga, sgm, gcol, grow = _inproj(
        x, ada3, norm1_pre.reshape(1, d), w_main, wif_col, wif_row, bif_col, bif_row, conv_a_w, conv_qk_w)
    x1 = _mixer(x, ada3, qk, v, so, sga, sgm, ya, gcol, grow, mh_norm_w.reshape(1, d), norm1_post.reshape(1, d),
                w_branch_a.astype(BF16), w_branch_m.astype(BF16), w_out.astype(BF16))

    x1f = x1.reshape(bsz * s, d)
    h2t, cnt, rk2, p1, p2 = _route(x1f, ada3, norm2_pre.reshape(1, d), peer_wq.T.astype(BF16),
                                   peer_subkeys.astype(BF16))
    out = _peer(h2t, _packed_words_xla(peer_u.astype(BF16)), _packed_words_xla(peer_v.T.astype(BF16)),
                cnt, rk2, p1, p2, x1f, ada3, norm2_post.reshape(1, d))
    return out.reshape(bsz, s, d)


def kernel(x, c, w_ada, b_ada, norm1_pre, norm1_post, w_in, conv_a_w, conv_qk_w, b_igate, b_fgate, mh_norm_w, w_branch_a, w_branch_m, w_out, norm2_pre, norm2_post, peer_wq, peer_subkeys, peer_u, peer_v):
    for l in range(w_ada.shape[0]):
        x = _layer(x, c, w_ada[l], b_ada[l], norm1_pre[l], norm1_post[l], w_in[l], conv_a_w[l], conv_qk_w[l],
                   b_igate[l], b_fgate[l], mh_norm_w[l], w_branch_a[l], w_branch_m[l], w_out[l], norm2_pre[l],
                   norm2_post[l], peer_wq[l], peer_subkeys[l], peer_u[l], peer_v[l])
    return x
```

```python
import functools

import jax
import jax.numpy as jnp
from jax import lax
from jax.experimental import pallas as pl
from jax.experimental.pallas import tpu as pltpu

F32 = jnp.float32
BF16 = jnp.bfloat16

EPS = 1e-6
M_HEADS = 8
M_DK = 64
M_DV = 128
CONV_K = 3
QK_CONV_K = 4
P_HEADS = 8
N_KEYS = 128
P_TOPK = 16
HALO = 8
LANES = 128
MXU_K = 256
MXU_N = 256
BF16_ROWS = 16
NOT_RANKED = 99.0

VMEM_LIMIT_BYTES = 56 * 1024 * 1024

MLSTM_CHUNK = 128
TS_INPROJ = 256
TS_MIXER = 256
TM_ROUTE = 256
TM_PEER = 1024
TE_PEER = 1024


def _resident(shape):
    nd = len(shape)
    return pl.BlockSpec(shape, lambda *_: (0,) * nd, pipeline_mode=pl.Buffered(1))


def _rms_norm(x, w):
    return x * lax.rsqrt(jnp.mean(x * x, axis=-1, keepdims=True) + EPS) * w


def _sigmoid(x):
    return 1.0 / (1.0 + jnp.exp(-x))


def _packed_words(x):
    return pltpu.bitcast(x, jnp.uint32)


def _unpacked(words):
    return pltpu.bitcast(words, BF16)


def _packed_words_xla(x):
    r, c = x.shape
    return lax.bitcast_convert_type(x.reshape(r // 2, 2, c).swapaxes(1, 2), jnp.uint32)


def _ada_kernel(c_ref, w_ref, b_ref, o_ref):
    c = c_ref[...]
    s = c * _sigmoid(c)
    o_ref[...] = jnp.dot(s.astype(BF16), w_ref[...].astype(BF16), preferred_element_type=F32) + b_ref[...]


def _ada(c, w_ada, b_ada):
    bsz, d = c.shape
    n = w_ada.shape[1]
    return pl.pallas_call(
        _ada_kernel,
        grid=(n // d,),
        in_specs=[pl.BlockSpec((bsz, d), lambda j: (0, 0)),
                  pl.BlockSpec((d, d), lambda j: (0, j)),
                  pl.BlockSpec((1, d), lambda j: (0, j))],
        out_specs=pl.BlockSpec((bsz, d), lambda j: (0, j)),
        out_shape=jax.ShapeDtypeStruct((bsz, n), F32),
        compiler_params=pltpu.CompilerParams(vmem_limit_bytes=VMEM_LIMIT_BYTES),
        name="ada",
    )(c, w_ada, b_ada.reshape(1, n))


def _inproj_kernel(x_ref, ada_ref, npre_ref, w_ref, wifc_ref, wifr_ref, bifc_ref, bifr_ref, cwa_ref, cwqk_ref,
                   ya_ref, q_ref, kt_ref, v_ref, so_ref, sga_ref, sgm_ref, gcol_ref, grow_ref,
                   ubuf, qkbuf, *, ts, d):
    @pl.when(pl.program_id(1) == 0)
    def _():
        ubuf[0:HALO, :] = jnp.zeros((HALO, d), F32)
        qkbuf[0:HALO, :] = jnp.zeros((HALO, d), F32)

    x = x_ref[0]
    ada = ada_ref[0]
    h = _rms_norm(x, npre_ref[...]) * (1.0 + ada[1:2, :]) + ada[0:1, :]
    hb = h.astype(BF16)

    def proj(k):
        return jnp.dot(hb, w_ref[:, k * d:(k + 1) * d], preferred_element_type=F32)

    u = proj(0) * proj(2)
    ubuf[HALO:HALO + ts, :] = u
    conv = u * cwa_ref[CONV_K - 1:CONV_K, :]
    for j in range(CONV_K - 1):
        off = HALO - (CONV_K - 1) + j
        conv = conv + ubuf[off:off + ts, :] * cwa_ref[j:j + 1, :]
    ya_ref[0] = _packed_words((proj(1) * conv).astype(BF16))
    ubuf[0:HALO, :] = ubuf[ts:ts + HALO, :]

    qk = proj(3)
    qkbuf[HALO:HALO + ts, :] = qk
    cq = qk * cwqk_ref[QK_CONV_K - 1:QK_CONV_K, :]
    for j in range(QK_CONV_K - 1):
        off = HALO - (QK_CONV_K - 1) + j
        cq = cq + qkbuf[off:off + ts, :] * cwqk_ref[j:j + 1, :]
    qkbuf[0:HALO, :] = qkbuf[ts:ts + HALO, :]
    cq = cq * _sigmoid(cq)
    nqk = M_HEADS * M_DK
    q_ref[0] = _packed_words((cq[:, :nqk] * (M_DK ** -0.5)).astype(BF16))
    kt_ref[0] = _packed_words(cq[:, nqk:].T.astype(BF16))

    v_ref[0] = _packed_words(proj(4).astype(BF16))
    so_ref[0] = _sigmoid(proj(5)).astype(BF16)
    sga_ref[0] = _sigmoid(proj(6)).astype(BF16)
    sgm_ref[0] = _sigmoid(proj(7)).astype(BF16)

    gcol_ref[0] = jnp.dot(hb, wifc_ref[...], preferred_element_type=F32) + bifc_ref[...]
    grow_ref[0] = lax.dot_general(wifr_ref[...], hb, (((1,), (1,)), ((), ())),
                                  preferred_element_type=F32) + bifr_ref[...]


def _inproj(x, ada3, norm_pre, w_main, wif_col, wif_row, bif_col, bif_row, conv_a_w, conv_qk_w):
    bsz, s, d = x.shape
    ts = min(TS_INPROJ, s)
    ng = 2 * M_HEADS
    nqk = M_HEADS * M_DK
    tok = lambda dt, w=d: jax.ShapeDtypeStruct((bsz, s, w), dt)
    tok_spec = lambda w=d: pl.BlockSpec((1, ts, w), lambda b, i: (b, i, 0))
    words = lambda w=d: jax.ShapeDtypeStruct((bsz, s // 2, w), jnp.uint32)
    words_spec = lambda w=d: pl.BlockSpec((1, ts // 2, w), lambda b, i: (b, i, 0))
    return pl.pallas_call(
        functools.partial(_inproj_kernel, ts=ts, d=d),
        grid=(bsz, s // ts),
        in_specs=[tok_spec(),
                  pl.BlockSpec((1, 6, d), lambda b, i: (b, 0, 0)),
                  _resident((1, d)),
                  _resident(w_main.shape),
                  _resident(wif_col.shape),
                  _resident(wif_row.shape),
                  _resident(bif_col.shape),
                  _resident(bif_row.shape),
                  _resident(conv_a_w.shape),
                  _resident(conv_qk_w.shape)],
        out_specs=[words_spec(), words_spec(nqk),
                   pl.BlockSpec((1, nqk // 2, ts), lambda b, i: (b, 0, i)),
                   words_spec(), tok_spec(), tok_spec(), tok_spec(),
                   tok_spec(LANES),
                   pl.BlockSpec((1, ng, ts), lambda b, i: (b, 0, i))],
        out_shape=[words(), words(nqk),
                   jax.ShapeDtypeStruct((bsz, nqk // 2, s), jnp.uint32),
                   words(), tok(BF16), tok(BF16), tok(BF16),
                   tok(F32, LANES),
                   jax.ShapeDtypeStruct((bsz, ng, s), F32)],
        scratch_shapes=[pltpu.VMEM((HALO + ts, d), F32), pltpu.VMEM((HALO + ts, d), F32)],
        compiler_params=pltpu.CompilerParams(dimension_semantics=("arbitrary", "arbitrary"),
                                             vmem_limit_bytes=VMEM_LIMIT_BYTES),
        name="inproj",
    )(x, ada3, norm_pre, w_main, wif_col, wif_row, bif_col, bif_row, conv_a_w, conv_qk_w)


def _log_sigmoid(x):
    return jnp.minimum(x, 0.0) - jnp.log(1.0 + jnp.exp(-jnp.abs(x)))


def _chunk_scan(x, axis, chunk, op, identity):
    pos = lax.broadcasted_iota(jnp.int32, x.shape, axis) % chunk
    k = 1
    while k < chunk:
        x = op(x, jnp.where(pos >= k, pltpu.roll(x, k, axis), identity))
        k *= 2
    return x


def _bf16_terms(x):
    hi = x.astype(BF16)
    r = x - hi.astype(F32)
    mid = r.astype(BF16)
    lo = (r - mid.astype(F32)).astype(BF16)
    return hi, mid, lo


def _spread_heads(x, sel):
    return sum(jnp.dot(t, sel, preferred_element_type=F32) for t in _bf16_terms(x))


def _mixer_kernel(x_ref, ada_ref, q_ref, kt_ref, v_ref, so_ref, sga_ref, sgm_ref, ya_ref, gcol_ref, grow_ref,
                  sel_ref, mhw_ref, npost_ref, wa_ref, wm_ref, wo_ref, o_ref,
                  c_st, mcol_st, mrow_st, ym_buf, *, ts, chunk):
    first = pl.program_id(1) == 0

    @pl.when(first)
    def _():
        c_st[...] = jnp.zeros(c_st.shape, F32)
        mcol_st[...] = jnp.zeros(mcol_st.shape, F32)
        mrow_st[...] = jnp.zeros(mrow_st.shape, F32)

    nc = ts // chunk
    H = M_HEADS
    neg_inf = -jnp.inf

    gcol = gcol_ref[0]
    b_col = _chunk_scan(_log_sigmoid(gcol), 0, chunk, jnp.add, 0.0)
    r_col = pltpu.roll(gcol, H, 1) - b_col
    cm_col = _chunk_scan(r_col, 0, chunk, jnp.maximum, neg_inf)
    m = mcol_st[0:1, :]
    m_rows, decay_rows = [], []
    for c in range(nc):
        last = (c + 1) * chunk - 1
        b_l, cm_l = b_col[last:last + 1, :], cm_col[last:last + 1, :]
        m_next = b_l + jnp.maximum(m, cm_l)
        m_rows.append(m)
        decay_rows.append(jnp.exp(b_l + m - m_next))
        m = m_next
    mcol_st[...] = jnp.broadcast_to(m, mcol_st.shape)
    pad = jnp.zeros((8 - (2 * nc) % 8, LANES), F32)
    spread = _spread_heads(jnp.concatenate([b_col, cm_col] + m_rows + decay_rows + [pad], axis=0), sel_ref[...])
    bc_all, cm_all, sc_all = spread[0:ts], spread[ts:2 * ts], spread[2 * ts:]

    grow = grow_ref[0]
    i_row = grow[0:H, :]
    b_row = _chunk_scan(_log_sigmoid(grow[H:2 * H, :]), 1, chunk, jnp.add, 0.0)
    r_row = i_row - b_row
    cm_row = _chunk_scan(r_row, 1, chunk, jnp.maximum, neg_inf)
    mr = mrow_st[:, 0:1]
    ws_rows = []
    for c in range(nc):
        last = (c + 1) * chunk - 1
        b_l, cm_l = b_row[:, last:last + 1], cm_row[:, last:last + 1]
        m_next = b_l + jnp.maximum(mr, cm_l)
        ws_rows.append(jnp.exp(b_l + r_row[:, c * chunk:(c + 1) * chunk] - m_next))
        mr = m_next
    mrow_st[...] = jnp.broadcast_to(mr, mrow_st.shape)

    tril = (lax.broadcasted_iota(jnp.int32, (chunk, chunk), 0)
            >= lax.broadcasted_iota(jnp.int32, (chunk, chunk), 1))
    ones = jnp.ones((chunk, M_DV), BF16)

    for hd in range(H):
        hl = slice(hd * M_DV, (hd + 1) * M_DV)
        c_aug = c_st[hd]
        for c in range(nc):
            rows = slice(c * chunk, (c + 1) * chunk)
            wrows = slice(c * chunk // 2, (c + 1) * chunk // 2)
            q = _unpacked(q_ref[0, wrows, hd * M_DK:(hd + 1) * M_DK])
            kt = _unpacked(kt_ref[0, hd * M_DK // 2:(hd + 1) * M_DK // 2, rows])
            v = _unpacked(v_ref[0, wrows, hl])
            bc = bc_all[rows, hl]
            m_prev = sc_all[c:c + 1, hl]
            decay = sc_all[nc + c:nc + c + 1, hl]
            m_t = bc + jnp.maximum(m_prev, cm_all[rows, hl])
            dlog = jnp.where(tril, bc[:, :chunk] + r_row[hd:hd + 1, rows], neg_inf)
            s_qk = jnp.dot(q, kt, preferred_element_type=F32) * jnp.exp(dlog - m_t[:, :chunk])
            a_inter = jnp.exp(bc + m_prev - m_t)
            qc = jnp.dot(q, c_aug.astype(BF16), preferred_element_type=F32)
            num = jnp.dot(s_qk.astype(BF16), v, preferred_element_type=F32) + a_inter * qc[:, :M_DV]
            den = jnp.sum(s_qk, axis=1, keepdims=True) + a_inter * qc[:, M_DV:]
            h = num / jnp.maximum(jnp.abs(den), jnp.exp(-m_t))
            hn = _rms_norm(h, mhw_ref[:, hl])
            ym_buf[rows, hl] = (so_ref[0, rows, hl].astype(F32) * hn).astype(BF16)
            ktw = (kt.astype(F32) * ws_rows[c][hd:hd + 1, :]).astype(BF16)
            upd = jnp.dot(ktw, jnp.concatenate([v, ones], axis=1), preferred_element_type=F32)
            c_aug = jnp.concatenate([decay, decay], axis=1) * c_aug + upd
        c_st[hd] = c_aug

    mix = (sga_ref[0].astype(F32) * jnp.dot(_unpacked(ya_ref[0]), wa_ref[...], preferred_element_type=F32)
           + sgm_ref[0].astype(F32) * jnp.dot(ym_buf[...], wm_ref[...], preferred_element_type=F32))
    y = jnp.dot(mix.astype(BF16), wo_ref[...], preferred_element_type=F32)
    o_ref[0] = x_ref[0] + ada_ref[0][2:3, :] * _rms_norm(y, npost_ref[...])


def _mixer(x, ada3, q, kt, v, so, sga, sgm, ya, gcol, grow, mh_norm_w, norm_post, wa, wm, wo):
    bsz, s, d = x.shape
    ts = min(TS_MIXER, s)
    chunk = min(MLSTM_CHUNK, ts)
    ng = 2 * M_HEADS
    nqk = M_HEADS * M_DK
    sel = (jnp.arange(LANES)[:, None] - M_HEADS == jnp.arange(M_HEADS * M_DV)[None, :] // M_DV).astype(BF16)
    tok_spec = lambda w=d: pl.BlockSpec((1, ts, w), lambda b, i: (b, i, 0))
    words_spec = lambda w=d: pl.BlockSpec((1, ts // 2, w), lambda b, i: (b, i, 0))
    return pl.pallas_call(
        functools.partial(_mixer_kernel, ts=ts, chunk=chunk),
        grid=(bsz, s // ts),
        in_specs=[tok_spec(),
                  pl.BlockSpec((1, 6, d), lambda b, i: (b, 0, 0)),
                  words_spec(nqk),
                  pl.BlockSpec((1, nqk // 2, ts), lambda b, i: (b, 0, i)),
                  words_spec(), tok_spec(), tok_spec(), tok_spec(), words_spec(),
                  tok_spec(LANES),
                  pl.BlockSpec((1, ng, ts), lambda b, i: (b, 0, i)),
                  _resident(sel.shape),
                  _resident((1, d)), _resident((1, d)),
                  _resident(wa.shape), _resident(wm.shape), _resident(wo.shape)],
        out_specs=tok_spec(),
        out_shape=jax.ShapeDtypeStruct((bsz, s, d), F32),
        scratch_shapes=[pltpu.VMEM((M_HEADS, M_DK, 2 * M_DV), F32),
                        pltpu.VMEM((8, LANES), F32),
                        pltpu.VMEM((M_HEADS, LANES), F32),
                        pltpu.VMEM((ts, d), BF16)],
        compiler_params=pltpu.CompilerParams(dimension_semantics=("arbitrary", "arbitrary"),
                                             vmem_limit_bytes=VMEM_LIMIT_BYTES),
        name="mixer",
    )(x, ada3, q, kt, v, so, sga, sgm, ya, gcol, grow, sel, mh_norm_w, norm_post, wa, wm, wo)


def _argmax_rows(s, exact):
    m = jnp.max(s, axis=0, keepdims=True)
    hit = s == m
    if exact:
        row = lax.broadcasted_iota(jnp.int32, s.shape, 0)
        hit = row == jnp.min(jnp.where(hit, row, s.shape[0]), axis=0, keepdims=True)
    return hit, m


def _topk_rows(s, k, exact):
    rank = jnp.full(s.shape, NOT_RANKED, F32)
    vals = []
    for r in range(k):
        hit, m = _argmax_rows(s, exact)
        rank = jnp.where(hit, float(r), rank)
        s = jnp.where(hit, -jnp.inf, s)
        vals.append(m)
    return vals, rank


def _route_tables(s1, s2, exact):
    v1, rank1 = _topk_rows(s1, P_TOPK, exact)
    v2, rank2 = _topk_rows(s2, P_TOPK, exact)
    v2_lo = jnp.concatenate(v2[0:8], axis=0)
    v2_all = jnp.concatenate(v2, axis=0)
    v1_hi = jnp.concatenate(v1[8:16], axis=0)
    cand = jnp.concatenate([v1[0] + v2_all] + [v1[j] + v2_lo for j in range(1, 8)] + [v1_hi + v2[0]], axis=0)
    sel = jnp.zeros(cand.shape, F32)
    cur = cand
    for _ in range(P_TOPK):
        hit, _ = _argmax_rows(cur, exact)
        sel = jnp.where(hit, 1.0, sel)
        cur = jnp.where(hit, -jnp.inf, cur)
    ranked = lambda rk: jnp.sum(jnp.where(rk < NOT_RANKED, 1.0, 0.0), axis=0, keepdims=True)
    picked = jnp.maximum(jnp.maximum(ranked(rank1), ranked(rank2)), jnp.sum(sel, axis=0, keepdims=True))
    tied = jnp.where(picked > float(P_TOPK), 1.0, 0.0)
    top = v1[0] + v2[0]
    z = jnp.sum(sel * jnp.exp(cand - top), axis=0, keepdims=True)
    counts = [jnp.sum(sel[0:16], axis=0, keepdims=True)]
    counts += [jnp.sum(sel[8 + 8 * j:16 + 8 * j], axis=0, keepdims=True) for j in range(1, 8)]
    counts += [sel[72 + j:73 + j] for j in range(8)]
    cnt = jnp.zeros(s1.shape, F32)
    for j in range(P_TOPK):
        cnt = jnp.where(rank1 == float(j), counts[j], cnt)
    p1 = jnp.exp(s1 - v1[0])
    p2 = jnp.exp(s2 - v2[0]) * (1.0 / z)
    return cnt, rank2, p1, p2, tied


def _bf16_pair_words(x):
    hi = pltpu.bitcast(x.astype(BF16).astype(F32), jnp.uint32)
    return hi | lax.shift_right_logical(hi, jnp.uint32(16))


def _route_kernel(x_ref, ada_ref, npre_ref, wqt_ref, sk_ref, h2t_ref, cnt_ref, rk2_ref, p1_ref, p2_ref, qt_buf,
                  *, tm):
    ada = ada_ref[0]
    h2 = _rms_norm(x_ref[...], npre_ref[...]) * (1.0 + ada[4:5, :]) + ada[3:4, :]
    h2t = h2.T.astype(BF16)
    h2t_ref[...] = _packed_words(h2t)
    qt_buf[...] = jnp.dot(wqt_ref[...], h2t, preferred_element_type=F32).astype(BF16)

    def head(hd, carry):
        r0 = pl.multiple_of(hd * (2 * N_KEYS), 2 * N_KEYS)
        s1 = jnp.dot(sk_ref[hd, 0], qt_buf[pl.ds(r0, N_KEYS), :], preferred_element_type=F32)
        s2 = jnp.dot(sk_ref[hd, 1], qt_buf[pl.ds(r0 + N_KEYS, N_KEYS), :], preferred_element_type=F32)

        def tables(lc, exact):
            sl = slice(lc * LANES, (lc + 1) * LANES)
            cnt, rk2, p1, p2, tied = _route_tables(s1[:, sl], s2[:, sl], exact)
            cnt_ref[hd, lc] = _bf16_pair_words(cnt)
            rk2_ref[hd, lc] = _packed_words(rk2.astype(BF16))
            p1_ref[hd, lc] = _bf16_pair_words(p1)
            p2_ref[hd, lc] = _packed_words(p2.astype(BF16))
            return tied

        for lc in range(tm // LANES):
            tied = tables(lc, exact=False)

            @pl.when(jnp.max(tied) > 0.0)
            def _():
                tables(lc, exact=True)
        return carry

    lax.fori_loop(0, P_HEADS, head, 0)


def _route(x1, ada3, norm_pre, wqt, subkeys):
    t, d = x1.shape
    s = t // ada3.shape[0]
    tm = min(TM_ROUTE, s)
    tab = lambda rows: jax.ShapeDtypeStruct((P_HEADS, t // LANES, rows, LANES), jnp.uint32)
    tab_spec = lambda rows: pl.BlockSpec((P_HEADS, tm // LANES, rows, LANES), lambda i: (0, i, 0, 0))
    k1, k2 = N_KEYS, N_KEYS // 2
    return pl.pallas_call(
        functools.partial(_route_kernel, tm=tm),
        grid=(t // tm,),
        in_specs=[pl.BlockSpec((tm, d), lambda i: (i, 0)),
                  pl.BlockSpec((1, 6, d), lambda i: ((i * tm) // s, 0, 0)),
                  _resident((1, d)),
                  _resident(wqt.shape),
                  _resident(subkeys.shape)],
        out_specs=[pl.BlockSpec((d // 2, tm), lambda i: (0, i)),
                   tab_spec(k1), tab_spec(k2), tab_spec(k1), tab_spec(k2)],
        out_shape=[jax.ShapeDtypeStruct((d // 2, t), jnp.uint32), tab(k1), tab(k2), tab(k1), tab(k2)],
        scratch_shapes=[pltpu.VMEM((wqt.shape[0], tm), BF16)],
        compiler_params=pltpu.CompilerParams(dimension_semantics=("arbitrary",),
                                             vmem_limit_bytes=VMEM_LIMIT_BYTES),
        name="route",
    )(x1, ada3, norm_pre, wqt, subkeys)


def _gelu_tanh(z):
    return 0.5 * z * (1.0 + jnp.tanh(0.7978845608028654 * (z + 0.044715 * (z * z * z))))


def _peer_kernel(h2t_ref, u_ref, vt_ref, cnt_ref, rk2_ref, p1_ref, p2_ref, x_ref, ada_ref, npost_ref, o_ref,
                 acc, g_buf, z_buf, a_buf, *, tm, te):
    j = pl.program_id(1)

    @pl.when(j == 0)
    def _():
        acc[...] = jnp.zeros(acc.shape, F32)

    na = te // N_KEYS
    apc = MXU_K // N_KEYS
    nkc = te // MXU_K
    nlc = tm // LANES
    nlh = tm // MXU_N
    d = acc.shape[0]
    rh = d // 2

    def gate_block(kc, ac, lc):
        al = kc * apc + ac
        pk = (N_KEYS // BF16_ROWS, BF16_ROWS, LANES)
        gate = None
        for hd in range(P_HEADS):
            cnt_a = pltpu.bitcast(jnp.broadcast_to(cnt_ref[hd, lc, al:al + 1, :], (8, LANES)), BF16)
            p1_a = pltpu.bitcast(jnp.broadcast_to(p1_ref[hd, lc, al:al + 1, :], (8, LANES)), BF16)
            hit = _unpacked(rk2_ref[hd, lc]).reshape(pk) < cnt_a[None]
            term = jnp.where(hit, _unpacked(p2_ref[hd, lc]).reshape(pk) * p1_a[None], jnp.zeros((), BF16))
            gate = term if gate is None else gate + term
        g_buf[kc % 2, ac, lc] = gate.reshape(N_KEYS, LANES)

    def score_piece(kc, ac, lh):
        r0 = kc * MXU_K + ac * N_KEYS
        z_buf[kc % 2, ac * N_KEYS:(ac + 1) * N_KEYS, lh * MXU_N:(lh + 1) * MXU_N] = jnp.dot(
            _unpacked(u_ref[r0 // 2:(r0 + N_KEYS) // 2, :]),
            _unpacked(h2t_ref[:, lh * MXU_N:(lh + 1) * MXU_N]), preferred_element_type=F32)

    def act_block(kc, ac, lc):
        r0 = kc * MXU_K + ac * N_KEYS
        zb = z_buf[kc % 2, ac * N_KEYS:(ac + 1) * N_KEYS, lc * LANES:(lc + 1) * LANES]
        a_buf[r0:r0 + N_KEYS, lc * LANES:(lc + 1) * LANES] = _gelu_tanh(zb).astype(BF16) * g_buf[kc % 2, ac, lc]

    outs = {}

    def out_piece(kc, r, lh):
        e0 = kc * MXU_K
        part = jnp.dot(_unpacked(vt_ref[r * rh // 2:(r + 1) * rh // 2, e0:e0 + MXU_K]),
                       a_buf[e0:e0 + MXU_K, lh * MXU_N:(lh + 1) * MXU_N], preferred_element_type=F32)
        outs[(r, lh)] = part if (r, lh) not in outs else outs[(r, lh)] + part

    blocks = [(ac, lc) for lc in range(nlc) for ac in range(apc)]
    pieces = [(ac, lh) for lh in range(nlh) for ac in range(apc)]
    for s, blk in enumerate(blocks):
        gate_block(0, *blk)
        if s % 2 == 1:
            score_piece(0, *pieces[s // 2])
    pending = []
    for kc in range(nkc):
        mxu = list(pending)
        pending = []
        if kc + 1 < nkc:
            mxu += [functools.partial(score_piece, kc + 1, ac, lh) for ac, lh in pieces]
        for s, (ac, lc) in enumerate(blocks):
            if kc + 1 < nkc:
                gate_block(kc + 1, ac, lc)
            act_block(kc, ac, lc)
            if mxu:
                mxu.pop(0)()
            if ac == apc - 1 and (lc + 1) % (MXU_N // LANES) == 0:
                lh = lc // (MXU_N // LANES)
                ready = [functools.partial(out_piece, kc, r, lh) for r in range(d // rh)]
                if lh + 1 < nlh:
                    mxu += ready
                else:
                    pending = ready
        for f in mxu:
            f()
    for f in pending:
        f()
    for (r, lh), val in outs.items():
        acc[r * rh:(r + 1) * rh, lh * MXU_N:(lh + 1) * MXU_N] += val

    @pl.when(j == pl.num_programs(1) - 1)
    def _():
        y = acc[...].T
        o_ref[...] = x_ref[...] + ada_ref[0][5:6, :] * _rms_norm(y, npost_ref[...])


def _peer(h2t, u_bf, vt_bf, cnt, rk2, p1, p2, x1, ada3, norm_post):
    t, d = x1.shape
    s = t // ada3.shape[0]
    ne = 2 * u_bf.shape[0]
    tm = min(TM_PEER, s)
    te = TE_PEER
    once = pl.Buffered(1)
    tab2_spec = pl.BlockSpec((P_HEADS, tm // LANES, N_KEYS // 2, LANES), lambda i, j: (0, i, 0, 0))
    tab1_spec = pl.BlockSpec((P_HEADS, tm // LANES, te // N_KEYS, LANES), lambda i, j: (0, i, j, 0))
    return pl.pallas_call(
        functools.partial(_peer_kernel, tm=tm, te=te),
        grid=(t // tm, ne // te),
        in_specs=[pl.BlockSpec((d // 2, tm), lambda i, j: (0, i)),
                  pl.BlockSpec((te // 2, d), lambda i, j: (j, 0)),
                  pl.BlockSpec((d // 2, te), lambda i, j: (0, j)),
                  tab1_spec, tab2_spec, tab1_spec, tab2_spec,
                  pl.BlockSpec((tm, d), lambda i, j: (i, 0), pipeline_mode=once),
                  pl.BlockSpec((1, 6, d), lambda i, j: ((i * tm) // s, 0, 0)),
                  _resident((1, d))],
        out_specs=pl.BlockSpec((tm, d), lambda i, j: (i, 0), pipeline_mode=once),
        out_shape=jax.ShapeDtypeStruct((t, d), F32),
        scratch_shapes=[pltpu.VMEM((d, tm), F32),
                        pltpu.VMEM((2, MXU_K // N_KEYS, tm // LANES, N_KEYS, LANES), BF16),
                        pltpu.VMEM((2, MXU_K, tm), F32),
                        pltpu.VMEM((te, tm), BF16)],
        compiler_params=pltpu.CompilerParams(dimension_semantics=("arbitrary", "arbitrary"),
                                             vmem_limit_bytes=VMEM_LIMIT_BYTES),
        name="peer",
    )(h2t, u_bf, vt_bf, cnt, rk2, p1, p2, x1, ada3, norm_post)


def _layer(x, c, w_ada, b_ada, norm1_pre, norm1_post, w_in, conv_a_w, conv_qk_w, b_igate, b_fgate, mh_norm_w,
           w_branch_a, w_branch_m, w_out, norm2_pre, norm2_post, peer_wq, peer_subkeys, peer_u, peer_v):
    bsz, s, d = x.shape
    ng = 2 * M_HEADS
    ada3 = _ada(c, w_ada, b_ada).reshape(bsz, 6, d)

    o_if = 3 * d + 2 * M_HEADS * M_DK + 2 * d
    w_main = jnp.concatenate([w_in[:, :o_if], w_in[:, o_if + ng:]], axis=1).astype(BF16)
    w_if = w_in[:, o_if:o_if + ng]
    wif_col = jnp.pad(w_if, ((0, 0), (0, LANES - ng))).astype(BF16)
    wif_row = w_if.T.astype(BF16)
    b_if = jnp.concatenate([b_igate, b_fgate])
    bif_col = jnp.pad(b_if, (0, LANES - ng)).reshape(1, LANES)
    bif_row = b_if.reshape(ng, 1)

    ya, q, kt, v, so, sga, sgm, gcol, grow = _inproj(
        x, ada3, norm1_pre.reshape(1, d), w_main, wif_col, wif_row, bif_col, bif_row, conv_a_w, conv_qk_w)
    x1 = _mixer(x, ada3, q, kt, v, so, sga, sgm, ya, gcol, grow, mh_norm_w.reshape(1, d),
                norm1_post.reshape(1, d), w_branch_a.astype(BF16), w_branch_m.astype(BF16), w_out.astype(BF16))

    x1f = x1.reshape(bsz * s, d)
    h2t, cnt, rk2, p1, p2 = _route(x1f, ada3, norm2_pre.reshape(1, d), peer_wq.T.astype(BF16),
                                   peer_subkeys.astype(BF16))
    out = _peer(h2t, _packed_words_xla(peer_u.astype(BF16)), _packed_words_xla(peer_v.T.astype(BF16)),
                cnt, rk2, p1, p2, x1f, ada3, norm2_post.reshape(1, d))
    return out.reshape(bsz, s, d)


def kernel(x, c, w_ada, b_ada, norm1_pre, norm1_post, w_in, conv_a_w, conv_qk_w, b_igate, b_fgate, mh_norm_w, w_branch_a, w_branch_m, w_out, norm2_pre, norm2_post, peer_wq, peer_subkeys, peer_u, peer_v):
    for l in range(w_ada.shape[0]):
        x = _layer(x, c, w_ada[l], b_ada[l], norm1_pre[l], norm1_post[l], w_in[l], conv_a_w[l], conv_qk_w[l],
                   b_igate[l], b_fgate[l], mh_norm_w[l], w_branch_a[l], w_branch_m[l], w_out[l], norm2_pre[l],
                   norm2_post[l], peer_wq[l], peer_subkeys[l], peer_u[l], peer_v[l])
    return x
```

```python
import functools

import jax
import jax.numpy as jnp
from jax import lax
from jax.experimental import pallas as pl
from jax.experimental.pallas import tpu as pltpu

F32 = jnp.float32
BF16 = jnp.bfloat16

EPS = 1e-6
M_HEADS = 8
M_DK = 64
M_DV = 128
CONV_K = 3
QK_CONV_K = 4
P_HEADS = 8
N_KEYS = 128
P_TOPK = 16
HALO = 8
LANES = 128
MXU_K = 256
MXU_N = 256
BF16_ROWS = 16
NOT_RANKED = 99.0

VMEM_LIMIT_BYTES = 56 * 1024 * 1024

MLSTM_CHUNK = 128
TS_INPROJ = 256
TS_MIXER = 256
TM_ROUTE = 256
TM_PEER = 1024
TE_PEER = 1024


def _resident(shape):
    nd = len(shape)
    return pl.BlockSpec(shape, lambda *_: (0,) * nd, pipeline_mode=pl.Buffered(1))


def _rms_norm(x, w):
    return x * lax.rsqrt(jnp.mean(x * x, axis=-1, keepdims=True) + EPS) * w


def _sigmoid(x):
    return 1.0 / (1.0 + jnp.exp(-x))


def _packed_words(x):
    return pltpu.bitcast(x, jnp.uint32)


def _unpacked(words):
    return pltpu.bitcast(words, BF16)


def _packed_words_xla(x):
    r, c = x.shape
    return lax.bitcast_convert_type(x.reshape(r // 2, 2, c).swapaxes(1, 2), jnp.uint32)


def _ada_kernel(c_ref, w_ref, b_ref, o_ref):
    c = c_ref[...]
    s = c * _sigmoid(c)
    o_ref[...] = jnp.dot(s.astype(BF16), w_ref[...].astype(BF16), preferred_element_type=F32) + b_ref[...]


def _ada(c, w_ada, b_ada):
    bsz, d = c.shape
    n = w_ada.shape[1]
    return pl.pallas_call(
        _ada_kernel,
        grid=(n // d,),
        in_specs=[pl.BlockSpec((bsz, d), lambda j: (0, 0)),
                  pl.BlockSpec((d, d), lambda j: (0, j)),
                  pl.BlockSpec((1, d), lambda j: (0, j))],
        out_specs=pl.BlockSpec((bsz, d), lambda j: (0, j)),
        out_shape=jax.ShapeDtypeStruct((bsz, n), F32),
        compiler_params=pltpu.CompilerParams(vmem_limit_bytes=VMEM_LIMIT_BYTES),
        name="ada",
    )(c, w_ada, b_ada.reshape(1, n))


def _inproj_kernel(x_ref, ada_ref, npre_ref, w_ref, wifc_ref, wifr_ref, bifc_ref, bifr_ref, cwa_ref, cwqk_ref,
                   ya_ref, q_ref, kt_ref, v_ref, so_ref, sga_ref, sgm_ref, gcol_ref, grow_ref,
                   ubuf, qkbuf, *, ts, d):
    @pl.when(pl.program_id(1) == 0)
    def _():
        ubuf[0:HALO, :] = jnp.zeros((HALO, d), F32)
        qkbuf[0:HALO, :] = jnp.zeros((HALO, d), F32)

    x = x_ref[0]
    ada = ada_ref[0]
    h = _rms_norm(x, npre_ref[...]) * (1.0 + ada[1:2, :]) + ada[0:1, :]
    hb = h.astype(BF16)

    def proj(k):
        return jnp.dot(hb, w_ref[:, k * d:(k + 1) * d], preferred_element_type=F32)

    u = proj(0) * proj(2)
    ubuf[HALO:HALO + ts, :] = u
    conv = u * cwa_ref[CONV_K - 1:CONV_K, :]
    for j in range(CONV_K - 1):
        off = HALO - (CONV_K - 1) + j
        conv = conv + ubuf[off:off + ts, :] * cwa_ref[j:j + 1, :]
    ya_ref[0] = _packed_words((proj(1) * conv).astype(BF16))
    ubuf[0:HALO, :] = ubuf[ts:ts + HALO, :]

    qk = proj(3)
    qkbuf[HALO:HALO + ts, :] = qk
    cq = qk * cwqk_ref[QK_CONV_K - 1:QK_CONV_K, :]
    for j in range(QK_CONV_K - 1):
        off = HALO - (QK_CONV_K - 1) + j
        cq = cq + qkbuf[off:off + ts, :] * cwqk_ref[j:j + 1, :]
    qkbuf[0:HALO, :] = qkbuf[ts:ts + HALO, :]
    cq = cq * _sigmoid(cq)
    nqk = M_HEADS * M_DK
    q_ref[0] = _packed_words((cq[:, :nqk] * (M_DK ** -0.5)).astype(BF16))
    kt_ref[0] = _packed_words(cq[:, nqk:].T.astype(BF16))

    v_ref[0] = _packed_words(proj(4).astype(BF16))
    so_ref[0] = _sigmoid(proj(5)).astype(BF16)
    sga_ref[0] = _sigmoid(proj(6)).astype(BF16)
    sgm_ref[0] = _sigmoid(proj(7)).astype(BF16)

    gcol_ref[0] = jnp.dot(hb, wifc_ref[...], preferred_element_type=F32) + bifc_ref[...]
    grow_ref[0] = lax.dot_general(wifr_ref[...], hb, (((1,), (1,)), ((), ())),
                                  preferred_element_type=F32) + bifr_ref[...]


def _inproj(x, ada3, norm_pre, w_main, wif_col, wif_row, bif_col, bif_row, conv_a_w, conv_qk_w):
    bsz, s, d = x.shape
    ts = min(TS_INPROJ, s)
    ng = 2 * M_HEADS
    nqk = M_HEADS * M_DK
    tok = lambda dt, w=d: jax.ShapeDtypeStruct((bsz, s, w), dt)
    tok_spec = lambda w=d: pl.BlockSpec((1, ts, w), lambda b, i: (b, i, 0))
    words = lambda w=d: jax.ShapeDtypeStruct((bsz, s // 2, w), jnp.uint32)
    words_spec = lambda w=d: pl.BlockSpec((1, ts // 2, w), lambda b, i: (b, i, 0))
    return pl.pallas_call(
        functools.partial(_inproj_kernel, ts=ts, d=d),
        grid=(bsz, s // ts),
        in_specs=[tok_spec(),
                  pl.BlockSpec((1, 6, d), lambda b, i: (b, 0, 0)),
                  _resident((1, d)),
                  _resident(w_main.shape),
                  _resident(wif_col.shape),
                  _resident(wif_row.shape),
                  _resident(bif_col.shape),
                  _resident(bif_row.shape),
                  _resident(conv_a_w.shape),
                  _resident(conv_qk_w.shape)],
        out_specs=[words_spec(), words_spec(nqk),
                   pl.BlockSpec((1, nqk // 2, ts), lambda b, i: (b, 0, i)),
                   words_spec(), tok_spec(), tok_spec(), tok_spec(),
                   tok_spec(LANES),
                   pl.BlockSpec((1, ng, ts), lambda b, i: (b, 0, i))],
        out_shape=[words(), words(nqk),
                   jax.ShapeDtypeStruct((bsz, nqk // 2, s), jnp.uint32),
                   words(), tok(BF16), tok(BF16), tok(BF16),
                   tok(F32, LANES),
                   jax.ShapeDtypeStruct((bsz, ng, s), F32)],
        scratch_shapes=[pltpu.VMEM((HALO + ts, d), F32), pltpu.VMEM((HALO + ts, d), F32)],
        compiler_params=pltpu.CompilerParams(dimension_semantics=("arbitrary", "arbitrary"),
                                             vmem_limit_bytes=VMEM_LIMIT_BYTES),
        name="inproj",
    )(x, ada3, norm_pre, w_main, wif_col, wif_row, bif_col, bif_row, conv_a_w, conv_qk_w)


def _log_sigmoid(x):
    return jnp.minimum(x, 0.0) - jnp.log(1.0 + jnp.exp(-jnp.abs(x)))


def _chunk_scan(x, axis, chunk, op, identity):
    pos = lax.broadcasted_iota(jnp.int32, x.shape, axis) % chunk
    k = 1
    while k < chunk:
        x = op(x, jnp.where(pos >= k, pltpu.roll(x, k, axis), identity))
        k *= 2
    return x


def _bf16_terms(x):
    hi = x.astype(BF16)
    r = x - hi.astype(F32)
    mid = r.astype(BF16)
    lo = (r - mid.astype(F32)).astype(BF16)
    return hi, mid, lo


def _spread_heads(x, sel):
    return sum(jnp.dot(t, sel, preferred_element_type=F32) for t in _bf16_terms(x))


def _mixer_kernel(x_ref, ada_ref, q_ref, kt_ref, v_ref, so_ref, sga_ref, sgm_ref, ya_ref, gcol_ref, grow_ref,
                  sel_ref, mhw_ref, npost_ref, wa_ref, wm_ref, wo_ref, o_ref,
                  c_st, mcol_st, mrow_st, ym_buf, *, ts, chunk):
    first = pl.program_id(1) == 0

    @pl.when(first)
    def _():
        c_st[...] = jnp.zeros(c_st.shape, F32)
        mcol_st[...] = jnp.zeros(mcol_st.shape, F32)
        mrow_st[...] = jnp.zeros(mrow_st.shape, F32)

    nc = ts // chunk
    H = M_HEADS
    neg_inf = -jnp.inf

    gcol = gcol_ref[0]
    b_col = _chunk_scan(_log_sigmoid(gcol), 0, chunk, jnp.add, 0.0)
    r_col = pltpu.roll(gcol, H, 1) - b_col
    cm_col = _chunk_scan(r_col, 0, chunk, jnp.maximum, neg_inf)
    m = mcol_st[0:1, :]
    m_rows, decay_rows = [], []
    for c in range(nc):
        last = (c + 1) * chunk - 1
        b_l, cm_l = b_col[last:last + 1, :], cm_col[last:last + 1, :]
        m_next = b_l + jnp.maximum(m, cm_l)
        m_rows.append(m)
        decay_rows.append(jnp.exp(b_l + m - m_next))
        m = m_next
    mcol_st[...] = jnp.broadcast_to(m, mcol_st.shape)
    pad = jnp.zeros((8 - (2 * nc) % 8, LANES), F32)
    spread = _spread_heads(jnp.concatenate([b_col, cm_col] + m_rows + decay_rows + [pad], axis=0), sel_ref[...])
    bc_all, cm_all, sc_all = spread[0:ts], spread[ts:2 * ts], spread[2 * ts:]

    grow = grow_ref[0]
    i_row = grow[0:H, :]
    b_row = _chunk_scan(_log_sigmoid(grow[H:2 * H, :]), 1, chunk, jnp.add, 0.0)
    r_row = i_row - b_row
    cm_row = _chunk_scan(r_row, 1, chunk, jnp.maximum, neg_inf)
    mr = mrow_st[:, 0:1]
    ws_rows = []
    for c in range(nc):
        last = (c + 1) * chunk - 1
        b_l, cm_l = b_row[:, last:last + 1], cm_row[:, last:last + 1]
        m_next = b_l + jnp.maximum(mr, cm_l)
        ws_rows.append(jnp.exp(b_l + r_row[:, c * chunk:(c + 1) * chunk] - m_next))
        mr = m_next
    mrow_st[...] = jnp.broadcast_to(mr, mrow_st.shape)

    tril = (lax.broadcasted_iota(jnp.int32, (chunk, chunk), 0)
            >= lax.broadcasted_iota(jnp.int32, (chunk, chunk), 1))
    ones = jnp.ones((chunk, M_DV), BF16)

    for hd in range(H):
        hl = slice(hd * M_DV, (hd + 1) * M_DV)
        c_aug = c_st[hd]
        for c in range(nc):
            rows = slice(c * chunk, (c + 1) * chunk)
            wrows = slice(c * chunk // 2, (c + 1) * chunk // 2)
            q = _unpacked(q_ref[0, wrows, hd * M_DK:(hd + 1) * M_DK])
            kt = _unpacked(kt_ref[0, hd * M_DK // 2:(hd + 1) * M_DK // 2, rows])
            v = _unpacked(v_ref[0, wrows, hl])
            bc = bc_all[rows, hl]
            m_prev = sc_all[c:c + 1, hl]
            decay = sc_all[nc + c:nc + c + 1, hl]
            m_t = bc + jnp.maximum(m_prev, cm_all[rows, hl])
            dlog = jnp.where(tril, bc[:, :chunk] + r_row[hd:hd + 1, rows], neg_inf)
            s_qk = jnp.dot(q, kt, preferred_element_type=F32) * jnp.exp(dlog - m_t[:, :chunk])
            a_inter = jnp.exp(bc + m_prev - m_t)
            qc = jnp.dot(q, c_aug.astype(BF16), preferred_element_type=F32)
            num = jnp.dot(s_qk.astype(BF16), v, preferred_element_type=F32) + a_inter * qc[:, :M_DV]
            den = jnp.sum(s_qk, axis=1, keepdims=True) + a_inter * qc[:, M_DV:]
            h = num / jnp.maximum(jnp.abs(den), jnp.exp(-m_t))
            hn = _rms_norm(h, mhw_ref[:, hl])
            ym_buf[rows, hl] = (so_ref[0, rows, hl].astype(F32) * hn).astype(BF16)
            ktw = (kt.astype(F32) * ws_rows[c][hd:hd + 1, :]).astype(BF16)
            upd = jnp.dot(ktw, jnp.concatenate([v, ones], axis=1), preferred_element_type=F32)
            c_aug = jnp.concatenate([decay, decay], axis=1) * c_aug + upd
        c_st[hd] = c_aug

    mix = (sga_ref[0].astype(F32) * jnp.dot(_unpacked(ya_ref[0]), wa_ref[...], preferred_element_type=F32)
           + sgm_ref[0].astype(F32) * jnp.dot(ym_buf[...], wm_ref[...], preferred_element_type=F32))
    y = jnp.dot(mix.astype(BF16), wo_ref[...], preferred_element_type=F32)
    o_ref[0] = x_ref[0] + ada_ref[0][2:3, :] * _rms_norm(y, npost_ref[...])


def _mixer(x, ada3, q, kt, v, so, sga, sgm, ya, gcol, grow, mh_norm_w, norm_post, wa, wm, wo):
    bsz, s, d = x.shape
    ts = min(TS_MIXER, s)
    chunk = min(MLSTM_CHUNK, ts)
    ng = 2 * M_HEADS
    nqk = M_HEADS * M_DK
    sel = (jnp.arange(LANES)[:, None] - M_HEADS == jnp.arange(M_HEADS * M_DV)[None, :] // M_DV).astype(BF16)
    tok_spec = lambda w=d: pl.BlockSpec((1, ts, w), lambda b, i: (b, i, 0))
    words_spec = lambda w=d: pl.BlockSpec((1, ts // 2, w), lambda b, i: (b, i, 0))
    return pl.pallas_call(
        functools.partial(_mixer_kernel, ts=ts, chunk=chunk),
        grid=(bsz, s // ts),
        in_specs=[tok_spec(),
                  pl.BlockSpec((1, 6, d), lambda b, i: (b, 0, 0)),
                  words_spec(nqk),
                  pl.BlockSpec((1, nqk // 2, ts), lambda b, i: (b, 0, i)),
                  words_spec(), tok_spec(), tok_spec(), tok_spec(), words_spec(),
                  tok_spec(LANES),
                  pl.BlockSpec((1, ng, ts), lambda b, i: (b, 0, i)),
                  _resident(sel.shape),
                  _resident((1, d)), _resident((1, d)),
                  _resident(wa.shape), _resident(wm.shape), _resident(wo.shape)],
        out_specs=tok_spec(),
        out_shape=jax.ShapeDtypeStruct((bsz, s, d), F32),
        scratch_shapes=[pltpu.VMEM((M_HEADS, M_DK, 2 * M_DV), F32),
                        pltpu.VMEM((8, LANES), F32),
                        pltpu.VMEM((M_HEADS, LANES), F32),
                        pltpu.VMEM((ts, d), BF16)],
        compiler_params=pltpu.CompilerParams(dimension_semantics=("arbitrary", "arbitrary"),
                                             vmem_limit_bytes=VMEM_LIMIT_BYTES),
        name="mixer",
    )(x, ada3, q, kt, v, so, sga, sgm, ya, gcol, grow, sel, mh_norm_w, norm_post, wa, wm, wo)


def _argmax_rows(s, exact):
    m = jnp.max(s, axis=0, keepdims=True)
    hit = s == m
    if exact:
        row = lax.broadcasted_iota(jnp.int32, s.shape, 0)
        hit = row == jnp.min(jnp.where(hit, row, s.shape[0]), axis=0, keepdims=True)
    return hit, m


def _topk_rows(arrays, k, exact):
    arrays = list(arrays)
    ranks = [jnp.full(s.shape, NOT_RANKED, F32) for s in arrays]
    vals = [[] for _ in arrays]
    for r in range(k):
        for i, s in enumerate(arrays):
            hit, m = _argmax_rows(s, exact)
            ranks[i] = jnp.where(hit, float(r), ranks[i])
            arrays[i] = jnp.where(hit, -jnp.inf, s)
            vals[i].append(m)
    return vals, ranks


def _route_tables(s1, s2, exact):
    (v1, v2), (rank1, rank2) = _topk_rows((s1, s2), P_TOPK, exact)
    v2_lo = jnp.concatenate(v2[0:8], axis=0)
    v2_all = jnp.concatenate(v2, axis=0)
    v1_hi = jnp.concatenate(v1[8:16], axis=0)
    cand = jnp.concatenate([v1[0] + v2_all] + [v1[j] + v2_lo for j in range(1, 8)] + [v1_hi + v2[0]], axis=0)
    sel = jnp.zeros(cand.shape, F32)
    cur = cand
    for _ in range(P_TOPK):
        hit, _ = _argmax_rows(cur, exact)
        sel = jnp.where(hit, 1.0, sel)
        cur = jnp.where(hit, -jnp.inf, cur)
    ranked = lambda rk: jnp.sum(jnp.where(rk < NOT_RANKED, 1.0, 0.0), axis=0, keepdims=True)
    picked = jnp.maximum(jnp.maximum(ranked(rank1), ranked(rank2)), jnp.sum(sel, axis=0, keepdims=True))
    tied = jnp.where(picked > float(P_TOPK), 1.0, 0.0)
    top = v1[0] + v2[0]
    z = jnp.sum(sel * jnp.exp(cand - top), axis=0, keepdims=True)
    counts = [jnp.sum(sel[0:16], axis=0, keepdims=True)]
    counts += [jnp.sum(sel[8 + 8 * j:16 + 8 * j], axis=0, keepdims=True) for j in range(1, 8)]
    counts += [sel[72 + j:73 + j] for j in range(8)]
    cnt = jnp.zeros(s1.shape, F32)
    for j in range(P_TOPK):
        cnt = jnp.where(rank1 == float(j), counts[j], cnt)
    p1 = jnp.exp(s1 - v1[0])
    p2 = jnp.exp(s2 - v2[0]) * (1.0 / z)
    return cnt, rank2, p1, p2, tied


def _bf16_pair_words(x):
    hi = pltpu.bitcast(x.astype(BF16).astype(F32), jnp.uint32)
    return hi | lax.shift_right_logical(hi, jnp.uint32(16))


def _route_kernel(x_ref, ada_ref, npre_ref, wqt_ref, sk_ref, h2t_ref, cnt_ref, rk2_ref, p1_ref, p2_ref, qt_buf,
                  *, tm):
    ada = ada_ref[0]
    h2 = _rms_norm(x_ref[...], npre_ref[...]) * (1.0 + ada[4:5, :]) + ada[3:4, :]
    h2t = h2.T.astype(BF16)
    h2t_ref[...] = _packed_words(h2t)
    qt_buf[...] = jnp.dot(wqt_ref[...], h2t, preferred_element_type=F32).astype(BF16)

    def head(hd, carry):
        r0 = pl.multiple_of(hd * (2 * N_KEYS), 2 * N_KEYS)
        s1 = jnp.dot(sk_ref[hd, 0], qt_buf[pl.ds(r0, N_KEYS), :], preferred_element_type=F32)
        s2 = jnp.dot(sk_ref[hd, 1], qt_buf[pl.ds(r0 + N_KEYS, N_KEYS), :], preferred_element_type=F32)

        def tables(lc, exact):
            sl = slice(lc * LANES, (lc + 1) * LANES)
            cnt, rk2, p1, p2, tied = _route_tables(s1[:, sl], s2[:, sl], exact)
            cnt_ref[hd, lc] = _bf16_pair_words(cnt)
            rk2_ref[hd, lc] = _packed_words(rk2.astype(BF16))
            p1_ref[hd, lc] = _bf16_pair_words(p1)
            p2_ref[hd, lc] = _packed_words(p2.astype(BF16))
            return tied

        for lc in range(tm // LANES):
            tied = tables(lc, exact=False)

            @pl.when(jnp.max(tied) > 0.0)
            def _():
                tables(lc, exact=True)
        return carry

    lax.fori_loop(0, P_HEADS, head, 0)


def _route(x1, ada3, norm_pre, wqt, subkeys):
    t, d = x1.shape
    s = t // ada3.shape[0]
    tm = min(TM_ROUTE, s)
    tab = lambda rows: jax.ShapeDtypeStruct((P_HEADS, t // LANES, rows, LANES), jnp.uint32)
    tab_spec = lambda rows: pl.BlockSpec((P_HEADS, tm // LANES, rows, LANES), lambda i: (0, i, 0, 0))
    k1, k2 = N_KEYS, N_KEYS // 2
    return pl.pallas_call(
        functools.partial(_route_kernel, tm=tm),
        grid=(t // tm,),
        in_specs=[pl.BlockSpec((tm, d), lambda i: (i, 0)),
                  pl.BlockSpec((1, 6, d), lambda i: ((i * tm) // s, 0, 0)),
                  _resident((1, d)),
                  _resident(wqt.shape),
                  _resident(subkeys.shape)],
        out_specs=[pl.BlockSpec((d // 2, tm), lambda i: (0, i)),
                   tab_spec(k1), tab_spec(k2), tab_spec(k1), tab_spec(k2)],
        out_shape=[jax.ShapeDtypeStruct((d // 2, t), jnp.uint32), tab(k1), tab(k2), tab(k1), tab(k2)],
        scratch_shapes=[pltpu.VMEM((wqt.shape[0], tm), BF16)],
        compiler_params=pltpu.CompilerParams(dimension_semantics=("arbitrary",),
                                             vmem_limit_bytes=VMEM_LIMIT_BYTES),
        name="route",
    )(x1, ada3, norm_pre, wqt, subkeys)


def _gelu_tanh(z):
    return 0.5 * z * (1.0 + jnp.tanh(0.7978845608028654 * (z + 0.044715 * (z * z * z))))


def _peer_kernel(h2t_ref, u_ref, vt_ref, cnt_ref, rk2_ref, p1_ref, p2_ref, x_ref, ada_ref, npost_ref, o_ref,
                 acc, g_buf, z_buf, a_buf, *, tm, te):
    j = pl.program_id(1)

    @pl.when(j == 0)
    def _():
        acc[...] = jnp.zeros(acc.shape, F32)

    apc = MXU_K // N_KEYS
    nkc = te // MXU_K
    nlc = tm // LANES
    nlh = tm // MXU_N
    d = acc.shape[0]
    rh = d // 2

    def gate_block(kc, ac, lc):
        al = kc * apc + ac
        pk = (N_KEYS // BF16_ROWS, BF16_ROWS, LANES)
        gate = None
        for hd in range(P_HEADS):
            cnt_a = pltpu.bitcast(jnp.broadcast_to(cnt_ref[hd, lc, al:al + 1, :], (8, LANES)), BF16)
            p1_a = pltpu.bitcast(jnp.broadcast_to(p1_ref[hd, lc, al:al + 1, :], (8, LANES)), BF16)
            hit = _unpacked(rk2_ref[hd, lc]).reshape(pk) < cnt_a[None]
            term = jnp.where(hit, _unpacked(p2_ref[hd, lc]).reshape(pk) * p1_a[None], jnp.zeros((), BF16))
            gate = term if gate is None else gate + term
        g_buf[kc % 2, ac, lc] = gate.reshape(N_KEYS, LANES)

    def score_piece(kc, ac, lh):
        r0 = kc * MXU_K + ac * N_KEYS
        z_buf[kc % 2, ac * N_KEYS:(ac + 1) * N_KEYS, lh * MXU_N:(lh + 1) * MXU_N] = jnp.dot(
            _unpacked(u_ref[r0 // 2:(r0 + N_KEYS) // 2, :]),
            _unpacked(h2t_ref[:, lh * MXU_N:(lh + 1) * MXU_N]), preferred_element_type=F32)

    def act_block(kc, ac, lc):
        r0 = kc * MXU_K + ac * N_KEYS
        zb = z_buf[kc % 2, ac * N_KEYS:(ac + 1) * N_KEYS, lc * LANES:(lc + 1) * LANES].astype(BF16)
        a_buf[r0:r0 + N_KEYS, lc * LANES:(lc + 1) * LANES] = _gelu_tanh(zb) * g_buf[kc % 2, ac, lc]

    outs = {}

    def out_piece(kc, r, lh):
        e0 = kc * MXU_K
        part = jnp.dot(_unpacked(vt_ref[r * rh // 2:(r + 1) * rh // 2, e0:e0 + MXU_K]),
                       a_buf[e0:e0 + MXU_K, lh * MXU_N:(lh + 1) * MXU_N], preferred_element_type=F32)
        outs[(r, lh)] = part if (r, lh) not in outs else outs[(r, lh)] + part

    blocks = [(ac, lc) for lc in range(nlc) for ac in range(apc)]
    pieces = [(ac, lh) for lh in range(nlh) for ac in range(apc)]
    for s, blk in enumerate(blocks):
        gate_block(0, *blk)
        if s % 2 == 1:
            score_piece(0, *pieces[s // 2])
    pending = []
    for kc in range(nkc):
        mxu = list(pending)
        pending = []
        if kc + 1 < nkc:
            mxu += [functools.partial(score_piece, kc + 1, ac, lh) for ac, lh in pieces]
        for s, (ac, lc) in enumerate(blocks):
            if kc + 1 < nkc:
                gate_block(kc + 1, ac, lc)
            act_block(kc, ac, lc)
            if mxu:
                mxu.pop(0)()
            if ac == apc - 1 and (lc + 1) % (MXU_N // LANES) == 0:
                lh = lc // (MXU_N // LANES)
                ready = [functools.partial(out_piece, kc, r, lh) for r in range(d // rh)]
                if lh + 1 < nlh:
                    mxu += ready
                else:
                    pending = ready
        for f in mxu:
            f()
    for f in pending:
        f()
    for (r, lh), val in outs.items():
        acc[r * rh:(r + 1) * rh, lh * MXU_N:(lh + 1) * MXU_N] += val

    @pl.when(j == pl.num_programs(1) - 1)
    def _():
        y = acc[...].T
        o_ref[...] = x_ref[...] + ada_ref[0][5:6, :] * _rms_norm(y, npost_ref[...])


def _peer(h2t, u_bf, vt_bf, cnt, rk2, p1, p2, x1, ada3, norm_post):
    t, d = x1.shape
    s = t // ada3.shape[0]
    ne = 2 * u_bf.shape[0]
    tm = min(TM_PEER, s)
    te = TE_PEER
    tab2_spec = pl.BlockSpec((P_HEADS, tm // LANES, N_KEYS // 2, LANES), lambda i, j: (0, i, 0, 0))
    tab1_spec = pl.BlockSpec((P_HEADS, tm // LANES, te // N_KEYS, LANES), lambda i, j: (0, i, j, 0))
    return pl.pallas_call(
        functools.partial(_peer_kernel, tm=tm, te=te),
        grid=(t // tm, ne // te),
        in_specs=[pl.BlockSpec((d // 2, tm), lambda i, j: (0, i)),
                  pl.BlockSpec((te // 2, d), lambda i, j: (j, 0)),
                  pl.BlockSpec((d // 2, te), lambda i, j: (0, j)),
                  tab1_spec, tab2_spec, tab1_spec, tab2_spec,
                  pl.BlockSpec((tm, d), lambda i, j: (i, 0)),
                  pl.BlockSpec((1, 6, d), lambda i, j: ((i * tm) // s, 0, 0)),
                  pl.BlockSpec((1, d), lambda i, j: (0, 0))],
        out_specs=pl.BlockSpec((tm, d), lambda i, j: (i, 0)),
        out_shape=jax.ShapeDtypeStruct((t, d), F32),
        scratch_shapes=[pltpu.VMEM((d, tm), F32),
                        pltpu.VMEM((2, MXU_K // N_KEYS, tm // LANES, N_KEYS, LANES), BF16),
                        pltpu.VMEM((2, MXU_K, tm), F32),
                        pltpu.VMEM((te, tm), BF16)],
        compiler_params=pltpu.CompilerParams(dimension_semantics=("arbitrary", "arbitrary"),
                                             vmem_limit_bytes=VMEM_LIMIT_BYTES),
        name="peer",
    )(h2t, u_bf, vt_bf, cnt, rk2, p1, p2, x1, ada3, norm_post)


def _layer(x, c, w_ada, b_ada, norm1_pre, norm1_post, w_in, conv_a_w, conv_qk_w, b_igate, b_fgate, mh_norm_w,
           w_branch_a, w_branch_m, w_out, norm2_pre, norm2_post, peer_wq, peer_subkeys, peer_u, peer_v):
    bsz, s, d = x.shape
    ng = 2 * M_HEADS
    ada3 = _ada(c, w_ada, b_ada).reshape(bsz, 6, d)

    o_if = 3 * d + 2 * M_HEADS * M_DK + 2 * d
    w_main = jnp.concatenate([w_in[:, :o_if], w_in[:, o_if + ng:]], axis=1).astype(BF16)
    w_if = w_in[:, o_if:o_if + ng]
    wif_col = jnp.pad(w_if, ((0, 0), (0, LANES - ng))).astype(BF16)
    wif_row = w_if.T.astype(BF16)
    b_if = jnp.concatenate([b_igate, b_fgate])
    bif_col = jnp.pad(b_if, (0, LANES - ng)).reshape(1, LANES)
    bif_row = b_if.reshape(ng, 1)

    ya, q, kt, v, so, sga, sgm, gcol, grow = _inproj(
        x, ada3, norm1_pre.reshape(1, d), w_main, wif_col, wif_row, bif_col, bif_row, conv_a_w, conv_qk_w)
    x1 = _mixer(x, ada3, q, kt, v, so, sga, sgm, ya, gcol, grow, mh_norm_w.reshape(1, d),
                norm1_post.reshape(1, d), w_branch_a.astype(BF16), w_branch_m.astype(BF16), w_out.astype(BF16))

    x1f = x1.reshape(bsz * s, d)
    h2t, cnt, rk2, p1, p2 = _route(x1f, ada3, norm2_pre.reshape(1, d), peer_wq.T.astype(BF16),
                                   peer_subkeys.astype(BF16))
    out = _peer(h2t, _packed_words_xla(peer_u.astype(BF16)), _packed_words_xla(peer_v.T.astype(BF16)),
                cnt, rk2, p1, p2, x1f, ada3, norm2_post.reshape(1, d))
    return out.reshape(bsz, s, d)


def kernel(x, c, w_ada, b_ada, norm1_pre, norm1_post, w_in, conv_a_w, conv_qk_w, b_igate, b_fgate, mh_norm_w, w_branch_a, w_branch_m, w_out, norm2_pre, norm2_post, peer_wq, peer_subkeys, peer_u, peer_v):
    for l in range(w_ada.shape[0]):
        x = _layer(x, c, w_ada[l], b_ada[l], norm1_pre[l], norm1_post[l], w_in[l], conv_a_w[l], conv_qk_w[l],
                   b_igate[l], b_fgate[l], mh_norm_w[l], w_branch_a[l], w_branch_m[l], w_out[l], norm2_pre[l],
                   norm2_post[l], peer_wq[l], peer_subkeys[l], peer_u[l], peer_v[l])
    return x
```

```python
import functools

import jax
import jax.numpy as jnp
from jax import lax
from jax.experimental import pallas as pl
from jax.experimental.pallas import tpu as pltpu

F32 = jnp.float32
BF16 = jnp.bfloat16

EPS = 1e-6
M_HEADS = 8
M_DK = 64
M_DV = 128
CONV_K = 3
QK_CONV_K = 4
P_HEADS = 8
N_KEYS = 128
P_TOPK = 16
HALO = 8
LANES = 128
MXU_K = 256
MXU_N = 256
BF16_ROWS = 16
NOT_RANKED = 99.0

VMEM_LIMIT_BYTES = 56 * 1024 * 1024

MLSTM_CHUNK = 128
TS_INPROJ = 256
TS_MIXER = 256
TM_ROUTE = 256
TM_PEER = 1024
TE_PEER = 1024


def _resident(shape):
    nd = len(shape)
    return pl.BlockSpec(shape, lambda *_: (0,) * nd, pipeline_mode=pl.Buffered(1))


def _rms_norm(x, w):
    return x * lax.rsqrt(jnp.mean(x * x, axis=-1, keepdims=True) + EPS) * w


def _sigmoid(x):
    return 1.0 / (1.0 + jnp.exp(-x))


def _packed_words(x):
    return pltpu.bitcast(x, jnp.uint32)


def _unpacked(words):
    return pltpu.bitcast(words, BF16)


def _ada_kernel(c_ref, w_ref, b_ref, o_ref):
    c = c_ref[...]
    s = c * _sigmoid(c)
    o_ref[...] = jnp.dot(s.astype(BF16), w_ref[...].astype(BF16), preferred_element_type=F32) + b_ref[...]


def _ada(c, w_ada, b_ada):
    bsz, d = c.shape
    n = w_ada.shape[1]
    return pl.pallas_call(
        _ada_kernel,
        grid=(n // d,),
        in_specs=[pl.BlockSpec((bsz, d), lambda j: (0, 0)),
                  pl.BlockSpec((d, d), lambda j: (0, j)),
                  pl.BlockSpec((1, d), lambda j: (0, j))],
        out_specs=pl.BlockSpec((bsz, d), lambda j: (0, j)),
        out_shape=jax.ShapeDtypeStruct((bsz, n), F32),
        compiler_params=pltpu.CompilerParams(vmem_limit_bytes=VMEM_LIMIT_BYTES),
        name="ada",
    )(c, w_ada, b_ada.reshape(1, n))


def _inproj_kernel(x_ref, ada_ref, npre_ref, w_ref, wifc_ref, wifr_ref, bifc_ref, bifr_ref, cwa_ref, cwqk_ref,
                   ya_ref, q_ref, kt_ref, v_ref, so_ref, sga_ref, sgm_ref, gcol_ref, grow_ref,
                   ubuf, qkbuf, *, ts, d):
    @pl.when(pl.program_id(1) == 0)
    def _():
        ubuf[0:HALO, :] = jnp.zeros((HALO, d), F32)
        qkbuf[0:HALO, :] = jnp.zeros((HALO, d), F32)

    x = x_ref[0]
    ada = ada_ref[0]
    h = _rms_norm(x, npre_ref[...]) * (1.0 + ada[1:2, :]) + ada[0:1, :]
    hb = h.astype(BF16)

    def proj(k):
        return jnp.dot(hb, w_ref[:, k * d:(k + 1) * d], preferred_element_type=F32)

    u = proj(0) * proj(2)
    ubuf[HALO:HALO + ts, :] = u
    conv = u * cwa_ref[CONV_K - 1:CONV_K, :]
    for j in range(CONV_K - 1):
        off = HALO - (CONV_K - 1) + j
        conv = conv + ubuf[off:off + ts, :] * cwa_ref[j:j + 1, :]
    ya_ref[0] = _packed_words((proj(1) * conv).astype(BF16))
    ubuf[0:HALO, :] = ubuf[ts:ts + HALO, :]

    qk = proj(3)
    qkbuf[HALO:HALO + ts, :] = qk
    cq = qk * cwqk_ref[QK_CONV_K - 1:QK_CONV_K, :]
    for j in range(QK_CONV_K - 1):
        off = HALO - (QK_CONV_K - 1) + j
        cq = cq + qkbuf[off:off + ts, :] * cwqk_ref[j:j + 1, :]
    qkbuf[0:HALO, :] = qkbuf[ts:ts + HALO, :]
    cq = cq * _sigmoid(cq)
    nqk = M_HEADS * M_DK
    q_ref[0] = _packed_words((cq[:, :nqk] * (M_DK ** -0.5)).astype(BF16))
    kt_ref[0] = _packed_words(cq[:, nqk:].T.astype(BF16))

    v_ref[0] = _packed_words(proj(4).astype(BF16))
    so_ref[0] = _sigmoid(proj(5)).astype(BF16)
    sga_ref[0] = _sigmoid(proj(6)).astype(BF16)
    sgm_ref[0] = _sigmoid(proj(7)).astype(BF16)

    gcol_ref[0] = jnp.dot(hb, wifc_ref[...], preferred_element_type=F32) + bifc_ref[...]
    grow_ref[0] = lax.dot_general(wifr_ref[...], hb, (((1,), (1,)), ((), ())),
                                  preferred_element_type=F32) + bifr_ref[...]


def _inproj(x, ada3, norm_pre, w_main, wif_col, wif_row, bif_col, bif_row, conv_a_w, conv_qk_w):
    bsz, s, d = x.shape
    ts = min(TS_INPROJ, s)
    ng = 2 * M_HEADS
    nqk = M_HEADS * M_DK
    tok = lambda dt, w=d: jax.ShapeDtypeStruct((bsz, s, w), dt)
    tok_spec = lambda w=d: pl.BlockSpec((1, ts, w), lambda b, i: (b, i, 0))
    words = lambda w=d: jax.ShapeDtypeStruct((bsz, s // 2, w), jnp.uint32)
    words_spec = lambda w=d: pl.BlockSpec((1, ts // 2, w), lambda b, i: (b, i, 0))
    return pl.pallas_call(
        functools.partial(_inproj_kernel, ts=ts, d=d),
        grid=(bsz, s // ts),
        in_specs=[tok_spec(),
                  pl.BlockSpec((1, 6, d), lambda b, i: (b, 0, 0)),
                  _resident((1, d)),
                  _resident(w_main.shape),
                  _resident(wif_col.shape),
                  _resident(wif_row.shape),
                  _resident(bif_col.shape),
                  _resident(bif_row.shape),
                  _resident(conv_a_w.shape),
                  _resident(conv_qk_w.shape)],
        out_specs=[words_spec(), words_spec(nqk),
                   pl.BlockSpec((1, nqk // 2, ts), lambda b, i: (b, 0, i)),
                   words_spec(), tok_spec(), tok_spec(), tok_spec(),
                   tok_spec(LANES),
                   pl.BlockSpec((1, ng, ts), lambda b, i: (b, 0, i))],
        out_shape=[words(), words(nqk),
                   jax.ShapeDtypeStruct((bsz, nqk // 2, s), jnp.uint32),
                   words(), tok(BF16), tok(BF16), tok(BF16),
                   tok(F32, LANES),
                   jax.ShapeDtypeStruct((bsz, ng, s), F32)],
        scratch_shapes=[pltpu.VMEM((HALO + ts, d), F32), pltpu.VMEM((HALO + ts, d), F32)],
        compiler_params=pltpu.CompilerParams(dimension_semantics=("arbitrary", "arbitrary"),
                                             vmem_limit_bytes=VMEM_LIMIT_BYTES),
        name="inproj",
    )(x, ada3, norm_pre, w_main, wif_col, wif_row, bif_col, bif_row, conv_a_w, conv_qk_w)


def _log_sigmoid(x):
    return jnp.minimum(x, 0.0) - jnp.log(1.0 + jnp.exp(-jnp.abs(x)))


def _chunk_scan(x, axis, chunk, op, identity):
    pos = lax.broadcasted_iota(jnp.int32, x.shape, axis) % chunk
    k = 1
    while k < chunk:
        x = op(x, jnp.where(pos >= k, pltpu.roll(x, k, axis), identity))
        k *= 2
    return x


def _bf16_terms(x):
    hi = x.astype(BF16)
    r = x - hi.astype(F32)
    mid = r.astype(BF16)
    lo = (r - mid.astype(F32)).astype(BF16)
    return hi, mid, lo


def _spread_heads(x, sel):
    return sum(jnp.dot(t, sel, preferred_element_type=F32) for t in _bf16_terms(x))


def _mixer_kernel(x_ref, ada_ref, q_ref, kt_ref, v_ref, so_ref, sga_ref, sgm_ref, ya_ref, gcol_ref, grow_ref,
                  sel_ref, mhw_ref, npost_ref, wa_ref, wm_ref, wo_ref, o_ref,
                  c_st, mcol_st, mrow_st, ym_buf, *, ts, chunk):
    first = pl.program_id(1) == 0

    @pl.when(first)
    def _():
        c_st[...] = jnp.zeros(c_st.shape, F32)
        mcol_st[...] = jnp.zeros(mcol_st.shape, F32)
        mrow_st[...] = jnp.zeros(mrow_st.shape, F32)

    nc = ts // chunk
    H = M_HEADS
    neg_inf = -jnp.inf

    gcol = gcol_ref[0]
    b_col = _chunk_scan(_log_sigmoid(gcol), 0, chunk, jnp.add, 0.0)
    r_col = pltpu.roll(gcol, H, 1) - b_col
    cm_col = _chunk_scan(r_col, 0, chunk, jnp.maximum, neg_inf)
    m = mcol_st[0:1, :]
    m_rows, decay_rows = [], []
    for c in range(nc):
        last = (c + 1) * chunk - 1
        b_l, cm_l = b_col[last:last + 1, :], cm_col[last:last + 1, :]
        m_next = b_l + jnp.maximum(m, cm_l)
        m_rows.append(m)
        decay_rows.append(jnp.exp(b_l + m - m_next))
        m = m_next
    mcol_st[...] = jnp.broadcast_to(m, mcol_st.shape)
    pad = jnp.zeros((8 - (2 * nc) % 8, LANES), F32)
    spread = _spread_heads(jnp.concatenate([b_col, cm_col] + m_rows + decay_rows + [pad], axis=0), sel_ref[...])
    bc_all, cm_all, sc_all = spread[0:ts], spread[ts:2 * ts], spread[2 * ts:]

    grow = grow_ref[0]
    i_row = grow[0:H, :]
    b_row = _chunk_scan(_log_sigmoid(grow[H:2 * H, :]), 1, chunk, jnp.add, 0.0)
    r_row = i_row - b_row
    cm_row = _chunk_scan(r_row, 1, chunk, jnp.maximum, neg_inf)
    mr = mrow_st[:, 0:1]
    ws_rows = []
    for c in range(nc):
        last = (c + 1) * chunk - 1
        b_l, cm_l = b_row[:, last:last + 1], cm_row[:, last:last + 1]
        m_next = b_l + jnp.maximum(mr, cm_l)
        ws_rows.append(jnp.exp(b_l + r_row[:, c * chunk:(c + 1) * chunk] - m_next))
        mr = m_next
    mrow_st[...] = jnp.broadcast_to(mr, mrow_st.shape)

    tril = (lax.broadcasted_iota(jnp.int32, (chunk, chunk), 0)
            >= lax.broadcasted_iota(jnp.int32, (chunk, chunk), 1))
    ones = jnp.ones((chunk, M_DV), BF16)

    for hd in range(H):
        hl = slice(hd * M_DV, (hd + 1) * M_DV)
        c_aug = c_st[hd]
        for c in range(nc):
            rows = slice(c * chunk, (c + 1) * chunk)
            wrows = slice(c * chunk // 2, (c + 1) * chunk // 2)
            q = _unpacked(q_ref[0, wrows, hd * M_DK:(hd + 1) * M_DK])
            kt = _unpacked(kt_ref[0, hd * M_DK // 2:(hd + 1) * M_DK // 2, rows])
            v = _unpacked(v_ref[0, wrows, hl])
            bc = bc_all[rows, hl]
            m_prev = sc_all[c:c + 1, hl]
            decay = sc_all[nc + c:nc + c + 1, hl]
            m_t = bc + jnp.maximum(m_prev, cm_all[rows, hl])
            dlog = jnp.where(tril, bc[:, :chunk] + r_row[hd:hd + 1, rows], neg_inf)
            s_qk = jnp.dot(q, kt, preferred_element_type=F32) * jnp.exp(dlog - m_t[:, :chunk])
            a_inter = jnp.exp(bc + m_prev - m_t)
            qc = jnp.dot(q, c_aug.astype(BF16), preferred_element_type=F32)
            num = jnp.dot(s_qk.astype(BF16), v, preferred_element_type=F32) + a_inter * qc[:, :M_DV]
            den = jnp.sum(s_qk, axis=1, keepdims=True) + a_inter * qc[:, M_DV:]
            h = num / jnp.maximum(jnp.abs(den), jnp.exp(-m_t))
            hn = _rms_norm(h, mhw_ref[:, hl])
            ym_buf[rows, hl] = (so_ref[0, rows, hl].astype(F32) * hn).astype(BF16)
            ktw = (kt.astype(F32) * ws_rows[c][hd:hd + 1, :]).astype(BF16)
            upd = jnp.dot(ktw, jnp.concatenate([v, ones], axis=1), preferred_element_type=F32)
            c_aug = jnp.concatenate([decay, decay], axis=1) * c_aug + upd
        c_st[hd] = c_aug

    mix = (sga_ref[0].astype(F32) * jnp.dot(_unpacked(ya_ref[0]), wa_ref[...], preferred_element_type=F32)
           + sgm_ref[0].astype(F32) * jnp.dot(ym_buf[...], wm_ref[...], preferred_element_type=F32))
    y = jnp.dot(mix.astype(BF16), wo_ref[...], preferred_element_type=F32)
    o_ref[0] = x_ref[0] + ada_ref[0][2:3, :] * _rms_norm(y, npost_ref[...])


def _mixer(x, ada3, q, kt, v, so, sga, sgm, ya, gcol, grow, mh_norm_w, norm_post, wa, wm, wo):
    bsz, s, d = x.shape
    ts = min(TS_MIXER, s)
    chunk = min(MLSTM_CHUNK, ts)
    ng = 2 * M_HEADS
    nqk = M_HEADS * M_DK
    sel = (jnp.arange(LANES)[:, None] - M_HEADS == jnp.arange(M_HEADS * M_DV)[None, :] // M_DV).astype(BF16)
    tok_spec = lambda w=d: pl.BlockSpec((1, ts, w), lambda b, i: (b, i, 0))
    words_spec = lambda w=d: pl.BlockSpec((1, ts // 2, w), lambda b, i: (b, i, 0))
    return pl.pallas_call(
        functools.partial(_mixer_kernel, ts=ts, chunk=chunk),
        grid=(bsz, s // ts),
        in_specs=[tok_spec(),
                  pl.BlockSpec((1, 6, d), lambda b, i: (b, 0, 0)),
                  words_spec(nqk),
                  pl.BlockSpec((1, nqk // 2, ts), lambda b, i: (b, 0, i)),
                  words_spec(), tok_spec(), tok_spec(), tok_spec(), words_spec(),
                  tok_spec(LANES),
                  pl.BlockSpec((1, ng, ts), lambda b, i: (b, 0, i)),
                  _resident(sel.shape),
                  _resident((1, d)), _resident((1, d)),
                  _resident(wa.shape), _resident(wm.shape), _resident(wo.shape)],
        out_specs=tok_spec(),
        out_shape=jax.ShapeDtypeStruct((bsz, s, d), F32),
        scratch_shapes=[pltpu.VMEM((M_HEADS, M_DK, 2 * M_DV), F32),
                        pltpu.VMEM((8, LANES), F32),
                        pltpu.VMEM((M_HEADS, LANES), F32),
                        pltpu.VMEM((ts, d), BF16)],
        compiler_params=pltpu.CompilerParams(dimension_semantics=("arbitrary", "arbitrary"),
                                             vmem_limit_bytes=VMEM_LIMIT_BYTES),
        name="mixer",
    )(x, ada3, q, kt, v, so, sga, sgm, ya, gcol, grow, sel, mh_norm_w, norm_post, wa, wm, wo)


def _argmax_rows(s, exact):
    m = jnp.max(s, axis=0, keepdims=True)
    hit = s == m
    if exact:
        row = lax.broadcasted_iota(jnp.int32, s.shape, 0)
        hit = row == jnp.min(jnp.where(hit, row, s.shape[0]), axis=0, keepdims=True)
    return hit, m


def _topk_rows(arrays, k, exact):
    arrays = list(arrays)
    ranks = [jnp.full(s.shape, NOT_RANKED, F32) for s in arrays]
    vals = [[] for _ in arrays]
    for r in range(k):
        for i, s in enumerate(arrays):
            hit, m = _argmax_rows(s, exact)
            ranks[i] = jnp.where(hit, float(r), ranks[i])
            arrays[i] = jnp.where(hit, -jnp.inf, s)
            vals[i].append(m)
    return vals, ranks


def _route_tables(s1, s2, exact):
    (v1, v2), (rank1, rank2) = _topk_rows((s1, s2), P_TOPK, exact)
    v2_lo = jnp.concatenate(v2[0:8], axis=0)
    v2_all = jnp.concatenate(v2, axis=0)
    v1_hi = jnp.concatenate(v1[8:16], axis=0)
    cand = jnp.concatenate([v1[0] + v2_all] + [v1[j] + v2_lo for j in range(1, 8)] + [v1_hi + v2[0]], axis=0)
    sel = jnp.zeros(cand.shape, F32)
    cur = cand
    for _ in range(P_TOPK):
        hit, _ = _argmax_rows(cur, exact)
        sel = jnp.where(hit, 1.0, sel)
        cur = jnp.where(hit, -jnp.inf, cur)
    ranked = lambda rk: jnp.sum(jnp.where(rk < NOT_RANKED, 1.0, 0.0), axis=0, keepdims=True)
    picked = jnp.maximum(jnp.maximum(ranked(rank1), ranked(rank2)), jnp.sum(sel, axis=0, keepdims=True))
    tied = jnp.where(picked > float(P_TOPK), 1.0, 0.0)
    top = v1[0] + v2[0]
    z = jnp.sum(sel * jnp.exp(cand - top), axis=0, keepdims=True)
    counts = [jnp.sum(sel[0:16], axis=0, keepdims=True)]
    counts += [jnp.sum(sel[8 + 8 * j:16 + 8 * j], axis=0, keepdims=True) for j in range(1, 8)]
    counts += [sel[72 + j:73 + j] for j in range(8)]
    cnt = jnp.zeros(s1.shape, F32)
    for j in range(P_TOPK):
        cnt = jnp.where(rank1 == float(j), counts[j], cnt)
    p1 = jnp.exp(s1 - v1[0])
    p2 = jnp.exp(s2 - v2[0]) * (1.0 / z)
    return cnt, rank2, p1, p2, tied


def _bf16_pair_words(x):
    hi = pltpu.bitcast(x.astype(BF16).astype(F32), jnp.uint32)
    return hi | lax.shift_right_logical(hi, jnp.uint32(16))


def _route_kernel(x_ref, ada_ref, npre_ref, wqt_ref, sk_ref, h2t_ref, cnt_ref, rk2_ref, p1_ref, p2_ref, qt_buf,
                  *, tm):
    ada = ada_ref[0]
    h2 = _rms_norm(x_ref[...], npre_ref[...]) * (1.0 + ada[4:5, :]) + ada[3:4, :]
    h2t = h2.T.astype(BF16)
    h2t_ref[...] = _packed_words(h2t)
    qt_buf[...] = jnp.dot(wqt_ref[...], h2t, preferred_element_type=F32).astype(BF16)

    def head(hd, carry):
        r0 = pl.multiple_of(hd * (2 * N_KEYS), 2 * N_KEYS)
        s1 = jnp.dot(sk_ref[hd, 0], qt_buf[pl.ds(r0, N_KEYS), :], preferred_element_type=F32)
        s2 = jnp.dot(sk_ref[hd, 1], qt_buf[pl.ds(r0 + N_KEYS, N_KEYS), :], preferred_element_type=F32)

        def tables(lc, exact):
            sl = slice(lc * LANES, (lc + 1) * LANES)
            cnt, rk2, p1, p2, tied = _route_tables(s1[:, sl], s2[:, sl], exact)
            cnt_ref[hd, lc] = _bf16_pair_words(cnt)
            rk2_ref[hd, lc] = _packed_words(rk2.astype(BF16))
            p1_ref[hd, lc] = _bf16_pair_words(p1)
            p2_ref[hd, lc] = _packed_words(p2.astype(BF16))
            return tied

        for lc in range(tm // LANES):
            tied = tables(lc, exact=False)

            @pl.when(jnp.max(tied) > 0.0)
            def _():
                tables(lc, exact=True)
        return carry

    lax.fori_loop(0, P_HEADS, head, 0)


def _route(x1, ada3, norm_pre, wqt, subkeys):
    t, d = x1.shape
    s = t // ada3.shape[0]
    tm = min(TM_ROUTE, s)
    tab = lambda rows: jax.ShapeDtypeStruct((P_HEADS, t // LANES, rows, LANES), jnp.uint32)
    tab_spec = lambda rows: pl.BlockSpec((P_HEADS, tm // LANES, rows, LANES), lambda i: (0, i, 0, 0))
    k1, k2 = N_KEYS, N_KEYS // 2
    return pl.pallas_call(
        functools.partial(_route_kernel, tm=tm),
        grid=(t // tm,),
        in_specs=[pl.BlockSpec((tm, d), lambda i: (i, 0)),
                  pl.BlockSpec((1, 6, d), lambda i: ((i * tm) // s, 0, 0)),
                  _resident((1, d)),
                  _resident(wqt.shape),
                  _resident(subkeys.shape)],
        out_specs=[pl.BlockSpec((d // 2, tm), lambda i: (0, i)),
                   tab_spec(k1), tab_spec(k2), tab_spec(k1), tab_spec(k2)],
        out_shape=[jax.ShapeDtypeStruct((d // 2, t), jnp.uint32), tab(k1), tab(k2), tab(k1), tab(k2)],
        scratch_shapes=[pltpu.VMEM((wqt.shape[0], tm), BF16)],
        compiler_params=pltpu.CompilerParams(dimension_semantics=("arbitrary",),
                                             vmem_limit_bytes=VMEM_LIMIT_BYTES),
        name="route",
    )(x1, ada3, norm_pre, wqt, subkeys)


def _expert_words_kernel(u_ref, v_ref, uw_ref, vtw_ref):
    uw_ref[...] = _packed_words(u_ref[...].astype(BF16))
    vtw_ref[...] = _packed_words(v_ref[...].T.astype(BF16))


def _expert_words(u, v):
    ne, d = u.shape
    te = min(TE_PEER, ne)
    return pl.pallas_call(
        _expert_words_kernel,
        grid=(ne // te,),
        in_specs=[pl.BlockSpec((te, d), lambda j: (j, 0)), pl.BlockSpec((te, d), lambda j: (j, 0))],
        out_specs=[pl.BlockSpec((te // 2, d), lambda j: (j, 0)), pl.BlockSpec((d // 2, te), lambda j: (0, j))],
        out_shape=[jax.ShapeDtypeStruct((ne // 2, d), jnp.uint32), jax.ShapeDtypeStruct((d // 2, ne), jnp.uint32)],
        compiler_params=pltpu.CompilerParams(vmem_limit_bytes=VMEM_LIMIT_BYTES),
        name="expert_words",
    )(u, v)


def _gelu_tanh(z):
    return 0.5 * z * (1.0 + jnp.tanh(0.7978845608028654 * (z + 0.044715 * (z * z * z))))


def _peer_kernel(h2t_ref, u_ref, vt_ref, cnt_ref, rk2_ref, p1_ref, p2_ref, x_ref, ada_ref, npost_ref, o_ref,
                 acc, g_buf, z_buf, a_buf, *, tm, te):
    j = pl.program_id(1)

    @pl.when(j == 0)
    def _():
        acc[...] = jnp.zeros(acc.shape, F32)

    apc = MXU_K // N_KEYS
    nkc = te // MXU_K
    nlc = tm // LANES
    nlh = tm // MXU_N
    d = acc.shape[0]
    rh = d // 2

    def gate_block(kc, ac, lc):
        if ac:
            return
        pk = (N_KEYS // BF16_ROWS, BF16_ROWS, LANES)
        gates = [None] * apc
        for hd in range(P_HEADS):
            rk2 = _unpacked(rk2_ref[hd, lc]).reshape(pk)
            p2 = _unpacked(p2_ref[hd, lc]).reshape(pk)
            for a in range(apc):
                al = kc * apc + a
                cnt_a = pltpu.bitcast(jnp.broadcast_to(cnt_ref[hd, lc, al:al + 1, :], (8, LANES)), BF16)
                p1_a = pltpu.bitcast(jnp.broadcast_to(p1_ref[hd, lc, al:al + 1, :], (8, LANES)), BF16)
                term = jnp.where(rk2 < cnt_a[None], p2 * p1_a[None], jnp.zeros((), BF16))
                gates[a] = term if gates[a] is None else gates[a] + term
        for a in range(apc):
            g_buf[kc % 2, a, lc] = gates[a].reshape(N_KEYS, LANES)

    def score_piece(kc, ac, lh):
        r0 = kc * MXU_K + ac * N_KEYS
        z_buf[kc % 2, ac * N_KEYS:(ac + 1) * N_KEYS, lh * MXU_N:(lh + 1) * MXU_N] = jnp.dot(
            _unpacked(u_ref[r0 // 2:(r0 + N_KEYS) // 2, :]),
            _unpacked(h2t_ref[:, lh * MXU_N:(lh + 1) * MXU_N]), preferred_element_type=F32).astype(BF16)

    def act_block(kc, ac, lc):
        r0 = kc * MXU_K + ac * N_KEYS
        zb = z_buf[kc % 2, ac * N_KEYS:(ac + 1) * N_KEYS, lc * LANES:(lc + 1) * LANES]
        a_buf[r0:r0 + N_KEYS, lc * LANES:(lc + 1) * LANES] = _gelu_tanh(zb) * g_buf[kc % 2, ac, lc]

    outs = {}

    def out_piece(kc, r, lh):
        e0 = kc * MXU_K
        part = jnp.dot(_unpacked(vt_ref[r * rh // 2:(r + 1) * rh // 2, e0:e0 + MXU_K]),
                       a_buf[e0:e0 + MXU_K, lh * MXU_N:(lh + 1) * MXU_N], preferred_element_type=F32)
        outs[(r, lh)] = part if (r, lh) not in outs else outs[(r, lh)] + part

    blocks = [(ac, lc) for lc in range(nlc) for ac in range(apc)]
    pieces = [(ac, lh) for lh in range(nlh) for ac in range(apc)]
    for s, blk in enumerate(blocks):
        gate_block(0, *blk)
        if s % 2 == 1:
            score_piece(0, *pieces[s // 2])
    pending = []
    for kc in range(nkc):
        mxu = list(pending)
        pending = []
        if kc + 1 < nkc:
            mxu += [functools.partial(score_piece, kc + 1, ac, lh) for ac, lh in pieces]
        for s, (ac, lc) in enumerate(blocks):
            if kc + 1 < nkc:
                gate_block(kc + 1, ac, lc)
            act_block(kc, ac, lc)
            if mxu:
                mxu.pop(0)()
            if ac == apc - 1 and (lc + 1) % (MXU_N // LANES) == 0:
                lh = lc // (MXU_N // LANES)
                ready = [functools.partial(out_piece, kc, r, lh) for r in range(d // rh)]
                if lh + 1 < nlh:
                    mxu += ready
                else:
                    pending = ready
        for f in mxu:
            f()
    for f in pending:
        f()
    for (r, lh), val in outs.items():
        acc[r * rh:(r + 1) * rh, lh * MXU_N:(lh + 1) * MXU_N] += val

    @pl.when(j == pl.num_programs(1) - 1)
    def _():
        y = acc[...].T
        o_ref[...] = x_ref[...] + ada_ref[0][5:6, :] * _rms_norm(y, npost_ref[...])


def _peer(h2t, u_bf, vt_bf, cnt, rk2, p1, p2, x1, ada3, norm_post):
    t, d = x1.shape
    s = t // ada3.shape[0]
    ne = 2 * u_bf.shape[0]
    tm = min(TM_PEER, s)
    te = TE_PEER
    tab2_spec = pl.BlockSpec((P_HEADS, tm // LANES, N_KEYS // 2, LANES), lambda i, j: (0, i, 0, 0))
    tab1_spec = pl.BlockSpec((P_HEADS, tm // LANES, te // N_KEYS, LANES), lambda i, j: (0, i, j, 0))
    return pl.pallas_call(
        functools.partial(_peer_kernel, tm=tm, te=te),
        grid=(t // tm, ne // te),
        in_specs=[pl.BlockSpec((d // 2, tm), lambda i, j: (0, i)),
                  pl.BlockSpec((te // 2, d), lambda i, j: (j, 0)),
                  pl.BlockSpec((d // 2, te), lambda i, j: (0, j)),
                  tab1_spec, tab2_spec, tab1_spec, tab2_spec,
                  pl.BlockSpec((tm, d), lambda i, j: (i, 0)),
                  pl.BlockSpec((1, 6, d), lambda i, j: ((i * tm) // s, 0, 0)),
                  pl.BlockSpec((1, d), lambda i, j: (0, 0))],
        out_specs=pl.BlockSpec((tm, d), lambda i, j: (i, 0)),
        out_shape=jax.ShapeDtypeStruct((t, d), F32),
        scratch_shapes=[pltpu.VMEM((d, tm), F32),
                        pltpu.VMEM((2, MXU_K // N_KEYS, tm // LANES, N_KEYS, LANES), BF16),
                        pltpu.VMEM((2, MXU_K, tm), BF16),
                        pltpu.VMEM((te, tm), BF16)],
        compiler_params=pltpu.CompilerParams(dimension_semantics=("arbitrary", "arbitrary"),
                                             vmem_limit_bytes=VMEM_LIMIT_BYTES),
        name="peer",
    )(h2t, u_bf, vt_bf, cnt, rk2, p1, p2, x1, ada3, norm_post)


def _layer(x, c, w_ada, b_ada, norm1_pre, norm1_post, w_in, conv_a_w, conv_qk_w, b_igate, b_fgate, mh_norm_w,
           w_branch_a, w_branch_m, w_out, norm2_pre, norm2_post, peer_wq, peer_subkeys, peer_u, peer_v):
    bsz, s, d = x.shape
    ng = 2 * M_HEADS
    ada3 = _ada(c, w_ada, b_ada).reshape(bsz, 6, d)

    o_if = 3 * d + 2 * M_HEADS * M_DK + 2 * d
    w_main = jnp.concatenate([w_in[:, :o_if], w_in[:, o_if + ng:]], axis=1).astype(BF16)
    w_if = w_in[:, o_if:o_if + ng]
    wif_col = jnp.pad(w_if, ((0, 0), (0, LANES - ng))).astype(BF16)
    wif_row = w_if.T.astype(BF16)
    b_if = jnp.concatenate([b_igate, b_fgate])
    bif_col = jnp.pad(b_if, (0, LANES - ng)).reshape(1, LANES)
    bif_row = b_if.reshape(ng, 1)

    ya, q, kt, v, so, sga, sgm, gcol, grow = _inproj(
        x, ada3, norm1_pre.reshape(1, d), w_main, wif_col, wif_row, bif_col, bif_row, conv_a_w, conv_qk_w)
    x1 = _mixer(x, ada3, q, kt, v, so, sga, sgm, ya, gcol, grow, mh_norm_w.reshape(1, d),
                norm1_post.reshape(1, d), w_branch_a.astype(BF16), w_branch_m.astype(BF16), w_out.astype(BF16))

    x1f = x1.reshape(bsz * s, d)
    h2t, cnt, rk2, p1, p2 = _route(x1f, ada3, norm2_pre.reshape(1, d), peer_wq.T.astype(BF16),
                                   peer_subkeys.astype(BF16))
    u_words, vt_words = _expert_words(peer_u, peer_v)
    out = _peer(h2t, u_words, vt_words, cnt, rk2, p1, p2, x1f, ada3, norm2_post.reshape(1, d))
    return out.reshape(bsz, s, d)


def kernel(x, c, w_ada, b_ada, norm1_pre, norm1_post, w_in, conv_a_w, conv_qk_w, b_igate, b_fgate, mh_norm_w, w_branch_a, w_branch_m, w_out, norm2_pre, norm2_post, peer_wq, peer_subkeys, peer_u, peer_v):
    for l in range(w_ada.shape[0]):
        x = _layer(x, c, w_ada[l], b_ada[l], norm1_pre[l], norm1_post[l], w_in[l], conv_a_w[l], conv_qk_w[l],
                   b_igate[l], b_fgate[l], mh_norm_w[l], w_branch_a[l], w_branch_m[l], w_out[l], norm2_pre[l],
                   norm2_post[l], peer_wq[l], peer_subkeys[l], peer_u[l], peer_v[l])
    return x
```

```python
import functools

import jax
import jax.numpy as jnp
from jax import lax
from jax.experimental import pallas as pl
from jax.experimental.pallas import tpu as pltpu

F32 = jnp.float32
BF16 = jnp.bfloat16

EPS = 1e-6
M_HEADS = 8
M_DK = 64
M_DV = 128
CONV_K = 3
QK_CONV_K = 4
P_HEADS = 8
N_KEYS = 128
P_TOPK = 16
HALO = 8
LANES = 128
MXU_K = 256
MXU_N = 256
BF16_ROWS = 16
NOT_RANKED = 99.0
KNOCK = 2.0 ** 100

VMEM_LIMIT_BYTES = 56 * 1024 * 1024

MLSTM_CHUNK = 128
TS_INPROJ = 256
TS_MIXER = 512
TM_ROUTE = 256
TM_PEER = 1024
TE_PEER = 1024
PEER_GROUP = 2


def _resident(shape):
    nd = len(shape)
    return pl.BlockSpec(shape, lambda *_: (0,) * nd, pipeline_mode=pl.Buffered(1))


def _rms_norm(x, w):
    return x * lax.rsqrt(jnp.mean(x * x, axis=-1, keepdims=True) + EPS) * w


def _sigmoid(x):
    return 1.0 / (1.0 + jnp.exp(-x))


def _packed_words(x):
    return pltpu.bitcast(x, jnp.uint32)


def _unpacked(words):
    return pltpu.bitcast(words, BF16)


def _ada_kernel(c_ref, w_ref, b_ref, o_ref):
    c = c_ref[...]
    s = c * _sigmoid(c)
    o_ref[...] = jnp.dot(s.astype(BF16), w_ref[...].astype(BF16), preferred_element_type=F32) + b_ref[...]


def _ada(c, w_ada, b_ada):
    bsz, d = c.shape
    n = w_ada.shape[1]
    return pl.pallas_call(
        _ada_kernel,
        grid=(n // d,),
        in_specs=[pl.BlockSpec((bsz, d), lambda j: (0, 0)),
                  pl.BlockSpec((d, d), lambda j: (0, j)),
                  pl.BlockSpec((1, d), lambda j: (0, j))],
        out_specs=pl.BlockSpec((bsz, d), lambda j: (0, j)),
        out_shape=jax.ShapeDtypeStruct((bsz, n), F32),
        compiler_params=pltpu.CompilerParams(vmem_limit_bytes=VMEM_LIMIT_BYTES),
        name="ada",
    )(c, w_ada, b_ada.reshape(1, n))


def _inproj_kernel(x_ref, ada_ref, npre_ref, w_ref, wifc_ref, wifr_ref, bifc_ref, bifr_ref, cwa_ref, cwqk_ref,
                   ya_ref, q_ref, kt_ref, v_ref, so_ref, sga_ref, sgm_ref, gcol_ref, grow_ref,
                   ubuf, qkbuf, *, ts, d):
    @pl.when(pl.program_id(1) == 0)
    def _():
        ubuf[0:HALO, :] = jnp.zeros((HALO, d), F32)
        qkbuf[0:HALO, :] = jnp.zeros((HALO, d), F32)

    x = x_ref[0]
    ada = ada_ref[0]
    h = _rms_norm(x, npre_ref[...]) * (1.0 + ada[1:2, :]) + ada[0:1, :]
    hb = h.astype(BF16)

    def proj(k):
        return jnp.dot(hb, w_ref[:, k * d:(k + 1) * d], preferred_element_type=F32)

    u = proj(0) * proj(2)
    ubuf[HALO:HALO + ts, :] = u
    conv = u * cwa_ref[CONV_K - 1:CONV_K, :]
    for j in range(CONV_K - 1):
        off = HALO - (CONV_K - 1) + j
        conv = conv + ubuf[off:off + ts, :] * cwa_ref[j:j + 1, :]
    ya_ref[0] = _packed_words((proj(1) * conv).astype(BF16))
    ubuf[0:HALO, :] = ubuf[ts:ts + HALO, :]

    qk = proj(3)
    qkbuf[HALO:HALO + ts, :] = qk
    cq = qk * cwqk_ref[QK_CONV_K - 1:QK_CONV_K, :]
    for j in range(QK_CONV_K - 1):
        off = HALO - (QK_CONV_K - 1) + j
        cq = cq + qkbuf[off:off + ts, :] * cwqk_ref[j:j + 1, :]
    qkbuf[0:HALO, :] = qkbuf[ts:ts + HALO, :]
    cq = cq * _sigmoid(cq)
    nqk = M_HEADS * M_DK
    q_ref[0] = _packed_words((cq[:, :nqk] * (M_DK ** -0.5)).astype(BF16))
    kt_ref[0] = _packed_words(cq[:, nqk:].T.astype(BF16))

    v_ref[0] = _packed_words(proj(4).astype(BF16))
    so_ref[0] = _sigmoid(proj(5)).astype(BF16)
    sga_ref[0] = _sigmoid(proj(6)).astype(BF16)
    sgm_ref[0] = _sigmoid(proj(7)).astype(BF16)

    gcol_ref[0] = jnp.dot(hb, wifc_ref[...], preferred_element_type=F32) + bifc_ref[...]
    grow_ref[0] = lax.dot_general(wifr_ref[...], hb, (((1,), (1,)), ((), ())),
                                  preferred_element_type=F32) + bifr_ref[...]


def _inproj(x, ada3, norm_pre, w_main, wif_col, wif_row, bif_col, bif_row, conv_a_w, conv_qk_w):
    bsz, s, d = x.shape
    ts = min(TS_INPROJ, s)
    ng = 2 * M_HEADS
    nqk = M_HEADS * M_DK
    tok = lambda dt, w=d: jax.ShapeDtypeStruct((bsz, s, w), dt)
    tok_spec = lambda w=d: pl.BlockSpec((1, ts, w), lambda b, i: (b, i, 0))
    words = lambda w=d: jax.ShapeDtypeStruct((bsz, s // 2, w), jnp.uint32)
    words_spec = lambda w=d: pl.BlockSpec((1, ts // 2, w), lambda b, i: (b, i, 0))
    return pl.pallas_call(
        functools.partial(_inproj_kernel, ts=ts, d=d),
        grid=(bsz, s // ts),
        in_specs=[tok_spec(),
                  pl.BlockSpec((1, 6, d), lambda b, i: (b, 0, 0)),
                  _resident((1, d)),
                  _resident(w_main.shape),
                  _resident(wif_col.shape),
                  _resident(wif_row.shape),
                  _resident(bif_col.shape),
                  _resident(bif_row.shape),
                  _resident(conv_a_w.shape),
                  _resident(conv_qk_w.shape)],
        out_specs=[words_spec(), words_spec(nqk),
                   pl.BlockSpec((1, nqk // 2, ts), lambda b, i: (b, 0, i)),
                   words_spec(), tok_spec(), tok_spec(), tok_spec(),
                   tok_spec(LANES),
                   pl.BlockSpec((1, ng, ts), lambda b, i: (b, 0, i))],
        out_shape=[words(), words(nqk),
                   jax.ShapeDtypeStruct((bsz, nqk // 2, s), jnp.uint32),
                   words(), tok(BF16), tok(BF16), tok(BF16),
                   tok(F32, LANES),
                   jax.ShapeDtypeStruct((bsz, ng, s), F32)],
        scratch_shapes=[pltpu.VMEM((HALO + ts, d), F32), pltpu.VMEM((HALO + ts, d), F32)],
        compiler_params=pltpu.CompilerParams(dimension_semantics=("arbitrary", "arbitrary"),
                                             vmem_limit_bytes=VMEM_LIMIT_BYTES),
        name="inproj",
    )(x, ada3, norm_pre, w_main, wif_col, wif_row, bif_col, bif_row, conv_a_w, conv_qk_w)


def _log_sigmoid(x):
    return jnp.minimum(x, 0.0) - jnp.log(1.0 + jnp.exp(-jnp.abs(x)))


def _chunk_scan(x, axis, chunk, op, identity):
    pos = lax.broadcasted_iota(jnp.int32, x.shape, axis) % chunk
    k = 1
    while k < chunk:
        x = op(x, jnp.where(pos >= k, pltpu.roll(x, k, axis), identity))
        k *= 2
    return x


def _bf16_terms(x):
    hi = x.astype(BF16)
    r = x - hi.astype(F32)
    mid = r.astype(BF16)
    lo = (r - mid.astype(F32)).astype(BF16)
    return hi, mid, lo


def _spread_heads(x, sel):
    return sum(jnp.dot(t, sel, preferred_element_type=F32) for t in _bf16_terms(x))


def _mixer_kernel(x_ref, ada_ref, q_ref, kt_ref, v_ref, so_ref, sga_ref, sgm_ref, ya_ref, gcol_ref, grow_ref,
                  sel_ref, mhw_ref, npost_ref, wa_ref, wm_ref, wo_ref, o_ref,
                  c_st, mcol_st, mrow_st, ym_buf, *, ts, chunk):
    first = pl.program_id(1) == 0

    @pl.when(first)
    def _():
        c_st[...] = jnp.zeros(c_st.shape, F32)
        mcol_st[...] = jnp.zeros(mcol_st.shape, F32)
        mrow_st[...] = jnp.zeros(mrow_st.shape, F32)

    nc = ts // chunk
    H = M_HEADS
    neg_inf = -jnp.inf

    gcol = gcol_ref[0]
    b_col = _chunk_scan(_log_sigmoid(gcol), 0, chunk, jnp.add, 0.0)
    r_col = pltpu.roll(gcol, H, 1) - b_col
    cm_col = _chunk_scan(r_col, 0, chunk, jnp.maximum, neg_inf)
    m = mcol_st[0:1, :]
    m_rows, decay_rows = [], []
    for c in range(nc):
        last = (c + 1) * chunk - 1
        b_l, cm_l = b_col[last:last + 1, :], cm_col[last:last + 1, :]
        m_next = b_l + jnp.maximum(m, cm_l)
        m_rows.append(m)
        decay_rows.append(jnp.exp(b_l + m - m_next))
        m = m_next
    mcol_st[...] = jnp.broadcast_to(m, mcol_st.shape)
    pad = jnp.zeros((8 - (2 * nc) % 8, LANES), F32)
    spread = _spread_heads(jnp.concatenate([b_col, cm_col] + m_rows + decay_rows + [pad], axis=0), sel_ref[...])
    bc_all, cm_all, sc_all = spread[0:ts], spread[ts:2 * ts], spread[2 * ts:]

    grow = grow_ref[0]
    i_row = grow[0:H, :]
    b_row = _chunk_scan(_log_sigmoid(grow[H:2 * H, :]), 1, chunk, jnp.add, 0.0)
    r_row = i_row - b_row
    cm_row = _chunk_scan(r_row, 1, chunk, jnp.maximum, neg_inf)
    mr = mrow_st[:, 0:1]
    ws_rows = []
    for c in range(nc):
        last = (c + 1) * chunk - 1
        b_l, cm_l = b_row[:, last:last + 1], cm_row[:, last:last + 1]
        m_next = b_l + jnp.maximum(mr, cm_l)
        ws_rows.append(jnp.exp(b_l + r_row[:, c * chunk:(c + 1) * chunk] - m_next))
        mr = m_next
    mrow_st[...] = jnp.broadcast_to(mr, mrow_st.shape)

    tril = (lax.broadcasted_iota(jnp.int32, (chunk, chunk), 0)
            >= lax.broadcasted_iota(jnp.int32, (chunk, chunk), 1))
    ones = jnp.ones((chunk, M_DV), BF16)

    for hd in range(H):
        hl = slice(hd * M_DV, (hd + 1) * M_DV)
        c_aug = c_st[hd]
        for c in range(nc):
            rows = slice(c * chunk, (c + 1) * chunk)
            wrows = slice(c * chunk // 2, (c + 1) * chunk // 2)
            q = _unpacked(q_ref[0, wrows, hd * M_DK:(hd + 1) * M_DK])
            kt = _unpacked(kt_ref[0, hd * M_DK // 2:(hd + 1) * M_DK // 2, rows])
            v = _unpacked(v_ref[0, wrows, hl])
            bc = bc_all[rows, hl]
            m_prev = sc_all[c:c + 1, hl]
            decay = sc_all[nc + c:nc + c + 1, hl]
            m_t = bc + jnp.maximum(m_prev, cm_all[rows, hl])
            dlog = jnp.where(tril, bc[:, :chunk] + r_row[hd:hd + 1, rows], neg_inf)
            s_qk = jnp.dot(q, kt, preferred_element_type=F32) * jnp.exp(dlog - m_t[:, :chunk])
            a_inter = jnp.exp(bc + m_prev - m_t)
            qc = jnp.dot(q, c_aug.astype(BF16), preferred_element_type=F32)
            num = jnp.dot(s_qk.astype(BF16), v, preferred_element_type=F32) + a_inter * qc[:, :M_DV]
            den = jnp.sum(s_qk, axis=1, keepdims=True) + a_inter * qc[:, M_DV:]
            h = num / jnp.maximum(jnp.abs(den), jnp.exp(-m_t))
            hn = _rms_norm(h, mhw_ref[:, hl])
            ym_buf[rows, hl] = (so_ref[0, rows, hl].astype(F32) * hn).astype(BF16)
            ktw = (kt.astype(F32) * ws_rows[c][hd:hd + 1, :]).astype(BF16)
            upd = jnp.dot(ktw, jnp.concatenate([v, ones], axis=1), preferred_element_type=F32)
            c_aug = jnp.concatenate([decay, decay], axis=1) * c_aug + upd
        c_st[hd] = c_aug

    mix = (sga_ref[0].astype(F32) * jnp.dot(_unpacked(ya_ref[0]), wa_ref[...], preferred_element_type=F32)
           + sgm_ref[0].astype(F32) * jnp.dot(ym_buf[...], wm_ref[...], preferred_element_type=F32))
    y = jnp.dot(mix.astype(BF16), wo_ref[...], preferred_element_type=F32)
    o_ref[0] = x_ref[0] + ada_ref[0][2:3, :] * _rms_norm(y, npost_ref[...])


def _mixer(x, ada3, q, kt, v, so, sga, sgm, ya, gcol, grow, mh_norm_w, norm_post, wa, wm, wo):
    bsz, s, d = x.shape
    ts = min(TS_MIXER, s)
    chunk = min(MLSTM_CHUNK, ts)
    ng = 2 * M_HEADS
    nqk = M_HEADS * M_DK
    sel = (jnp.arange(LANES)[:, None] - M_HEADS == jnp.arange(M_HEADS * M_DV)[None, :] // M_DV).astype(BF16)
    tok_spec = lambda w=d: pl.BlockSpec((1, ts, w), lambda b, i: (b, i, 0))
    words_spec = lambda w=d: pl.BlockSpec((1, ts // 2, w), lambda b, i: (b, i, 0))
    return pl.pallas_call(
        functools.partial(_mixer_kernel, ts=ts, chunk=chunk),
        grid=(bsz, s // ts),
        in_specs=[tok_spec(),
                  pl.BlockSpec((1, 6, d), lambda b, i: (b, 0, 0)),
                  words_spec(nqk),
                  pl.BlockSpec((1, nqk // 2, ts), lambda b, i: (b, 0, i)),
                  words_spec(), tok_spec(), tok_spec(), tok_spec(), words_spec(),
                  tok_spec(LANES),
                  pl.BlockSpec((1, ng, ts), lambda b, i: (b, 0, i)),
                  _resident(sel.shape),
                  _resident((1, d)), _resident((1, d)),
                  _resident(wa.shape), _resident(wm.shape), _resident(wo.shape)],
        out_specs=tok_spec(),
        out_shape=jax.ShapeDtypeStruct((bsz, s, d), F32),
        scratch_shapes=[pltpu.VMEM((M_HEADS, M_DK, 2 * M_DV), F32),
                        pltpu.VMEM((8, LANES), F32),
                        pltpu.VMEM((M_HEADS, LANES), F32),
                        pltpu.VMEM((ts, d), BF16)],
        compiler_params=pltpu.CompilerParams(dimension_semantics=("arbitrary", "arbitrary"),
                                             vmem_limit_bytes=VMEM_LIMIT_BYTES),
        name="mixer",
    )(x, ada3, q, kt, v, so, sga, sgm, ya, gcol, grow, sel, mh_norm_w, norm_post, wa, wm, wo)


def _argmax_rows(s, exact):
    m = jnp.max(s, axis=0, keepdims=True)
    hit = s == m
    if exact:
        row = lax.broadcasted_iota(jnp.int32, s.shape, 0)
        hit = row == jnp.min(jnp.where(hit, row, s.shape[0]), axis=0, keepdims=True)
    return hit, m


def _topk_rows(arrays, k, exact):
    arrays = list(arrays)
    vals = [[] for _ in arrays]
    if exact:
        ranks = [jnp.full(s.shape, NOT_RANKED, F32) for s in arrays]
        for r in range(k):
            for i, s in enumerate(arrays):
                hit, m = _argmax_rows(s, exact)
                ranks[i] = jnp.where(hit, float(r), ranks[i])
                arrays[i] = jnp.where(hit, -jnp.inf, s)
                vals[i].append(m)
        return vals, ranks
    for r in range(k):
        for i, s in enumerate(arrays):
            hit, m = _argmax_rows(s, exact)
            arrays[i] = jnp.where(hit, -(r + 2.0) * KNOCK, s)
            vals[i].append(m)
    ranks = [jnp.where(s <= -1.5 * KNOCK, s * (-1.0 / KNOCK) - 2.0, NOT_RANKED) for s in arrays]
    return vals, ranks


def _route_tables(s1, s2, exact):
    (v1, v2), (rank1, rank2) = _topk_rows((s1, s2), P_TOPK, exact)
    v2_lo = jnp.concatenate(v2[0:8], axis=0)
    v2_all = jnp.concatenate(v2, axis=0)
    v1_hi = jnp.concatenate(v1[8:16], axis=0)
    cand = jnp.concatenate([v1[0] + v2_all] + [v1[j] + v2_lo for j in range(1, 8)] + [v1_hi + v2[0]], axis=0)
    cur = cand
    for _ in range(P_TOPK):
        hit, _ = _argmax_rows(cur, exact)
        cur = jnp.where(hit, -jnp.inf, cur)
    sel = jnp.where(cur == -jnp.inf, 1.0, 0.0)
    ranked = lambda rk: jnp.sum(jnp.where(rk < NOT_RANKED, 1.0, 0.0), axis=0, keepdims=True)
    picked = jnp.maximum(jnp.maximum(ranked(rank1), ranked(rank2)), jnp.sum(sel, axis=0, keepdims=True))
    lowest = jnp.minimum(jnp.min(s1, axis=0, keepdims=True), jnp.min(s2, axis=0, keepdims=True))
    tied = jnp.where((picked > float(P_TOPK)) | (lowest <= -KNOCK), 1.0, 0.0)
    top = v1[0] + v2[0]
    z = jnp.sum(sel * jnp.exp(cand - top), axis=0, keepdims=True)
    counts = [jnp.sum(sel[0:16], axis=0, keepdims=True)]
    counts += [jnp.sum(sel[8 + 8 * j:16 + 8 * j], axis=0, keepdims=True) for j in range(1, 8)]
    counts += [sel[72 + j:73 + j] for j in range(8)]
    cnt = jnp.zeros(s1.shape, F32)
    for j in range(P_TOPK):
        cnt = jnp.where(rank1 == float(j), counts[j], cnt)
    p1 = jnp.exp(s1 - v1[0])
    p2 = jnp.exp(s2 - v2[0]) * (1.0 / z)
    return cnt, rank2, p1, p2, tied


def _bf16_pair_words(x):
    hi = pltpu.bitcast(x.astype(BF16).astype(F32), jnp.uint32)
    return hi | lax.shift_right_logical(hi, jnp.uint32(16))


def _route_kernel(x_ref, ada_ref, npre_ref, wqt_ref, sk_ref, h2t_ref, cnt_ref, rk2_ref, p1_ref, p2_ref, qt_buf,
                  *, tm):
    ada = ada_ref[0]
    h2 = _rms_norm(x_ref[...], npre_ref[...]) * (1.0 + ada[4:5, :]) + ada[3:4, :]
    h2t = h2.T.astype(BF16)
    h2t_ref[...] = _packed_words(h2t)
    qt_buf[...] = jnp.dot(wqt_ref[...], h2t, preferred_element_type=F32).astype(BF16)

    def head(hd, carry):
        r0 = pl.multiple_of(hd * (2 * N_KEYS), 2 * N_KEYS)
        s1 = jnp.dot(sk_ref[hd, 0], qt_buf[pl.ds(r0, N_KEYS), :], preferred_element_type=F32)
        s2 = jnp.dot(sk_ref[hd, 1], qt_buf[pl.ds(r0 + N_KEYS, N_KEYS), :], preferred_element_type=F32)

        def tables(lc, exact):
            sl = slice(lc * LANES, (lc + 1) * LANES)
            cnt, rk2, p1, p2, tied = _route_tables(s1[:, sl], s2[:, sl], exact)
            cnt_ref[hd, lc] = _bf16_pair_words(cnt)
            rk2_ref[hd, lc] = _packed_words(rk2.astype(BF16))
            p1_ref[hd, lc] = _bf16_pair_words(p1)
            p2_ref[hd, lc] = _packed_words(p2.astype(BF16))
            return tied

        for lc in range(tm // LANES):
            tied = tables(lc, exact=False)

            @pl.when(jnp.max(tied) > 0.0)
            def _():
                tables(lc, exact=True)
        return carry

    lax.fori_loop(0, P_HEADS, head, 0)


def _route(x1, ada3, norm_pre, wqt, subkeys):
    t, d = x1.shape
    s = t // ada3.shape[0]
    tm = min(TM_ROUTE, s)
    tab = lambda rows: jax.ShapeDtypeStruct((P_HEADS, t // LANES, rows, LANES), jnp.uint32)
    tab_spec = lambda rows: pl.BlockSpec((P_HEADS, tm // LANES, rows, LANES), lambda i: (0, i, 0, 0))
    k1, k2 = N_KEYS, N_KEYS // 2
    return pl.pallas_call(
        functools.partial(_route_kernel, tm=tm),
        grid=(t // tm,),
        in_specs=[pl.BlockSpec((tm, d), lambda i: (i, 0)),
                  pl.BlockSpec((1, 6, d), lambda i: ((i * tm) // s, 0, 0)),
                  _resident((1, d)),
                  _resident(wqt.shape),
                  _resident(subkeys.shape)],
        out_specs=[pl.BlockSpec((d // 2, tm), lambda i: (0, i)),
                   tab_spec(k1), tab_spec(k2), tab_spec(k1), tab_spec(k2)],
        out_shape=[jax.ShapeDtypeStruct((d // 2, t), jnp.uint32), tab(k1), tab(k2), tab(k1), tab(k2)],
        scratch_shapes=[pltpu.VMEM((wqt.shape[0], tm), BF16)],
        compiler_params=pltpu.CompilerParams(dimension_semantics=("arbitrary",),
                                             vmem_limit_bytes=VMEM_LIMIT_BYTES),
        name="route",
    )(x1, ada3, norm_pre, wqt, subkeys)


def _expert_words_kernel(u_ref, v_ref, uw_ref, vtw_ref):
    uw_ref[...] = _packed_words(u_ref[...].astype(BF16))
    vtw_ref[...] = _packed_words(v_ref[...].T.astype(BF16))


def _expert_words(u, v):
    ne, d = u.shape
    te = min(TE_PEER, ne)
    return pl.pallas_call(
        _expert_words_kernel,
        grid=(ne // te,),
        in_specs=[pl.BlockSpec((te, d), lambda j: (j, 0)), pl.BlockSpec((te, d), lambda j: (j, 0))],
        out_specs=[pl.BlockSpec((te // 2, d), lambda j: (j, 0)), pl.BlockSpec((d // 2, te), lambda j: (0, j))],
        out_shape=[jax.ShapeDtypeStruct((ne // 2, d), jnp.uint32), jax.ShapeDtypeStruct((d // 2, ne), jnp.uint32)],
        compiler_params=pltpu.CompilerParams(vmem_limit_bytes=VMEM_LIMIT_BYTES),
        name="expert_words",
    )(u, v)


def _gelu_tanh(z):
    return 0.5 * z * (1.0 + jnp.tanh(0.7978845608028654 * (z + 0.044715 * (z * z * z))))


def _peer_kernel(h2t_ref, u_ref, vt_ref, cnt_ref, rk2_ref, p1_ref, p2_ref, x_ref, ada_ref, npost_ref, o_ref,
                 acc, z_buf, a_buf, *, tm, te):
    j = pl.program_id(1)

    @pl.when(j == 0)
    def _():
        acc[...] = jnp.zeros(acc.shape, F32)

    z_buf[...] = jnp.dot(_unpacked(u_ref[...]), _unpacked(h2t_ref[...]), preferred_element_type=F32).astype(BF16)

    pk = (N_KEYS // BF16_ROWS, BF16_ROWS, LANES)
    for lc in range(tm // LANES):
        sl = slice(lc * LANES, (lc + 1) * LANES)
        for a0 in range(0, te // N_KEYS, PEER_GROUP):
            gates = [None] * PEER_GROUP
            for hd in range(P_HEADS):
                rk2 = _unpacked(rk2_ref[hd, lc]).reshape(pk)
                p2 = _unpacked(p2_ref[hd, lc]).reshape(pk)
                for g in range(PEER_GROUP):
                    al = a0 + g
                    cnt_a = pltpu.bitcast(jnp.broadcast_to(cnt_ref[hd, lc, al:al + 1, :], (8, LANES)), BF16)
                    p1_a = pltpu.bitcast(jnp.broadcast_to(p1_ref[hd, lc, al:al + 1, :], (8, LANES)), BF16)
                    term = jnp.where(rk2 < cnt_a[None], p2 * p1_a[None], jnp.zeros((), BF16))
                    gates[g] = term if gates[g] is None else gates[g] + term
            for g in range(PEER_GROUP):
                rows = slice((a0 + g) * N_KEYS, (a0 + g + 1) * N_KEYS)
                a_buf[rows, sl] = _gelu_tanh(z_buf[rows, sl]) * gates[g].reshape(N_KEYS, LANES)

    acc[...] += jnp.dot(_unpacked(vt_ref[...]), a_buf[...], preferred_element_type=F32)

    @pl.when(j == pl.num_programs(1) - 1)
    def _():
        y = acc[...].T
        o_ref[...] = x_ref[...] + ada_ref[0][5:6, :] * _rms_norm(y, npost_ref[...])


def _peer(h2t, u_bf, vt_bf, cnt, rk2, p1, p2, x1, ada3, norm_post):
    t, d = x1.shape
    s = t // ada3.shape[0]
    ne = 2 * u_bf.shape[0]
    tm = min(TM_PEER, s)
    te = TE_PEER
    tab2_spec = pl.BlockSpec((P_HEADS, tm // LANES, N_KEYS // 2, LANES), lambda i, j: (0, i, 0, 0))
    tab1_spec = pl.BlockSpec((P_HEADS, tm // LANES, te // N_KEYS, LANES), lambda i, j: (0, i, j, 0))
    return pl.pallas_call(
        functools.partial(_peer_kernel, tm=tm, te=te),
        grid=(t // tm, ne // te),
        in_specs=[pl.BlockSpec((d // 2, tm), lambda i, j: (0, i)),
                  pl.BlockSpec((te // 2, d), lambda i, j: (j, 0)),
                  pl.BlockSpec((d // 2, te), lambda i, j: (0, j)),
                  tab1_spec, tab2_spec, tab1_spec, tab2_spec,
                  pl.BlockSpec((tm, d), lambda i, j: (i, 0)),
                  pl.BlockSpec((1, 6, d), lambda i, j: ((i * tm) // s, 0, 0)),
                  pl.BlockSpec((1, d), lambda i, j: (0, 0))],
        out_specs=pl.BlockSpec((tm, d), lambda i, j: (i, 0)),
        out_shape=jax.ShapeDtypeStruct((t, d), F32),
        scratch_shapes=[pltpu.VMEM((d, tm), F32),
                        pltpu.VMEM((te, tm), BF16),
                        pltpu.VMEM((te, tm), BF16)],
        compiler_params=pltpu.CompilerParams(dimension_semantics=("arbitrary", "arbitrary"),
                                             vmem_limit_bytes=VMEM_LIMIT_BYTES),
        name="peer",
    )(h2t, u_bf, vt_bf, cnt, rk2, p1, p2, x1, ada3, norm_post)


def _layer(x, c, w_ada, b_ada, norm1_pre, norm1_post, w_in, conv_a_w, conv_qk_w, b_igate, b_fgate, mh_norm_w,
           w_branch_a, w_branch_m, w_out, norm2_pre, norm2_post, peer_wq, peer_subkeys, peer_u, peer_v):
    bsz, s, d = x.shape
    ng = 2 * M_HEADS
    ada3 = _ada(c, w_ada, b_ada).reshape(bsz, 6, d)

    o_if = 3 * d + 2 * M_HEADS * M_DK + 2 * d
    w_main = jnp.concatenate([w_in[:, :o_if], w_in[:, o_if + ng:]], axis=1).astype(BF16)
    w_if = w_in[:, o_if:o_if + ng]
    wif_col = jnp.pad(w_if, ((0, 0), (0, LANES - ng))).astype(BF16)
    wif_row = w_if.T.astype(BF16)
    b_if = jnp.concatenate([b_igate, b_fgate])
    bif_col = jnp.pad(b_if, (0, LANES - ng)).reshape(1, LANES)
    bif_row = b_if.reshape(ng, 1)

    ya, q, kt, v, so, sga, sgm, gcol, grow = _inproj(
        x, ada3, norm1_pre.reshape(1, d), w_main, wif_col, wif_row, bif_col, bif_row, conv_a_w, conv_qk_w)
    x1 = _mixer(x, ada3, q, kt, v, so, sga, sgm, ya, gcol, grow, mh_norm_w.reshape(1, d),
                norm1_post.reshape(1, d), w_branch_a.astype(BF16), w_branch_m.astype(BF16), w_out.astype(BF16))

    x1f = x1.reshape(bsz * s, d)
    h2t, cnt, rk2, p1, p2 = _route(x1f, ada3, norm2_pre.reshape(1, d), peer_wq.T.astype(BF16),
                                   peer_subkeys.astype(BF16))
    u_words, vt_words = _expert_words(peer_u, peer_v)
    out = _peer(h2t, u_words, vt_words, cnt, rk2, p1, p2, x1f, ada3, norm2_post.reshape(1, d))
    return out.reshape(bsz, s, d)


def kernel(x, c, w_ada, b_ada, norm1_pre, norm1_post, w_in, conv_a_w, conv_qk_w, b_igate, b_fgate, mh_norm_w, w_branch_a, w_branch_m, w_out, norm2_pre, norm2_post, peer_wq, peer_subkeys, peer_u, peer_v):
    for l in range(w_ada.shape[0]):
        x = _layer(x, c, w_ada[l], b_ada[l], norm1_pre[l], norm1_post[l], w_in[l], conv_a_w[l], conv_qk_w[l],
                   b_igate[l], b_fgate[l], mh_norm_w[l], w_branch_a[l], w_branch_m[l], w_out[l], norm2_pre[l],
                   norm2_post[l], peer_wq[l], peer_subkeys[l], peer_u[l], peer_v[l])
    return x
```

```python
import functools

import jax
import jax.numpy as jnp
from jax import lax
from jax.experimental import pallas as pl
from jax.experimental.pallas import tpu as pltpu

F32 = jnp.float32
BF16 = jnp.bfloat16

EPS = 1e-6
M_HEADS = 8
M_DK = 64
M_DV = 128
CONV_K = 3
QK_CONV_K = 4
P_HEADS = 8
N_KEYS = 128
P_TOPK = 16
HALO = 8
LANES = 128
MXU_K = 256
MXU_N = 256
BF16_ROWS = 16
NOT_RANKED = 99.0
KNOCK = 2.0 ** 100

VMEM_LIMIT_BYTES = 56 * 1024 * 1024

MLSTM_CHUNK = 128
TS_INPROJ = 256
TS_MIXER = 512
TM_ROUTE = 256
TM_PEER = 1024
TE_PEER = 1024
PEER_GROUP = 2


def _resident(shape):
    nd = len(shape)
    return pl.BlockSpec(shape, lambda *_: (0,) * nd, pipeline_mode=pl.Buffered(1))


def _rms_norm(x, w):
    return x * lax.rsqrt(jnp.mean(x * x, axis=-1, keepdims=True) + EPS) * w


def _sigmoid(x):
    return 1.0 / (1.0 + jnp.exp(-x))


def _packed_words(x):
    return pltpu.bitcast(x, jnp.uint32)


def _unpacked(words):
    return pltpu.bitcast(words, BF16)


def _ada_kernel(c_ref, w_ref, b_ref, o_ref):
    c = c_ref[...]
    s = c * _sigmoid(c)
    o_ref[...] = jnp.dot(s.astype(BF16), w_ref[...].astype(BF16), preferred_element_type=F32) + b_ref[...]


def _ada(c, w_ada, b_ada):
    bsz, d = c.shape
    n = w_ada.shape[1]
    return pl.pallas_call(
        _ada_kernel,
        grid=(n // d,),
        in_specs=[pl.BlockSpec((bsz, d), lambda j: (0, 0)),
                  pl.BlockSpec((d, d), lambda j: (0, j)),
                  pl.BlockSpec((1, d), lambda j: (0, j))],
        out_specs=pl.BlockSpec((bsz, d), lambda j: (0, j)),
        out_shape=jax.ShapeDtypeStruct((bsz, n), F32),
        compiler_params=pltpu.CompilerParams(vmem_limit_bytes=VMEM_LIMIT_BYTES),
        name="ada",
    )(c, w_ada, b_ada.reshape(1, n))


def _inproj_kernel(x_ref, ada_ref, npre_ref, w_ref, wifc_ref, wifr_ref, bifc_ref, bifr_ref, cwa_ref, cwqk_ref,
                   ya_ref, q_ref, kt_ref, v_ref, so_ref, sga_ref, sgm_ref, gcol_ref, grow_ref,
                   ubuf, qkbuf, *, ts, d):
    @pl.when(pl.program_id(1) == 0)
    def _():
        ubuf[0:HALO, :] = jnp.zeros((HALO, d), F32)
        qkbuf[0:HALO, :] = jnp.zeros((HALO, d), F32)

    x = x_ref[0]
    ada = ada_ref[0]
    h = _rms_norm(x, npre_ref[...]) * (1.0 + ada[1:2, :]) + ada[0:1, :]
    hb = h.astype(BF16)

    def proj(k):
        return jnp.dot(hb, w_ref[:, k * d:(k + 1) * d], preferred_element_type=F32)

    u = proj(0) * proj(2)
    ubuf[HALO:HALO + ts, :] = u
    conv = u * cwa_ref[CONV_K - 1:CONV_K, :]
    for j in range(CONV_K - 1):
        off = HALO - (CONV_K - 1) + j
        conv = conv + ubuf[off:off + ts, :] * cwa_ref[j:j + 1, :]
    ya_ref[0] = _packed_words((proj(1) * conv).astype(BF16))
    ubuf[0:HALO, :] = ubuf[ts:ts + HALO, :]

    qk = proj(3)
    qkbuf[HALO:HALO + ts, :] = qk
    cq = qk * cwqk_ref[QK_CONV_K - 1:QK_CONV_K, :]
    for j in range(QK_CONV_K - 1):
        off = HALO - (QK_CONV_K - 1) + j
        cq = cq + qkbuf[off:off + ts, :] * cwqk_ref[j:j + 1, :]
    qkbuf[0:HALO, :] = qkbuf[ts:ts + HALO, :]
    cq = cq * _sigmoid(cq)
    nqk = M_HEADS * M_DK
    q_ref[0] = _packed_words((cq[:, :nqk] * (M_DK ** -0.5)).astype(BF16))
    kt_ref[0] = _packed_words(cq[:, nqk:].T.astype(BF16))

    v_ref[0] = _packed_words(proj(4).astype(BF16))
    so_ref[0] = _sigmoid(proj(5)).astype(BF16)
    sga_ref[0] = _sigmoid(proj(6)).astype(BF16)
    sgm_ref[0] = _sigmoid(proj(7)).astype(BF16)

    gcol_ref[0] = jnp.dot(hb, wifc_ref[...], preferred_element_type=F32) + bifc_ref[...]
    grow_ref[0] = lax.dot_general(wifr_ref[...], hb, (((1,), (1,)), ((), ())),
                                  preferred_element_type=F32) + bifr_ref[...]


def _inproj(x, ada3, norm_pre, w_main, wif_col, wif_row, bif_col, bif_row, conv_a_w, conv_qk_w):
    bsz, s, d = x.shape
    ts = min(TS_INPROJ, s)
    ng = 2 * M_HEADS
    nqk = M_HEADS * M_DK
    tok = lambda dt, w=d: jax.ShapeDtypeStruct((bsz, s, w), dt)
    tok_spec = lambda w=d: pl.BlockSpec((1, ts, w), lambda b, i: (b, i, 0))
    words = lambda w=d: jax.ShapeDtypeStruct((bsz, s // 2, w), jnp.uint32)
    words_spec = lambda w=d: pl.BlockSpec((1, ts // 2, w), lambda b, i: (b, i, 0))
    return pl.pallas_call(
        functools.partial(_inproj_kernel, ts=ts, d=d),
        grid=(bsz, s // ts),
        in_specs=[tok_spec(),
                  pl.BlockSpec((1, 6, d), lambda b, i: (b, 0, 0)),
                  _resident((1, d)),
                  _resident(w_main.shape),
                  _resident(wif_col.shape),
                  _resident(wif_row.shape),
                  _resident(bif_col.shape),
                  _resident(bif_row.shape),
                  _resident(conv_a_w.shape),
                  _resident(conv_qk_w.shape)],
        out_specs=[words_spec(), words_spec(nqk),
                   pl.BlockSpec((1, nqk // 2, ts), lambda b, i: (b, 0, i)),
                   words_spec(), tok_spec(), tok_spec(), tok_spec(),
                   tok_spec(LANES),
                   pl.BlockSpec((1, ng, ts), lambda b, i: (b, 0, i))],
        out_shape=[words(), words(nqk),
                   jax.ShapeDtypeStruct((bsz, nqk // 2, s), jnp.uint32),
                   words(), tok(BF16), tok(BF16), tok(BF16),
                   tok(F32, LANES),
                   jax.ShapeDtypeStruct((bsz, ng, s), F32)],
        scratch_shapes=[pltpu.VMEM((HALO + ts, d), F32), pltpu.VMEM((HALO + ts, d), F32)],
        compiler_params=pltpu.CompilerParams(dimension_semantics=("arbitrary", "arbitrary"),
                                             vmem_limit_bytes=VMEM_LIMIT_BYTES),
        name="inproj",
    )(x, ada3, norm_pre, w_main, wif_col, wif_row, bif_col, bif_row, conv_a_w, conv_qk_w)


def _log_sigmoid(x):
    return jnp.minimum(x, 0.0) - jnp.log(1.0 + jnp.exp(-jnp.abs(x)))


def _chunk_scan(x, axis, chunk, op, identity):
    pos = lax.broadcasted_iota(jnp.int32, x.shape, axis) % chunk
    k = 1
    while k < chunk:
        x = op(x, jnp.where(pos >= k, pltpu.roll(x, k, axis), identity))
        k *= 2
    return x


def _bf16_terms(x):
    hi = x.astype(BF16)
    r = x - hi.astype(F32)
    mid = r.astype(BF16)
    lo = (r - mid.astype(F32)).astype(BF16)
    return hi, mid, lo


def _spread_heads(x, sel):
    return sum(jnp.dot(t, sel, preferred_element_type=F32) for t in _bf16_terms(x))


def _mixer_kernel(x_ref, ada_ref, q_ref, kt_ref, v_ref, so_ref, sga_ref, sgm_ref, ya_ref, gcol_ref, grow_ref,
                  sel_ref, mhw_ref, npost_ref, wa_ref, wm_ref, wo_ref, o_ref,
                  c_st, mcol_st, mrow_st, ym_buf, *, ts, chunk):
    first = pl.program_id(1) == 0

    @pl.when(first)
    def _():
        c_st[...] = jnp.zeros(c_st.shape, F32)
        mcol_st[...] = jnp.zeros(mcol_st.shape, F32)
        mrow_st[...] = jnp.zeros(mrow_st.shape, F32)

    nc = ts // chunk
    H = M_HEADS
    neg_inf = -jnp.inf

    gcol = gcol_ref[0]
    b_col = _chunk_scan(_log_sigmoid(gcol), 0, chunk, jnp.add, 0.0)
    r_col = pltpu.roll(gcol, H, 1) - b_col
    cm_col = _chunk_scan(r_col, 0, chunk, jnp.maximum, neg_inf)
    m = mcol_st[0:1, :]
    m_rows, decay_rows = [], []
    for c in range(nc):
        last = (c + 1) * chunk - 1
        b_l, cm_l = b_col[last:last + 1, :], cm_col[last:last + 1, :]
        m_next = b_l + jnp.maximum(m, cm_l)
        m_rows.append(m)
        decay_rows.append(jnp.exp(b_l + m - m_next))
        m = m_next
    mcol_st[...] = jnp.broadcast_to(m, mcol_st.shape)
    pad = jnp.zeros((8 - (2 * nc) % 8, LANES), F32)
    spread = _spread_heads(jnp.concatenate([b_col, cm_col] + m_rows + decay_rows + [pad], axis=0), sel_ref[...])
    bc_all, cm_all, sc_all = spread[0:ts], spread[ts:2 * ts], spread[2 * ts:]

    grow = grow_ref[0]
    i_row = grow[0:H, :]
    b_row = _chunk_scan(_log_sigmoid(grow[H:2 * H, :]), 1, chunk, jnp.add, 0.0)
    r_row = i_row - b_row
    cm_row = _chunk_scan(r_row, 1, chunk, jnp.maximum, neg_inf)
    mr = mrow_st[:, 0:1]
    ws_rows = []
    for c in range(nc):
        last = (c + 1) * chunk - 1
        b_l, cm_l = b_row[:, last:last + 1], cm_row[:, last:last + 1]
        m_next = b_l + jnp.maximum(mr, cm_l)
        ws_rows.append(jnp.exp(b_l + r_row[:, c * chunk:(c + 1) * chunk] - m_next))
        mr = m_next
    mrow_st[...] = jnp.broadcast_to(mr, mrow_st.shape)

    tril = (lax.broadcasted_iota(jnp.int32, (chunk, chunk), 0)
            >= lax.broadcasted_iota(jnp.int32, (chunk, chunk), 1))
    ones = jnp.ones((chunk, M_DV), BF16)

    for hd in range(H):
        hl = slice(hd * M_DV, (hd + 1) * M_DV)
        c_aug = c_st[hd]
        for c in range(nc):
            rows = slice(c * chunk, (c + 1) * chunk)
            wrows = slice(c * chunk // 2, (c + 1) * chunk // 2)
            q = _unpacked(q_ref[0, wrows, hd * M_DK:(hd + 1) * M_DK])
            kt = _unpacked(kt_ref[0, hd * M_DK // 2:(hd + 1) * M_DK // 2, rows])
            v = _unpacked(v_ref[0, wrows, hl])
            bc = bc_all[rows, hl]
            m_prev = sc_all[c:c + 1, hl]
            decay = sc_all[nc + c:nc + c + 1, hl]
            m_t = bc + jnp.maximum(m_prev, cm_all[rows, hl])
            dlog = jnp.where(tril, bc[:, :chunk] + r_row[hd:hd + 1, rows], neg_inf)
            s_qk = jnp.dot(q, kt, preferred_element_type=F32) * jnp.exp(dlog - m_t[:, :chunk])
            a_inter = jnp.exp(bc + m_prev - m_t)
            qc = jnp.dot(q, c_aug.astype(BF16), preferred_element_type=F32)
            num = jnp.dot(s_qk.astype(BF16), v, preferred_element_type=F32) + a_inter * qc[:, :M_DV]
            den = jnp.sum(s_qk, axis=1, keepdims=True) + a_inter * qc[:, M_DV:]
            h = num / jnp.maximum(jnp.abs(den), jnp.exp(-m_t))
            hn = _rms_norm(h, mhw_ref[:, hl])
            ym_buf[rows, hl] = (so_ref[0, rows, hl].astype(F32) * hn).astype(BF16)
            ktw = (kt.astype(F32) * ws_rows[c][hd:hd + 1, :]).astype(BF16)
            upd = jnp.dot(ktw, jnp.concatenate([v, ones], axis=1), preferred_element_type=F32)
            c_aug = jnp.concatenate([decay, decay], axis=1) * c_aug + upd
        c_st[hd] = c_aug

    mix = (sga_ref[0].astype(F32) * jnp.dot(_unpacked(ya_ref[0]), wa_ref[...], preferred_element_type=F32)
           + sgm_ref[0].astype(F32) * jnp.dot(ym_buf[...], wm_ref[...], preferred_element_type=F32))
    y = jnp.dot(mix.astype(BF16), wo_ref[...], preferred_element_type=F32)
    o_ref[0] = x_ref[0] + ada_ref[0][2:3, :] * _rms_norm(y, npost_ref[...])


def _mixer(x, ada3, q, kt, v, so, sga, sgm, ya, gcol, grow, mh_norm_w, norm_post, wa, wm, wo):
    bsz, s, d = x.shape
    ts = min(TS_MIXER, s)
    chunk = min(MLSTM_CHUNK, ts)
    ng = 2 * M_HEADS
    nqk = M_HEADS * M_DK
    sel = (jnp.arange(LANES)[:, None] - M_HEADS == jnp.arange(M_HEADS * M_DV)[None, :] // M_DV).astype(BF16)
    tok_spec = lambda w=d: pl.BlockSpec((1, ts, w), lambda b, i: (b, i, 0))
    words_spec = lambda w=d: pl.BlockSpec((1, ts // 2, w), lambda b, i: (b, i, 0))
    return pl.pallas_call(
        functools.partial(_mixer_kernel, ts=ts, chunk=chunk),
        grid=(bsz, s // ts),
        in_specs=[tok_spec(),
                  pl.BlockSpec((1, 6, d), lambda b, i: (b, 0, 0)),
                  words_spec(nqk),
                  pl.BlockSpec((1, nqk // 2, ts), lambda b, i: (b, 0, i)),
                  words_spec(), tok_spec(), tok_spec(), tok_spec(), words_spec(),
                  tok_spec(LANES),
                  pl.BlockSpec((1, ng, ts), lambda b, i: (b, 0, i)),
                  _resident(sel.shape),
                  _resident((1, d)), _resident((1, d)),
                  _resident(wa.shape), _resident(wm.shape), _resident(wo.shape)],
        out_specs=tok_spec(),
        out_shape=jax.ShapeDtypeStruct((bsz, s, d), F32),
        scratch_shapes=[pltpu.VMEM((M_HEADS, M_DK, 2 * M_DV), F32),
                        pltpu.VMEM((8, LANES), F32),
                        pltpu.VMEM((M_HEADS, LANES), F32),
                        pltpu.VMEM((ts, d), BF16)],
        compiler_params=pltpu.CompilerParams(dimension_semantics=("arbitrary", "arbitrary"),
                                             vmem_limit_bytes=VMEM_LIMIT_BYTES),
        name="mixer",
    )(x, ada3, q, kt, v, so, sga, sgm, ya, gcol, grow, sel, mh_norm_w, norm_post, wa, wm, wo)


def _argmax_rows(s, exact):
    m = jnp.max(s, axis=0, keepdims=True)
    hit = s == m
    if exact:
        row = lax.broadcasted_iota(jnp.int32, s.shape, 0)
        hit = row == jnp.min(jnp.where(hit, row, s.shape[0]), axis=0, keepdims=True)
    return hit, m


def _topk_rows(arrays, k, exact):
    arrays = list(arrays)
    vals = [[] for _ in arrays]
    if exact:
        ranks = [jnp.full(s.shape, NOT_RANKED, F32) for s in arrays]
        for r in range(k):
            for i, s in enumerate(arrays):
                hit, m = _argmax_rows(s, exact)
                ranks[i] = jnp.where(hit, float(r), ranks[i])
                arrays[i] = jnp.where(hit, -jnp.inf, s)
                vals[i].append(m)
        return vals, ranks
    for r in range(k):
        for i, s in enumerate(arrays):
            hit, m = _argmax_rows(s, exact)
            arrays[i] = jnp.where(hit, -(r + 2.0) * KNOCK, s)
            vals[i].append(m)
    ranks = [jnp.where(s <= -1.5 * KNOCK, s * (-1.0 / KNOCK) - 2.0, NOT_RANKED) for s in arrays]
    return vals, ranks


def _route_tables(s1, s2, exact, topk=None):
    (v1, v2), (rank1, rank2) = topk if topk is not None else _topk_rows((s1, s2), P_TOPK, exact)
    v2_lo = jnp.concatenate(v2[0:8], axis=0)
    v2_all = jnp.concatenate(v2, axis=0)
    v1_hi = jnp.concatenate(v1[8:16], axis=0)
    cand = jnp.concatenate([v1[0] + v2_all] + [v1[j] + v2_lo for j in range(1, 8)] + [v1_hi + v2[0]], axis=0)
    cur = cand
    for _ in range(P_TOPK):
        hit, _ = _argmax_rows(cur, exact)
        cur = jnp.where(hit, -jnp.inf, cur)
    sel = jnp.where(cur == -jnp.inf, 1.0, 0.0)
    ranked = lambda rk: jnp.sum(jnp.where(rk < NOT_RANKED, 1.0, 0.0), axis=0, keepdims=True)
    picked = jnp.maximum(jnp.maximum(ranked(rank1), ranked(rank2)), jnp.sum(sel, axis=0, keepdims=True))
    lowest = jnp.minimum(jnp.min(s1, axis=0, keepdims=True), jnp.min(s2, axis=0, keepdims=True))
    tied = jnp.where((picked > float(P_TOPK)) | (lowest <= -KNOCK), 1.0, 0.0)
    top = v1[0] + v2[0]
    z = jnp.sum(sel * jnp.exp(cand - top), axis=0, keepdims=True)
    counts = [jnp.sum(sel[0:16], axis=0, keepdims=True)]
    counts += [jnp.sum(sel[8 + 8 * j:16 + 8 * j], axis=0, keepdims=True) for j in range(1, 8)]
    counts += [sel[72 + j:73 + j] for j in range(8)]
    cnt = jnp.zeros(s1.shape, F32)
    for j in range(P_TOPK):
        cnt = jnp.where(rank1 == float(j), counts[j], cnt)
    p1 = jnp.exp(s1 - v1[0])
    p2 = jnp.exp(s2 - v2[0]) * (1.0 / z)
    return cnt, rank2, p1, p2, tied


def _bf16_pair_words(x):
    hi = pltpu.bitcast(x.astype(BF16).astype(F32), jnp.uint32)
    return hi | lax.shift_right_logical(hi, jnp.uint32(16))


def _route_kernel(x_ref, ada_ref, npre_ref, wqt_ref, sk_ref, h2t_ref, cnt_ref, rk2_ref, p1_ref, p2_ref, qt_buf,
                  *, tm):
    ada = ada_ref[0]
    h2 = _rms_norm(x_ref[...], npre_ref[...]) * (1.0 + ada[4:5, :]) + ada[3:4, :]
    h2t = h2.T.astype(BF16)
    h2t_ref[...] = _packed_words(h2t)
    qt_buf[...] = jnp.dot(wqt_ref[...], h2t, preferred_element_type=F32).astype(BF16)

    def head(hd, carry):
        r0 = pl.multiple_of(hd * (2 * N_KEYS), 2 * N_KEYS)
        s1 = jnp.dot(sk_ref[hd, 0], qt_buf[pl.ds(r0, N_KEYS), :], preferred_element_type=F32)
        s2 = jnp.dot(sk_ref[hd, 1], qt_buf[pl.ds(r0 + N_KEYS, N_KEYS), :], preferred_element_type=F32)

        def tables(lc, exact, topk=None):
            sl = slice(lc * LANES, (lc + 1) * LANES)
            cnt, rk2, p1, p2, tied = _route_tables(s1[:, sl], s2[:, sl], exact, topk)
            cnt_ref[hd, lc] = _bf16_pair_words(cnt)
            rk2_ref[hd, lc] = _packed_words(rk2.astype(BF16))
            p1_ref[hd, lc] = _bf16_pair_words(p1)
            p2_ref[hd, lc] = _packed_words(p2.astype(BF16))
            return tied

        nlc = tm // LANES
        slabs = [s[:, lc * LANES:(lc + 1) * LANES] for lc in range(nlc) for s in (s1, s2)]
        vals, ranks = _topk_rows(slabs, P_TOPK, False)
        for lc in range(nlc):
            tied = tables(lc, False, ((vals[2 * lc], vals[2 * lc + 1]), (ranks[2 * lc], ranks[2 * lc + 1])))

            @pl.when(jnp.max(tied) > 0.0)
            def _():
                tables(lc, exact=True)
        return carry

    lax.fori_loop(0, P_HEADS, head, 0)


def _route(x1, ada3, norm_pre, wqt, subkeys):
    t, d = x1.shape
    s = t // ada3.shape[0]
    tm = min(TM_ROUTE, s)
    tab = lambda rows: jax.ShapeDtypeStruct((P_HEADS, t // LANES, rows, LANES), jnp.uint32)
    tab_spec = lambda rows: pl.BlockSpec((P_HEADS, tm // LANES, rows, LANES), lambda i: (0, i, 0, 0))
    k1, k2 = N_KEYS, N_KEYS // 2
    return pl.pallas_call(
        functools.partial(_route_kernel, tm=tm),
        grid=(t // tm,),
        in_specs=[pl.BlockSpec((tm, d), lambda i: (i, 0)),
                  pl.BlockSpec((1, 6, d), lambda i: ((i * tm) // s, 0, 0)),
                  _resident((1, d)),
                  _resident(wqt.shape),
                  _resident(subkeys.shape)],
        out_specs=[pl.BlockSpec((d // 2, tm), lambda i: (0, i)),
                   tab_spec(k1), tab_spec(k2), tab_spec(k1), tab_spec(k2)],
        out_shape=[jax.ShapeDtypeStruct((d // 2, t), jnp.uint32), tab(k1), tab(k2), tab(k1), tab(k2)],
        scratch_shapes=[pltpu.VMEM((wqt.shape[0], tm), BF16)],
        compiler_params=pltpu.CompilerParams(dimension_semantics=("arbitrary",),
                                             vmem_limit_bytes=VMEM_LIMIT_BYTES),
        name="route",
    )(x1, ada3, norm_pre, wqt, subkeys)


def _expert_words_kernel(u_ref, v_ref, uw_ref, vtw_ref):
    uw_ref[...] = _packed_words(u_ref[...].astype(BF16))
    vtw_ref[...] = _packed_words(v_ref[...].T.astype(BF16))


def _expert_words(u, v):
    ne, d = u.shape
    te = min(TE_PEER, ne)
    return pl.pallas_call(
        _expert_words_kernel,
        grid=(ne // te,),
        in_specs=[pl.BlockSpec((te, d), lambda j: (j, 0)), pl.BlockSpec((te, d), lambda j: (j, 0))],
        out_specs=[pl.BlockSpec((te // 2, d), lambda j: (j, 0)), pl.BlockSpec((d // 2, te), lambda j: (0, j))],
        out_shape=[jax.ShapeDtypeStruct((ne // 2, d), jnp.uint32), jax.ShapeDtypeStruct((d // 2, ne), jnp.uint32)],
        compiler_params=pltpu.CompilerParams(vmem_limit_bytes=VMEM_LIMIT_BYTES),
        name="expert_words",
    )(u, v)


def _gelu_tanh(z):
    c = 0.7978845608028654
    half = 0.5 * z
    return half + half * jnp.tanh(z * (c + (c * 0.044715) * (z * z)))


def _peer_kernel(h2t_ref, u_ref, vt_ref, cnt_ref, rk2_ref, p1_ref, p2_ref, x_ref, ada_ref, npost_ref, o_ref,
                 acc, z_buf, a_buf, *, tm, te):
    j = pl.program_id(1)

    @pl.when(j == 0)
    def _():
        acc[...] = jnp.zeros(acc.shape, F32)

    z_buf[...] = jnp.dot(_unpacked(u_ref[...]), _unpacked(h2t_ref[...]), preferred_element_type=F32).astype(BF16)

    pk = (N_KEYS // BF16_ROWS, BF16_ROWS, LANES)
    for lc in range(tm // LANES):
        sl = slice(lc * LANES, (lc + 1) * LANES)
        for a0 in range(0, te // N_KEYS, PEER_GROUP):
            gates = [None] * PEER_GROUP
            for hd in range(P_HEADS):
                rk2 = _unpacked(rk2_ref[hd, lc]).reshape(pk)
                p2 = _unpacked(p2_ref[hd, lc]).reshape(pk)
                for g in range(PEER_GROUP):
                    al = a0 + g
                    cnt_a = pltpu.bitcast(jnp.broadcast_to(cnt_ref[hd, lc, al:al + 1, :], (8, LANES)), BF16)
                    p1_a = pltpu.bitcast(jnp.broadcast_to(p1_ref[hd, lc, al:al + 1, :], (8, LANES)), BF16)
                    term = jnp.where(rk2 < cnt_a[None], p2 * p1_a[None], jnp.zeros((), BF16))
                    gates[g] = term if gates[g] is None else gates[g] + term
            for g in range(PEER_GROUP):
                rows = slice((a0 + g) * N_KEYS, (a0 + g + 1) * N_KEYS)
                a_buf[rows, sl] = _gelu_tanh(z_buf[rows, sl]) * gates[g].reshape(N_KEYS, LANES)

    acc[...] += jnp.dot(_unpacked(vt_ref[...]), a_buf[...], preferred_element_type=F32)

    @pl.when(j == pl.num_programs(1) - 1)
    def _():
        y = acc[...].T
        o_ref[...] = x_ref[...] + ada_ref[0][5:6, :] * _rms_norm(y, npost_ref[...])


def _peer(h2t, u_bf, vt_bf, cnt, rk2, p1, p2, x1, ada3, norm_post):
    t, d = x1.shape
    s = t // ada3.shape[0]
    ne = 2 * u_bf.shape[0]
    tm = min(TM_PEER, s)
    te = TE_PEER
    tab2_spec = pl.BlockSpec((P_HEADS, tm // LANES, N_KEYS // 2, LANES), lambda i, j: (0, i, 0, 0))
    tab1_spec = pl.BlockSpec((P_HEADS, tm // LANES, te // N_KEYS, LANES), lambda i, j: (0, i, j, 0))
    return pl.pallas_call(
        functools.partial(_peer_kernel, tm=tm, te=te),
        grid=(t // tm, ne // te),
        in_specs=[pl.BlockSpec((d // 2, tm), lambda i, j: (0, i)),
                  pl.BlockSpec((te // 2, d), lambda i, j: (j, 0)),
                  pl.BlockSpec((d // 2, te), lambda i, j: (0, j)),
                  tab1_spec, tab2_spec, tab1_spec, tab2_spec,
                  pl.BlockSpec((tm, d), lambda i, j: (i, 0)),
                  pl.BlockSpec((1, 6, d), lambda i, j: ((i * tm) // s, 0, 0)),
                  pl.BlockSpec((1, d), lambda i, j: (0, 0))],
        out_specs=pl.BlockSpec((tm, d), lambda i, j: (i, 0)),
        out_shape=jax.ShapeDtypeStruct((t, d), F32),
        scratch_shapes=[pltpu.VMEM((d, tm), F32),
                        pltpu.VMEM((te, tm), BF16),
                        pltpu.VMEM((te, tm), BF16)],
        compiler_params=pltpu.CompilerParams(dimension_semantics=("arbitrary", "arbitrary"),
                                             vmem_limit_bytes=VMEM_LIMIT_BYTES),
        name="peer",
    )(h2t, u_bf, vt_bf, cnt, rk2, p1, p2, x1, ada3, norm_post)


def _layer(x, c, w_ada, b_ada, norm1_pre, norm1_post, w_in, conv_a_w, conv_qk_w, b_igate, b_fgate, mh_norm_w,
           w_branch_a, w_branch_m, w_out, norm2_pre, norm2_post, peer_wq, peer_subkeys, peer_u, peer_v):
    bsz, s, d = x.shape
    ng = 2 * M_HEADS
    ada3 = _ada(c, w_ada, b_ada).reshape(bsz, 6, d)

    o_if = 3 * d + 2 * M_HEADS * M_DK + 2 * d
    w_main = jnp.concatenate([w_in[:, :o_if], w_in[:, o_if + ng:]], axis=1).astype(BF16)
    w_if = w_in[:, o_if:o_if + ng]
    wif_col = jnp.pad(w_if, ((0, 0), (0, LANES - ng))).astype(BF16)
    wif_row = w_if.T.astype(BF16)
    b_if = jnp.concatenate([b_igate, b_fgate])
    bif_col = jnp.pad(b_if, (0, LANES - ng)).reshape(1, LANES)
    bif_row = b_if.reshape(ng, 1)

    ya, q, kt, v, so, sga, sgm, gcol, grow = _inproj(
        x, ada3, norm1_pre.reshape(1, d), w_main, wif_col, wif_row, bif_col, bif_row, conv_a_w, conv_qk_w)
    x1 = _mixer(x, ada3, q, kt, v, so, sga, sgm, ya, gcol, grow, mh_norm_w.reshape(1, d),
                norm1_post.reshape(1, d), w_branch_a.astype(BF16), w_branch_m.astype(BF16), w_out.astype(BF16))

    x1f = x1.reshape(bsz * s, d)
    h2t, cnt, rk2, p1, p2 = _route(x1f, ada3, norm2_pre.reshape(1, d), peer_wq.T.astype(BF16),
                                   peer_subkeys.astype(BF16))
    u_words, vt_words = _expert_words(peer_u, peer_v)
    out = _peer(h2t, u_words, vt_words, cnt, rk2, p1, p2, x1f, ada3, norm2_post.reshape(1, d))
    return out.reshape(bsz, s, d)


def kernel(x, c, w_ada, b_ada, norm1_pre, norm1_post, w_in, conv_a_w, conv_qk_w, b_igate, b_fgate, mh_norm_w, w_branch_a, w_branch_m, w_out, norm2_pre, norm2_post, peer_wq, peer_subkeys, peer_u, peer_v):
    for l in range(w_ada.shape[0]):
        x = _layer(x, c, w_ada[l], b_ada[l], norm1_pre[l], norm1_post[l], w_in[l], conv_a_w[l], conv_qk_w[l],
                   b_igate[l], b_fgate[l], mh_norm_w[l], w_branch_a[l], w_branch_m[l], w_out[l], norm2_pre[l],
                   norm2_post[l], peer_wq[l], peer_subkeys[l], peer_u[l], peer_v[l])
    return x
```

```python
import functools

import jax
import jax.numpy as jnp
from jax import lax
from jax.experimental import pallas as pl
from jax.experimental.pallas import tpu as pltpu

F32 = jnp.float32
BF16 = jnp.bfloat16

EPS = 1e-6
M_HEADS = 8
M_DK = 64
M_DV = 128
CONV_K = 3
QK_CONV_K = 4
P_HEADS = 8
N_KEYS = 128
P_TOPK = 16
HALO = 8
LANES = 128
BF16_ROWS = 16
NOT_RANKED = 99.0
KNOCK = 2.0 ** 100

VMEM_LIMIT_BYTES = 56 * 1024 * 1024

MLSTM_CHUNK = 128
TS_INPROJ = 256
TS_MIXER = 512
TM_ROUTE = 256
TM_PEER = 1024
TE_PEER = 1024
PEER_GROUP = 2
TE_WORDS = 1024


def _resident(shape):
    nd = len(shape)
    return pl.BlockSpec(shape, lambda *_: (0,) * nd, pipeline_mode=pl.Buffered(1))


def _rms_norm(x, w):
    return x * lax.rsqrt(jnp.mean(x * x, axis=-1, keepdims=True) + EPS) * w


def _sigmoid(x):
    return 1.0 / (1.0 + jnp.exp(-x))


def _packed_words(x):
    return pltpu.bitcast(x, jnp.uint32)


def _unpacked(words):
    return pltpu.bitcast(words, BF16)


def _ada_kernel(c_ref, w_ref, b_ref, o_ref):
    c = c_ref[...]
    s = c * _sigmoid(c)
    o_ref[...] = jnp.dot(s.astype(BF16), w_ref[...].astype(BF16), preferred_element_type=F32) + b_ref[...]


def _ada(c, w_ada, b_ada):
    bsz, d = c.shape
    n = w_ada.shape[1]
    return pl.pallas_call(
        _ada_kernel,
        grid=(n // d,),
        in_specs=[pl.BlockSpec((bsz, d), lambda j: (0, 0)),
                  pl.BlockSpec((d, d), lambda j: (0, j)),
                  pl.BlockSpec((1, d), lambda j: (0, j))],
        out_specs=pl.BlockSpec((bsz, d), lambda j: (0, j)),
        out_shape=jax.ShapeDtypeStruct((bsz, n), F32),
        compiler_params=pltpu.CompilerParams(vmem_limit_bytes=VMEM_LIMIT_BYTES),
        name="ada",
    )(c, w_ada, b_ada.reshape(1, n))


def _inproj_kernel(x_ref, ada_ref, npre_ref, w_ref, wifc_ref, wifr_ref, bifc_ref, bifr_ref, cwa_ref, cwqk_ref,
                   ya_ref, q_ref, kt_ref, v_ref, so_ref, sga_ref, sgm_ref, gcol_ref, grow_ref,
                   ubuf, qkbuf, *, ts, d):
    @pl.when(pl.program_id(1) == 0)
    def _():
        ubuf[0:HALO, :] = jnp.zeros((HALO, d), F32)
        qkbuf[0:HALO, :] = jnp.zeros((HALO, d), F32)

    x = x_ref[0]
    ada = ada_ref[0]
    h = _rms_norm(x, npre_ref[...]) * (1.0 + ada[1:2, :]) + ada[0:1, :]
    hb = h.astype(BF16)

    def proj(k):
        return jnp.dot(hb, w_ref[:, k * d:(k + 1) * d], preferred_element_type=F32)

    u = proj(0) * proj(2)
    ubuf[HALO:HALO + ts, :] = u
    conv = u * cwa_ref[CONV_K - 1:CONV_K, :]
    for j in range(CONV_K - 1):
        off = HALO - (CONV_K - 1) + j
        conv = conv + ubuf[off:off + ts, :] * cwa_ref[j:j + 1, :]
    ya_ref[0] = _packed_words((proj(1) * conv).astype(BF16))
    ubuf[0:HALO, :] = ubuf[ts:ts + HALO, :]

    qk = proj(3)
    qkbuf[HALO:HALO + ts, :] = qk
    cq = qk * cwqk_ref[QK_CONV_K - 1:QK_CONV_K, :]
    for j in range(QK_CONV_K - 1):
        off = HALO - (QK_CONV_K - 1) + j
        cq = cq + qkbuf[off:off + ts, :] * cwqk_ref[j:j + 1, :]
    qkbuf[0:HALO, :] = qkbuf[ts:ts + HALO, :]
    cq = cq * _sigmoid(cq)
    nqk = M_HEADS * M_DK
    q_ref[0] = _packed_words((cq[:, :nqk] * (M_DK ** -0.5)).astype(BF16))
    kt_ref[0] = _packed_words(cq[:, nqk:].T.astype(BF16))

    v_ref[0] = _packed_words(proj(4).astype(BF16))
    so_ref[0] = _sigmoid(proj(5)).astype(BF16)
    sga_ref[0] = _sigmoid(proj(6)).astype(BF16)
    sgm_ref[0] = _sigmoid(proj(7)).astype(BF16)

    gcol_ref[0] = jnp.dot(hb, wifc_ref[...], preferred_element_type=F32) + bifc_ref[...]
    grow_ref[0] = lax.dot_general(wifr_ref[...], hb, (((1,), (1,)), ((), ())),
                                  preferred_element_type=F32) + bifr_ref[...]


def _inproj(x, ada3, norm_pre, w_main, wif_col, wif_row, bif_col, bif_row, conv_a_w, conv_qk_w):
    bsz, s, d = x.shape
    ts = min(TS_INPROJ, s)
    ng = 2 * M_HEADS
    nqk = M_HEADS * M_DK
    tok = lambda dt, w=d: jax.ShapeDtypeStruct((bsz, s, w), dt)
    tok_spec = lambda w=d: pl.BlockSpec((1, ts, w), lambda b, i: (b, i, 0))
    words = lambda w=d: jax.ShapeDtypeStruct((bsz, s // 2, w), jnp.uint32)
    words_spec = lambda w=d: pl.BlockSpec((1, ts // 2, w), lambda b, i: (b, i, 0))
    return pl.pallas_call(
        functools.partial(_inproj_kernel, ts=ts, d=d),
        grid=(bsz, s // ts),
        in_specs=[tok_spec(),
                  pl.BlockSpec((1, 6, d), lambda b, i: (b, 0, 0)),
                  _resident((1, d)),
                  _resident(w_main.shape),
                  _resident(wif_col.shape),
                  _resident(wif_row.shape),
                  _resident(bif_col.shape),
                  _resident(bif_row.shape),
                  _resident(conv_a_w.shape),
                  _resident(conv_qk_w.shape)],
        out_specs=[words_spec(), words_spec(nqk),
                   pl.BlockSpec((1, nqk // 2, ts), lambda b, i: (b, 0, i)),
                   words_spec(), tok_spec(), tok_spec(), tok_spec(),
                   tok_spec(LANES),
                   pl.BlockSpec((1, ng, ts), lambda b, i: (b, 0, i))],
        out_shape=[words(), words(nqk),
                   jax.ShapeDtypeStruct((bsz, nqk // 2, s), jnp.uint32),
                   words(), tok(BF16), tok(BF16), tok(BF16),
                   tok(F32, LANES),
                   jax.ShapeDtypeStruct((bsz, ng, s), F32)],
        scratch_shapes=[pltpu.VMEM((HALO + ts, d), F32), pltpu.VMEM((HALO + ts, d), F32)],
        compiler_params=pltpu.CompilerParams(dimension_semantics=("arbitrary", "arbitrary"),
                                             vmem_limit_bytes=VMEM_LIMIT_BYTES),
        name="inproj",
    )(x, ada3, norm_pre, w_main, wif_col, wif_row, bif_col, bif_row, conv_a_w, conv_qk_w)


def _log_sigmoid(x):
    return jnp.minimum(x, 0.0) - jnp.log(1.0 + jnp.exp(-jnp.abs(x)))


def _chunk_scan(x, axis, chunk, op, identity):
    pos = lax.broadcasted_iota(jnp.int32, x.shape, axis) % chunk
    k = 1
    while k < chunk:
        x = op(x, jnp.where(pos >= k, pltpu.roll(x, k, axis), identity))
        k *= 2
    return x


def _bf16_terms(x):
    hi = x.astype(BF16)
    r = x - hi.astype(F32)
    mid = r.astype(BF16)
    lo = (r - mid.astype(F32)).astype(BF16)
    return hi, mid, lo


def _spread_heads(x, sel):
    return sum(jnp.dot(t, sel, preferred_element_type=F32) for t in _bf16_terms(x))


def _mixer_kernel(x_ref, ada_ref, q_ref, kt_ref, v_ref, so_ref, sga_ref, sgm_ref, ya_ref, gcol_ref, grow_ref,
                  sel_ref, mhw_ref, npost_ref, wa_ref, wm_ref, wo_ref, o_ref,
                  c_st, mcol_st, mrow_st, ym_buf, *, ts, chunk):
    first = pl.program_id(1) == 0

    @pl.when(first)
    def _():
        c_st[...] = jnp.zeros(c_st.shape, F32)
        mcol_st[...] = jnp.zeros(mcol_st.shape, F32)
        mrow_st[...] = jnp.zeros(mrow_st.shape, F32)

    nc = ts // chunk
    H = M_HEADS
    neg_inf = -jnp.inf

    gcol = gcol_ref[0]
    b_col = _chunk_scan(_log_sigmoid(gcol), 0, chunk, jnp.add, 0.0)
    r_col = pltpu.roll(gcol, H, 1) - b_col
    cm_col = _chunk_scan(r_col, 0, chunk, jnp.maximum, neg_inf)
    m = mcol_st[0:1, :]
    m_rows, decay_rows = [], []
    for c in range(nc):
        last = (c + 1) * chunk - 1
        b_l, cm_l = b_col[last:last + 1, :], cm_col[last:last + 1, :]
        m_next = b_l + jnp.maximum(m, cm_l)
        m_rows.append(m)
        decay_rows.append(jnp.exp(b_l + m - m_next))
        m = m_next
    mcol_st[...] = jnp.broadcast_to(m, mcol_st.shape)
    pad = jnp.zeros((8 - (2 * nc) % 8, LANES), F32)
    spread = _spread_heads(jnp.concatenate([b_col, cm_col] + m_rows + decay_rows + [pad], axis=0), sel_ref[...])
    bc_all, cm_all, sc_all = spread[0:ts], spread[ts:2 * ts], spread[2 * ts:]

    grow = grow_ref[0]
    i_row = grow[0:H, :]
    b_row = _chunk_scan(_log_sigmoid(grow[H:2 * H, :]), 1, chunk, jnp.add, 0.0)
    r_row = i_row - b_row
    cm_row = _chunk_scan(r_row, 1, chunk, jnp.maximum, neg_inf)
    mr = mrow_st[:, 0:1]
    ws_rows = []
    for c in range(nc):
        last = (c + 1) * chunk - 1
        b_l, cm_l = b_row[:, last:last + 1], cm_row[:, last:last + 1]
        m_next = b_l + jnp.maximum(mr, cm_l)
        ws_rows.append(jnp.exp(b_l + r_row[:, c * chunk:(c + 1) * chunk] - m_next))
        mr = m_next
    mrow_st[...] = jnp.broadcast_to(mr, mrow_st.shape)

    tril = (lax.broadcasted_iota(jnp.int32, (chunk, chunk), 0)
            >= lax.broadcasted_iota(jnp.int32, (chunk, chunk), 1))
    ones = jnp.ones((chunk, M_DV), BF16)

    for hd in range(H):
        hl = slice(hd * M_DV, (hd + 1) * M_DV)
        c_aug = c_st[hd]
        for c in range(nc):
            rows = slice(c * chunk, (c + 1) * chunk)
            wrows = slice(c * chunk // 2, (c + 1) * chunk // 2)
            q = _unpacked(q_ref[0, wrows, hd * M_DK:(hd + 1) * M_DK])
            kt = _unpacked(kt_ref[0, hd * M_DK // 2:(hd + 1) * M_DK // 2, rows])
            v = _unpacked(v_ref[0, wrows, hl])
            bc = bc_all[rows, hl]
            m_prev = sc_all[c:c + 1, hl]
            decay = sc_all[nc + c:nc + c + 1, hl]
            m_t = bc + jnp.maximum(m_prev, cm_all[rows, hl])
            dlog = jnp.where(tril, bc[:, :chunk] + r_row[hd:hd + 1, rows], neg_inf)
            s_qk = jnp.dot(q, kt, preferred_element_type=F32) * jnp.exp(dlog - m_t[:, :chunk])
            a_inter = jnp.exp(bc + m_prev - m_t)
            qc = jnp.dot(q, c_aug.astype(BF16), preferred_element_type=F32)
            num = jnp.dot(s_qk.astype(BF16), v, preferred_element_type=F32) + a_inter * qc[:, :M_DV]
            den = jnp.sum(s_qk, axis=1, keepdims=True) + a_inter * qc[:, M_DV:]
            h = num / jnp.maximum(jnp.abs(den), jnp.exp(-m_t))
            hn = _rms_norm(h, mhw_ref[:, hl])
            ym_buf[rows, hl] = (so_ref[0, rows, hl].astype(F32) * hn).astype(BF16)
            ktw = (kt.astype(F32) * ws_rows[c][hd:hd + 1, :]).astype(BF16)
            upd = jnp.dot(ktw, jnp.concatenate([v, ones], axis=1), preferred_element_type=F32)
            c_aug = jnp.concatenate([decay, decay], axis=1) * c_aug + upd
        c_st[hd] = c_aug

    mix = (sga_ref[0].astype(F32) * jnp.dot(_unpacked(ya_ref[0]), wa_ref[...], preferred_element_type=F32)
           + sgm_ref[0].astype(F32) * jnp.dot(ym_buf[...], wm_ref[...], preferred_element_type=F32))
    y = jnp.dot(mix.astype(BF16), wo_ref[...], preferred_element_type=F32)
    o_ref[0] = x_ref[0] + ada_ref[0][2:3, :] * _rms_norm(y, npost_ref[...])


def _mixer(x, ada3, q, kt, v, so, sga, sgm, ya, gcol, grow, mh_norm_w, norm_post, wa, wm, wo):
    bsz, s, d = x.shape
    ts = min(TS_MIXER, s)
    chunk = min(MLSTM_CHUNK, ts)
    ng = 2 * M_HEADS
    nqk = M_HEADS * M_DK
    sel = (jnp.arange(LANES)[:, None] - M_HEADS == jnp.arange(M_HEADS * M_DV)[None, :] // M_DV).astype(BF16)
    tok_spec = lambda w=d: pl.BlockSpec((1, ts, w), lambda b, i: (b, i, 0))
    words_spec = lambda w=d: pl.BlockSpec((1, ts // 2, w), lambda b, i: (b, i, 0))
    return pl.pallas_call(
        functools.partial(_mixer_kernel, ts=ts, chunk=chunk),
        grid=(bsz, s // ts),
        in_specs=[tok_spec(),
                  pl.BlockSpec((1, 6, d), lambda b, i: (b, 0, 0)),
                  words_spec(nqk),
                  pl.BlockSpec((1, nqk // 2, ts), lambda b, i: (b, 0, i)),
                  words_spec(), tok_spec(), tok_spec(), tok_spec(), words_spec(),
                  tok_spec(LANES),
                  pl.BlockSpec((1, ng, ts), lambda b, i: (b, 0, i)),
                  _resident(sel.shape),
                  _resident((1, d)), _resident((1, d)),
                  _resident(wa.shape), _resident(wm.shape), _resident(wo.shape)],
        out_specs=tok_spec(),
        out_shape=jax.ShapeDtypeStruct((bsz, s, d), F32),
        scratch_shapes=[pltpu.VMEM((M_HEADS, M_DK, 2 * M_DV), F32),
                        pltpu.VMEM((8, LANES), F32),
                        pltpu.VMEM((M_HEADS, LANES), F32),
                        pltpu.VMEM((ts, d), BF16)],
        compiler_params=pltpu.CompilerParams(dimension_semantics=("arbitrary", "arbitrary"),
                                             vmem_limit_bytes=VMEM_LIMIT_BYTES),
        name="mixer",
    )(x, ada3, q, kt, v, so, sga, sgm, ya, gcol, grow, sel, mh_norm_w, norm_post, wa, wm, wo)


def _argmax_rows(s, exact):
    m = jnp.max(s, axis=0, keepdims=True)
    hit = s == m
    if exact:
        row = lax.broadcasted_iota(jnp.int32, s.shape, 0)
        hit = row == jnp.min(jnp.where(hit, row, s.shape[0]), axis=0, keepdims=True)
    return hit, m


def _topk_rows(arrays, k, exact):
    arrays = list(arrays)
    vals = [[] for _ in arrays]
    if exact:
        ranks = [jnp.full(s.shape, NOT_RANKED, F32) for s in arrays]
        for r in range(k):
            for i, s in enumerate(arrays):
                hit, m = _argmax_rows(s, exact)
                ranks[i] = jnp.where(hit, float(r), ranks[i])
                arrays[i] = jnp.where(hit, -jnp.inf, s)
                vals[i].append(m)
        return vals, ranks
    for r in range(k):
        for i, s in enumerate(arrays):
            hit, m = _argmax_rows(s, exact)
            arrays[i] = jnp.where(hit, -(r + 2.0) * KNOCK, s)
            vals[i].append(m)
    ranks = [jnp.where(s <= -1.5 * KNOCK, s * (-1.0 / KNOCK) - 2.0, NOT_RANKED) for s in arrays]
    return vals, ranks


def _route_tables(s1, s2, exact, topk=None):
    (v1, v2), (rank1, rank2) = topk if topk is not None else _topk_rows((s1, s2), P_TOPK, exact)
    v2_lo = jnp.concatenate(v2[0:8], axis=0)
    v2_all = jnp.concatenate(v2, axis=0)
    v1_hi = jnp.concatenate(v1[8:16], axis=0)
    cand = jnp.concatenate([v1[0] + v2_all] + [v1[j] + v2_lo for j in range(1, 8)] + [v1_hi + v2[0]], axis=0)
    cur = cand
    for _ in range(P_TOPK):
        hit, _ = _argmax_rows(cur, exact)
        cur = jnp.where(hit, -jnp.inf, cur)
    sel = jnp.where(cur == -jnp.inf, 1.0, 0.0)
    ranked = lambda rk: jnp.sum(jnp.where(rk < NOT_RANKED, 1.0, 0.0), axis=0, keepdims=True)
    picked = jnp.maximum(jnp.maximum(ranked(rank1), ranked(rank2)), jnp.sum(sel, axis=0, keepdims=True))
    lowest = jnp.minimum(jnp.min(s1, axis=0, keepdims=True), jnp.min(s2, axis=0, keepdims=True))
    tied = jnp.where((picked > float(P_TOPK)) | (lowest <= -KNOCK), 1.0, 0.0)
    top = v1[0] + v2[0]
    z = jnp.sum(sel * jnp.exp(cand - top), axis=0, keepdims=True)
    counts = [jnp.sum(sel[0:16], axis=0, keepdims=True)]
    counts += [jnp.sum(sel[8 + 8 * j:16 + 8 * j], axis=0, keepdims=True) for j in range(1, 8)]
    counts += [sel[72 + j:73 + j] for j in range(8)]
    cnt = jnp.zeros(s1.shape, F32)
    for j in range(P_TOPK):
        cnt = jnp.where(rank1 == float(j), counts[j], cnt)
    p1 = jnp.exp(s1 - v1[0])
    p2 = jnp.exp(s2 - v2[0]) * (1.0 / z)
    return cnt, rank2, p1, p2, tied


def _bf16_pair_words(x):
    hi = pltpu.bitcast(x.astype(BF16).astype(F32), jnp.uint32)
    return hi | lax.shift_right_logical(hi, jnp.uint32(16))


def _route_kernel(x_ref, ada_ref, npre_ref, wqt_ref, sk_ref, h2t_ref, cnt_ref, rk2_ref, p1_ref, p2_ref, qt_buf,
                  *, tm):
    ada = ada_ref[0]
    h2 = _rms_norm(x_ref[...], npre_ref[...]) * (1.0 + ada[4:5, :]) + ada[3:4, :]
    h2t = h2.T.astype(BF16)
    h2t_ref[...] = _packed_words(h2t)
    qt_buf[...] = jnp.dot(wqt_ref[...], h2t, preferred_element_type=F32).astype(BF16)

    def head(hd, carry):
        r0 = pl.multiple_of(hd * (2 * N_KEYS), 2 * N_KEYS)
        s1 = jnp.dot(sk_ref[hd, 0], qt_buf[pl.ds(r0, N_KEYS), :], preferred_element_type=F32)
        s2 = jnp.dot(sk_ref[hd, 1], qt_buf[pl.ds(r0 + N_KEYS, N_KEYS), :], preferred_element_type=F32)

        def tables(lc, exact, topk=None):
            sl = slice(lc * LANES, (lc + 1) * LANES)
            cnt, rk2, p1, p2, tied = _route_tables(s1[:, sl], s2[:, sl], exact, topk)
            cnt_ref[hd, lc] = _bf16_pair_words(cnt)
            rk2_ref[hd, lc] = _packed_words(rk2.astype(BF16))
            p1_ref[hd, lc] = _bf16_pair_words(p1)
            p2_ref[hd, lc] = _packed_words(p2.astype(BF16))
            return tied

        nlc = tm // LANES
        slabs = [s[:, lc * LANES:(lc + 1) * LANES] for lc in range(nlc) for s in (s1, s2)]
        vals, ranks = _topk_rows(slabs, P_TOPK, False)
        for lc in range(nlc):
            tied = tables(lc, False, ((vals[2 * lc], vals[2 * lc + 1]), (ranks[2 * lc], ranks[2 * lc + 1])))

            @pl.when(jnp.max(tied) > 0.0)
            def _():
                tables(lc, exact=True)
        return carry

    lax.fori_loop(0, P_HEADS, head, 0)


def _route(x1, ada3, norm_pre, wqt, subkeys):
    t, d = x1.shape
    s = t // ada3.shape[0]
    tm = min(TM_ROUTE, s)
    tab = lambda rows: jax.ShapeDtypeStruct((P_HEADS, t // LANES, rows, LANES), jnp.uint32)
    tab_spec = lambda rows: pl.BlockSpec((P_HEADS, tm // LANES, rows, LANES), lambda i: (0, i, 0, 0))
    k1, k2 = N_KEYS, N_KEYS // 2
    return pl.pallas_call(
        functools.partial(_route_kernel, tm=tm),
        grid=(t // tm,),
        in_specs=[pl.BlockSpec((tm, d), lambda i: (i, 0)),
                  pl.BlockSpec((1, 6, d), lambda i: ((i * tm) // s, 0, 0)),
                  _resident((1, d)),
                  _resident(wqt.shape),
                  _resident(subkeys.shape)],
        out_specs=[pl.BlockSpec((d // 2, tm), lambda i: (0, i)),
                   tab_spec(k1), tab_spec(k2), tab_spec(k1), tab_spec(k2)],
        out_shape=[jax.ShapeDtypeStruct((d // 2, t), jnp.uint32), tab(k1), tab(k2), tab(k1), tab(k2)],
        scratch_shapes=[pltpu.VMEM((wqt.shape[0], tm), BF16)],
        compiler_params=pltpu.CompilerParams(dimension_semantics=("arbitrary",),
                                             vmem_limit_bytes=VMEM_LIMIT_BYTES),
        name="route",
    )(x1, ada3, norm_pre, wqt, subkeys)


def _expert_words_kernel(u_ref, v_ref, uw_ref, vtw_ref):
    uw_ref[...] = _packed_words(u_ref[...].astype(BF16))
    vtw_ref[...] = _packed_words(v_ref[...].T.astype(BF16))


def _expert_words(u, v):
    ne, d = u.shape
    te = min(TE_WORDS, ne)
    return pl.pallas_call(
        _expert_words_kernel,
        grid=(ne // te,),
        in_specs=[pl.BlockSpec((te, d), lambda j: (j, 0)), pl.BlockSpec((te, d), lambda j: (j, 0))],
        out_specs=[pl.BlockSpec((te // 2, d), lambda j: (j, 0)), pl.BlockSpec((d // 2, te), lambda j: (0, j))],
        out_shape=[jax.ShapeDtypeStruct((ne // 2, d), jnp.uint32), jax.ShapeDtypeStruct((d // 2, ne), jnp.uint32)],
        compiler_params=pltpu.CompilerParams(vmem_limit_bytes=VMEM_LIMIT_BYTES),
        name="expert_words",
    )(u, v)


def _gelu_tanh(z):
    c = 0.7978845608028654
    half = 0.5 * z
    return half + half * jnp.tanh(z * (c + (c * 0.044715) * (z * z)))


def _peer_kernel(h2t_ref, u_ref, vt_ref, cnt_ref, rk2_ref, p1_ref, p2_ref, x_ref, ada_ref, npost_ref, o_ref,
                 acc, z_buf, a_buf, *, tm, te):
    j = pl.program_id(1)

    @pl.when(j == 0)
    def _():
        acc[...] = jnp.zeros(acc.shape, F32)

    z_buf[...] = jnp.dot(_unpacked(u_ref[...]), _unpacked(h2t_ref[...]), preferred_element_type=F32).astype(BF16)

    pk = (N_KEYS // BF16_ROWS, BF16_ROWS, LANES)
    for lc in range(tm // LANES):
        sl = slice(lc * LANES, (lc + 1) * LANES)
        for a0 in range(0, te // N_KEYS, PEER_GROUP):
            gates = [None] * PEER_GROUP
            for hd in range(P_HEADS):
                rk2 = _unpacked(rk2_ref[hd, lc]).reshape(pk)
                p2 = _unpacked(p2_ref[hd, lc]).reshape(pk)
                for g in range(PEER_GROUP):
                    al = a0 + g
                    cnt_a = pltpu.bitcast(jnp.broadcast_to(cnt_ref[hd, lc, al:al + 1, :], (8, LANES)), BF16)
                    p1_a = pltpu.bitcast(jnp.broadcast_to(p1_ref[hd, lc, al:al + 1, :], (8, LANES)), BF16)
                    term = jnp.where(rk2 < cnt_a[None], p2 * p1_a[None], jnp.zeros((), BF16))
                    gates[g] = term if gates[g] is None else gates[g] + term
            for g in range(PEER_GROUP):
                rows = slice((a0 + g) * N_KEYS, (a0 + g + 1) * N_KEYS)
                a_buf[rows, sl] = _gelu_tanh(z_buf[rows, sl]) * gates[g].reshape(N_KEYS, LANES)

    acc[...] += jnp.dot(_unpacked(vt_ref[...]), a_buf[...], preferred_element_type=F32)

    @pl.when(j == pl.num_programs(1) - 1)
    def _():
        y = acc[...].T
        o_ref[...] = x_ref[...] + ada_ref[0][5:6, :] * _rms_norm(y, npost_ref[...])


def _peer(h2t, u_bf, vt_bf, cnt, rk2, p1, p2, x1, ada3, norm_post):
    t, d = x1.shape
    s = t // ada3.shape[0]
    ne = 2 * u_bf.shape[0]
    tm = min(TM_PEER, s)
    te = TE_PEER
    tab2_spec = pl.BlockSpec((P_HEADS, tm // LANES, N_KEYS // 2, LANES), lambda i, j: (0, i, 0, 0))
    tab1_spec = pl.BlockSpec((P_HEADS, tm // LANES, te // N_KEYS, LANES), lambda i, j: (0, i, j, 0))
    return pl.pallas_call(
        functools.partial(_peer_kernel, tm=tm, te=te),
        grid=(t // tm, ne // te),
        in_specs=[pl.BlockSpec((d // 2, tm), lambda i, j: (0, i)),
                  pl.BlockSpec((te // 2, d), lambda i, j: (j, 0)),
                  pl.BlockSpec((d // 2, te), lambda i, j: (0, j)),
                  tab1_spec, tab2_spec, tab1_spec, tab2_spec,
                  pl.BlockSpec((tm, d), lambda i, j: (i, 0)),
                  pl.BlockSpec((1, 6, d), lambda i, j: ((i * tm) // s, 0, 0)),
                  pl.BlockSpec((1, d), lambda i, j: (0, 0))],
        out_specs=pl.BlockSpec((tm, d), lambda i, j: (i, 0)),
        out_shape=jax.ShapeDtypeStruct((t, d), F32),
        scratch_shapes=[pltpu.VMEM((d, tm), F32),
                        pltpu.VMEM((te, tm), BF16),
                        pltpu.VMEM((te, tm), BF16)],
        compiler_params=pltpu.CompilerParams(dimension_semantics=("arbitrary", "arbitrary"),
                                             vmem_limit_bytes=VMEM_LIMIT_BYTES),
        name="peer",
    )(h2t, u_bf, vt_bf, cnt, rk2, p1, p2, x1, ada3, norm_post)


def _layer(x, c, w_ada, b_ada, norm1_pre, norm1_post, w_in, conv_a_w, conv_qk_w, b_igate, b_fgate, mh_norm_w,
           w_branch_a, w_branch_m, w_out, norm2_pre, norm2_post, peer_wq, peer_subkeys, peer_u, peer_v):
    bsz, s, d = x.shape
    ng = 2 * M_HEADS
    ada3 = _ada(c, w_ada, b_ada).reshape(bsz, 6, d)

    o_if = 3 * d + 2 * M_HEADS * M_DK + 2 * d
    w_main = jnp.concatenate([w_in[:, :o_if], w_in[:, o_if + ng:]], axis=1).astype(BF16)
    w_if = w_in[:, o_if:o_if + ng]
    wif_col = jnp.pad(w_if, ((0, 0), (0, LANES - ng))).astype(BF16)
    wif_row = w_if.T.astype(BF16)
    b_if = jnp.concatenate([b_igate, b_fgate])
    bif_col = jnp.pad(b_if, (0, LANES - ng)).reshape(1, LANES)
    bif_row = b_if.reshape(ng, 1)

    ya, q, kt, v, so, sga, sgm, gcol, grow = _inproj(
        x, ada3, norm1_pre.reshape(1, d), w_main, wif_col, wif_row, bif_col, bif_row, conv_a_w, conv_qk_w)
    x1 = _mixer(x, ada3, q, kt, v, so, sga, sgm, ya, gcol, grow, mh_norm_w.reshape(1, d),
                norm1_post.reshape(1, d), w_branch_a.astype(BF16), w_branch_m.astype(BF16), w_out.astype(BF16))

    x1f = x1.reshape(bsz * s, d)
    h2t, cnt, rk2, p1, p2 = _route(x1f, ada3, norm2_pre.reshape(1, d), peer_wq.T.astype(BF16),
                                   peer_subkeys.astype(BF16))
    u_words, vt_words = _expert_words(peer_u, peer_v)
    out = _peer(h2t, u_words, vt_words, cnt, rk2, p1, p2, x1f, ada3, norm2_post.reshape(1, d))
    return out.reshape(bsz, s, d)


def kernel(x, c, w_ada, b_ada, norm1_pre, norm1_post, w_in, conv_a_w, conv_qk_w, b_igate, b_fgate, mh_norm_w, w_branch_a, w_branch_m, w_out, norm2_pre, norm2_post, peer_wq, peer_subkeys, peer_u, peer_v):
    for l in range(w_ada.shape[0]):
        x = _layer(x, c, w_ada[l], b_ada[l], norm1_pre[l], norm1_post[l], w_in[l], conv_a_w[l], conv_qk_w[l],
                   b_igate[l], b_fgate[l], mh_norm_w[l], w_branch_a[l], w_branch_m[l], w_out[l], norm2_pre[l],
                   norm2_post[l], peer_wq[l], peer_subkeys[l], peer_u[l], peer_v[l])
    return x
```

```python
import functools

import jax
import jax.numpy as jnp
from jax import lax
from jax.experimental import pallas as pl
from jax.experimental.pallas import tpu as pltpu

F32 = jnp.float32
BF16 = jnp.bfloat16

EPS = 1e-6
M_HEADS = 8
M_DK = 64
M_DV = 128
CONV_K = 3
QK_CONV_K = 4
P_HEADS = 8
N_KEYS = 128
P_TOPK = 16
HALO = 8
LANES = 128
BF16_ROWS = 16
NOT_RANKED = 99.0
KNOCK = 2.0 ** 100
GELU_HALF = 0.5

VMEM_LIMIT_BYTES = 56 * 1024 * 1024

MLSTM_CHUNK = 128
TS_INPROJ = 256
TS_MIXER = 512
TM_ROUTE = 256
TM_PEER = 1024
TE_PEER = 1024
PEER_GROUP = 2
TE_WORDS = 1024


def _resident(shape):
    nd = len(shape)
    return pl.BlockSpec(shape, lambda *_: (0,) * nd, pipeline_mode=pl.Buffered(1))


def _rms_norm(x, w):
    return x * lax.rsqrt(jnp.mean(x * x, axis=-1, keepdims=True) + EPS) * w


def _sigmoid(x):
    return 1.0 / (1.0 + jnp.exp(-x))


def _packed_words(x):
    return pltpu.bitcast(x, jnp.uint32)


def _unpacked(words):
    return pltpu.bitcast(words, BF16)


def _ada_kernel(c_ref, w_ref, b_ref, o_ref):
    c = c_ref[...]
    s = c * _sigmoid(c)
    o_ref[...] = jnp.dot(s.astype(BF16), w_ref[...].astype(BF16), preferred_element_type=F32) + b_ref[...]


def _ada(c, w_ada, b_ada):
    bsz, d = c.shape
    n = w_ada.shape[1]
    return pl.pallas_call(
        _ada_kernel,
        grid=(n // d,),
        in_specs=[pl.BlockSpec((bsz, d), lambda j: (0, 0)),
                  pl.BlockSpec((d, d), lambda j: (0, j)),
                  pl.BlockSpec((1, d), lambda j: (0, j))],
        out_specs=pl.BlockSpec((bsz, d), lambda j: (0, j)),
        out_shape=jax.ShapeDtypeStruct((bsz, n), F32),
        compiler_params=pltpu.CompilerParams(vmem_limit_bytes=VMEM_LIMIT_BYTES),
        name="ada",
    )(c, w_ada, b_ada.reshape(1, n))


def _inproj_kernel(x_ref, ada_ref, npre_ref, w_ref, wifc_ref, wifr_ref, bifc_ref, bifr_ref, cwa_ref, cwqk_ref,
                   ya_ref, q_ref, kt_ref, v_ref, so_ref, sga_ref, sgm_ref, gcol_ref, grow_ref,
                   ubuf, qkbuf, *, ts, d):
    @pl.when(pl.program_id(1) == 0)
    def _():
        ubuf[0:HALO, :] = jnp.zeros((HALO, d), F32)
        qkbuf[0:HALO, :] = jnp.zeros((HALO, d), F32)

    x = x_ref[0]
    ada = ada_ref[0]
    h = _rms_norm(x, npre_ref[...]) * (1.0 + ada[1:2, :]) + ada[0:1, :]
    hb = h.astype(BF16)

    def proj(k):
        return jnp.dot(hb, w_ref[:, k * d:(k + 1) * d], preferred_element_type=F32)

    u = proj(0) * proj(2)
    ubuf[HALO:HALO + ts, :] = u
    conv = u * cwa_ref[CONV_K - 1:CONV_K, :]
    for j in range(CONV_K - 1):
        off = HALO - (CONV_K - 1) + j
        conv = conv + ubuf[off:off + ts, :] * cwa_ref[j:j + 1, :]
    ya_ref[0] = _packed_words((proj(1) * conv).astype(BF16))
    ubuf[0:HALO, :] = ubuf[ts:ts + HALO, :]

    qk = proj(3)
    qkbuf[HALO:HALO + ts, :] = qk
    cq = qk * cwqk_ref[QK_CONV_K - 1:QK_CONV_K, :]
    for j in range(QK_CONV_K - 1):
        off = HALO - (QK_CONV_K - 1) + j
        cq = cq + qkbuf[off:off + ts, :] * cwqk_ref[j:j + 1, :]
    qkbuf[0:HALO, :] = qkbuf[ts:ts + HALO, :]
    cq = cq * _sigmoid(cq)
    nqk = M_HEADS * M_DK
    q_ref[0] = _packed_words((cq[:, :nqk] * (M_DK ** -0.5)).astype(BF16))
    kt_ref[0] = _packed_words(cq[:, nqk:].T.astype(BF16))

    v_ref[0] = _packed_words(proj(4).astype(BF16))
    so_ref[0] = _sigmoid(proj(5)).astype(BF16)
    sga_ref[0] = _sigmoid(proj(6)).astype(BF16)
    sgm_ref[0] = _sigmoid(proj(7)).astype(BF16)

    gcol_ref[0] = jnp.dot(hb, wifc_ref[...], preferred_element_type=F32) + bifc_ref[...]
    grow_ref[0] = lax.dot_general(wifr_ref[...], hb, (((1,), (1,)), ((), ())),
                                  preferred_element_type=F32) + bifr_ref[...]


def _inproj(x, ada3, norm_pre, w_main, wif_col, wif_row, bif_col, bif_row, conv_a_w, conv_qk_w):
    bsz, s, d = x.shape
    ts = min(TS_INPROJ, s)
    ng = 2 * M_HEADS
    nqk = M_HEADS * M_DK
    tok = lambda dt, w=d: jax.ShapeDtypeStruct((bsz, s, w), dt)
    tok_spec = lambda w=d: pl.BlockSpec((1, ts, w), lambda b, i: (b, i, 0))
    words = lambda w=d: jax.ShapeDtypeStruct((bsz, s // 2, w), jnp.uint32)
    words_spec = lambda w=d: pl.BlockSpec((1, ts // 2, w), lambda b, i: (b, i, 0))
    return pl.pallas_call(
        functools.partial(_inproj_kernel, ts=ts, d=d),
        grid=(bsz, s // ts),
        in_specs=[tok_spec(),
                  pl.BlockSpec((1, 6, d), lambda b, i: (b, 0, 0)),
                  _resident((1, d)),
                  _resident(w_main.shape),
                  _resident(wif_col.shape),
                  _resident(wif_row.shape),
                  _resident(bif_col.shape),
                  _resident(bif_row.shape),
                  _resident(conv_a_w.shape),
                  _resident(conv_qk_w.shape)],
        out_specs=[words_spec(), words_spec(nqk),
                   pl.BlockSpec((1, nqk // 2, ts), lambda b, i: (b, 0, i)),
                   words_spec(), tok_spec(), tok_spec(), tok_spec(),
                   tok_spec(LANES),
                   pl.BlockSpec((1, ng, ts), lambda b, i: (b, 0, i))],
        out_shape=[words(), words(nqk),
                   jax.ShapeDtypeStruct((bsz, nqk // 2, s), jnp.uint32),
                   words(), tok(BF16), tok(BF16), tok(BF16),
                   tok(F32, LANES),
                   jax.ShapeDtypeStruct((bsz, ng, s), F32)],
        scratch_shapes=[pltpu.VMEM((HALO + ts, d), F32), pltpu.VMEM((HALO + ts, d), F32)],
        compiler_params=pltpu.CompilerParams(dimension_semantics=("arbitrary", "arbitrary"),
                                             vmem_limit_bytes=VMEM_LIMIT_BYTES),
        name="inproj",
    )(x, ada3, norm_pre, w_main, wif_col, wif_row, bif_col, bif_row, conv_a_w, conv_qk_w)


def _log_sigmoid(x):
    return jnp.minimum(x, 0.0) - jnp.log(1.0 + jnp.exp(-jnp.abs(x)))


def _chunk_scan(x, axis, chunk, op, identity):
    pos = lax.broadcasted_iota(jnp.int32, x.shape, axis) % chunk
    k = 1
    while k < chunk:
        x = op(x, jnp.where(pos >= k, pltpu.roll(x, k, axis), identity))
        k *= 2
    return x


def _bf16_terms(x):
    hi = x.astype(BF16)
    r = x - hi.astype(F32)
    mid = r.astype(BF16)
    lo = (r - mid.astype(F32)).astype(BF16)
    return hi, mid, lo


def _spread_heads(x, sel):
    return sum(jnp.dot(t, sel, preferred_element_type=F32) for t in _bf16_terms(x))


def _mixer_kernel(x_ref, ada_ref, q_ref, kt_ref, v_ref, so_ref, sga_ref, sgm_ref, ya_ref, gcol_ref, grow_ref,
                  sel_ref, mhw_ref, npost_ref, wa_ref, wm_ref, wo_ref, o_ref,
                  c_st, mcol_st, mrow_st, ym_buf, *, ts, chunk):
    first = pl.program_id(1) == 0

    @pl.when(first)
    def _():
        c_st[...] = jnp.zeros(c_st.shape, F32)
        mcol_st[...] = jnp.zeros(mcol_st.shape, F32)
        mrow_st[...] = jnp.zeros(mrow_st.shape, F32)

    nc = ts // chunk
    H = M_HEADS
    neg_inf = -jnp.inf

    gcol = gcol_ref[0]
    b_col = _chunk_scan(_log_sigmoid(gcol), 0, chunk, jnp.add, 0.0)
    r_col = pltpu.roll(gcol, H, 1) - b_col
    cm_col = _chunk_scan(r_col, 0, chunk, jnp.maximum, neg_inf)
    m = mcol_st[0:1, :]
    m_rows, decay_rows = [], []
    for c in range(nc):
        last = (c + 1) * chunk - 1
        b_l, cm_l = b_col[last:last + 1, :], cm_col[last:last + 1, :]
        m_next = b_l + jnp.maximum(m, cm_l)
        m_rows.append(m)
        decay_rows.append(jnp.exp(b_l + m - m_next))
        m = m_next
    mcol_st[...] = jnp.broadcast_to(m, mcol_st.shape)
    pad = jnp.zeros((8 - (2 * nc) % 8, LANES), F32)
    spread = _spread_heads(jnp.concatenate([b_col, cm_col] + m_rows + decay_rows + [pad], axis=0), sel_ref[...])
    bc_all, cm_all, sc_all = spread[0:ts], spread[ts:2 * ts], spread[2 * ts:]

    grow = grow_ref[0]
    i_row = grow[0:H, :]
    b_row = _chunk_scan(_log_sigmoid(grow[H:2 * H, :]), 1, chunk, jnp.add, 0.0)
    r_row = i_row - b_row
    cm_row = _chunk_scan(r_row, 1, chunk, jnp.maximum, neg_inf)
    mr = mrow_st[:, 0:1]
    ws_rows = []
    for c in range(nc):
        last = (c + 1) * chunk - 1
        b_l, cm_l = b_row[:, last:last + 1], cm_row[:, last:last + 1]
        m_next = b_l + jnp.maximum(mr, cm_l)
        ws_rows.append(jnp.exp(b_l + r_row[:, c * chunk:(c + 1) * chunk] - m_next))
        mr = m_next
    mrow_st[...] = jnp.broadcast_to(mr, mrow_st.shape)

    tril = (lax.broadcasted_iota(jnp.int32, (chunk, chunk), 0)
            >= lax.broadcasted_iota(jnp.int32, (chunk, chunk), 1))
    ones = jnp.ones((chunk, M_DV), BF16)

    for hd in range(H):
        hl = slice(hd * M_DV, (hd + 1) * M_DV)
        c_aug = c_st[hd]
        for c in range(nc):
            rows = slice(c * chunk, (c + 1) * chunk)
            wrows = slice(c * chunk // 2, (c + 1) * chunk // 2)
            q = _unpacked(q_ref[0, wrows, hd * M_DK:(hd + 1) * M_DK])
            kt = _unpacked(kt_ref[0, hd * M_DK // 2:(hd + 1) * M_DK // 2, rows])
            v = _unpacked(v_ref[0, wrows, hl])
            bc = bc_all[rows, hl]
            m_prev = sc_all[c:c + 1, hl]
            decay = sc_all[nc + c:nc + c + 1, hl]
            m_t = bc + jnp.maximum(m_prev, cm_all[rows, hl])
            dlog = jnp.where(tril, bc[:, :chunk] + r_row[hd:hd + 1, rows], neg_inf)
            s_qk = jnp.dot(q, kt, preferred_element_type=F32) * jnp.exp(dlog - m_t[:, :chunk])
            a_inter = jnp.exp(bc + m_prev - m_t)
            qc = jnp.dot(q, c_aug.astype(BF16), preferred_element_type=F32)
            num = jnp.dot(s_qk.astype(BF16), v, preferred_element_type=F32) + a_inter * qc[:, :M_DV]
            den = jnp.sum(s_qk, axis=1, keepdims=True) + a_inter * qc[:, M_DV:]
            h = num / jnp.maximum(jnp.abs(den), jnp.exp(-m_t))
            hn = _rms_norm(h, mhw_ref[:, hl])
            ym_buf[rows, hl] = (so_ref[0, rows, hl].astype(F32) * hn).astype(BF16)
            ktw = (kt.astype(F32) * ws_rows[c][hd:hd + 1, :]).astype(BF16)
            upd = jnp.dot(ktw, jnp.concatenate([v, ones], axis=1), preferred_element_type=F32)
            c_aug = jnp.concatenate([decay, decay], axis=1) * c_aug + upd
        c_st[hd] = c_aug

    mix = (sga_ref[0].astype(F32) * jnp.dot(_unpacked(ya_ref[0]), wa_ref[...], preferred_element_type=F32)
           + sgm_ref[0].astype(F32) * jnp.dot(ym_buf[...], wm_ref[...], preferred_element_type=F32))
    y = jnp.dot(mix.astype(BF16), wo_ref[...], preferred_element_type=F32)
    o_ref[0] = x_ref[0] + ada_ref[0][2:3, :] * _rms_norm(y, npost_ref[...])


def _mixer(x, ada3, q, kt, v, so, sga, sgm, ya, gcol, grow, mh_norm_w, norm_post, wa, wm, wo):
    bsz, s, d = x.shape
    ts = min(TS_MIXER, s)
    chunk = min(MLSTM_CHUNK, ts)
    ng = 2 * M_HEADS
    nqk = M_HEADS * M_DK
    sel = (jnp.arange(LANES)[:, None] - M_HEADS == jnp.arange(M_HEADS * M_DV)[None, :] // M_DV).astype(BF16)
    tok_spec = lambda w=d: pl.BlockSpec((1, ts, w), lambda b, i: (b, i, 0))
    words_spec = lambda w=d: pl.BlockSpec((1, ts // 2, w), lambda b, i: (b, i, 0))
    return pl.pallas_call(
        functools.partial(_mixer_kernel, ts=ts, chunk=chunk),
        grid=(bsz, s // ts),
        in_specs=[tok_spec(),
                  pl.BlockSpec((1, 6, d), lambda b, i: (b, 0, 0)),
                  words_spec(nqk),
                  pl.BlockSpec((1, nqk // 2, ts), lambda b, i: (b, 0, i)),
                  words_spec(), tok_spec(), tok_spec(), tok_spec(), words_spec(),
                  tok_spec(LANES),
                  pl.BlockSpec((1, ng, ts), lambda b, i: (b, 0, i)),
                  _resident(sel.shape),
                  _resident((1, d)), _resident((1, d)),
                  _resident(wa.shape), _resident(wm.shape), _resident(wo.shape)],
        out_specs=tok_spec(),
        out_shape=jax.ShapeDtypeStruct((bsz, s, d), F32),
        scratch_shapes=[pltpu.VMEM((M_HEADS, M_DK, 2 * M_DV), F32),
                        pltpu.VMEM((8, LANES), F32),
                        pltpu.VMEM((M_HEADS, LANES), F32),
                        pltpu.VMEM((ts, d), BF16)],
        compiler_params=pltpu.CompilerParams(dimension_semantics=("arbitrary", "arbitrary"),
                                             vmem_limit_bytes=VMEM_LIMIT_BYTES),
        name="mixer",
    )(x, ada3, q, kt, v, so, sga, sgm, ya, gcol, grow, sel, mh_norm_w, norm_post, wa, wm, wo)


def _argmax_rows(s, exact):
    m = jnp.max(s, axis=0, keepdims=True)
    hit = s == m
    if exact:
        row = lax.broadcasted_iota(jnp.int32, s.shape, 0)
        hit = row == jnp.min(jnp.where(hit, row, s.shape[0]), axis=0, keepdims=True)
    return hit, m


def _topk_rows(arrays, k, exact):
    arrays = list(arrays)
    vals = [[] for _ in arrays]
    if exact:
        ranks = [jnp.full(s.shape, NOT_RANKED, F32) for s in arrays]
        for r in range(k):
            for i, s in enumerate(arrays):
                hit, m = _argmax_rows(s, exact)
                ranks[i] = jnp.where(hit, float(r), ranks[i])
                arrays[i] = jnp.where(hit, -jnp.inf, s)
                vals[i].append(m)
        return vals, ranks
    for r in range(k):
        for i, s in enumerate(arrays):
            hit, m = _argmax_rows(s, exact)
            arrays[i] = jnp.where(hit, -(r + 2.0) * KNOCK, s)
            vals[i].append(m)
    ranks = [jnp.where(s <= -1.5 * KNOCK, s * (-1.0 / KNOCK) - 2.0, NOT_RANKED) for s in arrays]
    return vals, ranks


def _route_tables(s1, s2, exact, topk=None):
    (v1, v2), (rank1, rank2) = topk if topk is not None else _topk_rows((s1, s2), P_TOPK, exact)
    v2_lo = jnp.concatenate(v2[0:8], axis=0)
    v2_all = jnp.concatenate(v2, axis=0)
    v1_hi = jnp.concatenate(v1[8:16], axis=0)
    cand = jnp.concatenate([v1[0] + v2_all] + [v1[j] + v2_lo for j in range(1, 8)] + [v1_hi + v2[0]], axis=0)
    cur = cand
    for _ in range(P_TOPK):
        hit, _ = _argmax_rows(cur, exact)
        cur = jnp.where(hit, -jnp.inf, cur)
    sel = jnp.where(cur == -jnp.inf, 1.0, 0.0)
    ranked = lambda rk: jnp.sum(jnp.where(rk < NOT_RANKED, 1.0, 0.0), axis=0, keepdims=True)
    picked = jnp.maximum(jnp.maximum(ranked(rank1), ranked(rank2)), jnp.sum(sel, axis=0, keepdims=True))
    lowest = jnp.minimum(jnp.min(s1, axis=0, keepdims=True), jnp.min(s2, axis=0, keepdims=True))
    tied = jnp.where((picked > float(P_TOPK)) | (lowest <= -KNOCK), 1.0, 0.0)
    top = v1[0] + v2[0]
    z = jnp.sum(sel * jnp.exp(cand - top), axis=0, keepdims=True)
    counts = [jnp.sum(sel[0:16], axis=0, keepdims=True)]
    counts += [jnp.sum(sel[8 + 8 * j:16 + 8 * j], axis=0, keepdims=True) for j in range(1, 8)]
    counts += [sel[72 + j:73 + j] for j in range(8)]
    pk = (N_KEYS // BF16_ROWS, BF16_ROWS, LANES)
    rank1b = rank1.astype(BF16).reshape(pk)
    cnt = jnp.zeros(pk, BF16)
    for j in range(P_TOPK):
        count_j = pltpu.bitcast(jnp.broadcast_to(_bf16_pair_words(counts[j]), (8, LANES)), BF16)
        cnt = jnp.where(rank1b == float(j), count_j[None], cnt)
    cnt = cnt.reshape(N_KEYS, LANES)
    p1 = GELU_HALF * jnp.exp(s1 - v1[0])
    p2 = jnp.exp(s2 - v2[0]) * (1.0 / z)
    return cnt, rank2, p1, p2, tied


def _bf16_pair_words(x):
    hi = pltpu.bitcast(x.astype(BF16).astype(F32), jnp.uint32)
    return hi | lax.shift_right_logical(hi, jnp.uint32(16))


def _route_kernel(x_ref, ada_ref, npre_ref, wqt_ref, sk_ref, h2t_ref, cnt_ref, rk2_ref, p1_ref, p2_ref, qt_buf,
                  *, tm):
    ada = ada_ref[0]
    h2 = _rms_norm(x_ref[...], npre_ref[...]) * (1.0 + ada[4:5, :]) + ada[3:4, :]
    h2t = h2.T.astype(BF16)
    h2t_ref[...] = _packed_words(h2t)
    qt_buf[...] = jnp.dot(wqt_ref[...], h2t, preferred_element_type=F32).astype(BF16)

    def head(hd, carry):
        r0 = pl.multiple_of(hd * (2 * N_KEYS), 2 * N_KEYS)
        s1 = jnp.dot(sk_ref[hd, 0], qt_buf[pl.ds(r0, N_KEYS), :], preferred_element_type=F32)
        s2 = jnp.dot(sk_ref[hd, 1], qt_buf[pl.ds(r0 + N_KEYS, N_KEYS), :], preferred_element_type=F32)

        def tables(lc, exact, topk=None):
            sl = slice(lc * LANES, (lc + 1) * LANES)
            cnt, rk2, p1, p2, tied = _route_tables(s1[:, sl], s2[:, sl], exact, topk)
            cnt_ref[hd, lc] = _bf16_pair_words(cnt)
            rk2_ref[hd, lc] = _packed_words(rk2.astype(BF16))
            p1_ref[hd, lc] = _bf16_pair_words(p1)
            p2_ref[hd, lc] = _packed_words(p2.astype(BF16))
            return tied

        nlc = tm // LANES
        slabs = [s[:, lc * LANES:(lc + 1) * LANES] for lc in range(nlc) for s in (s1, s2)]
        vals, ranks = _topk_rows(slabs, P_TOPK, False)
        for lc in range(nlc):
            tied = tables(lc, False, ((vals[2 * lc], vals[2 * lc + 1]), (ranks[2 * lc], ranks[2 * lc + 1])))

            @pl.when(jnp.max(tied) > 0.0)
            def _():
                tables(lc, exact=True)
        return carry

    lax.fori_loop(0, P_HEADS, head, 0)


def _route(x1, ada3, norm_pre, wqt, subkeys):
    t, d = x1.shape
    s = t // ada3.shape[0]
    tm = min(TM_ROUTE, s)
    tab = lambda rows: jax.ShapeDtypeStruct((P_HEADS, t // LANES, rows, LANES), jnp.uint32)
    tab_spec = lambda rows: pl.BlockSpec((P_HEADS, tm // LANES, rows, LANES), lambda i: (0, i, 0, 0))
    k1, k2 = N_KEYS, N_KEYS // 2
    return pl.pallas_call(
        functools.partial(_route_kernel, tm=tm),
        grid=(t // tm,),
        in_specs=[pl.BlockSpec((tm, d), lambda i: (i, 0)),
                  pl.BlockSpec((1, 6, d), lambda i: ((i * tm) // s, 0, 0)),
                  _resident((1, d)),
                  _resident(wqt.shape),
                  _resident(subkeys.shape)],
        out_specs=[pl.BlockSpec((d // 2, tm), lambda i: (0, i)),
                   tab_spec(k1), tab_spec(k2), tab_spec(k1), tab_spec(k2)],
        out_shape=[jax.ShapeDtypeStruct((d // 2, t), jnp.uint32), tab(k1), tab(k2), tab(k1), tab(k2)],
        scratch_shapes=[pltpu.VMEM((wqt.shape[0], tm), BF16)],
        compiler_params=pltpu.CompilerParams(dimension_semantics=("arbitrary",),
                                             vmem_limit_bytes=VMEM_LIMIT_BYTES),
        name="route",
    )(x1, ada3, norm_pre, wqt, subkeys)


def _expert_words_kernel(u_ref, v_ref, uw_ref, vtw_ref):
    uw_ref[...] = _packed_words(u_ref[...].astype(BF16))
    vtw_ref[...] = _packed_words(v_ref[...].T.astype(BF16))


def _expert_words(u, v):
    ne, d = u.shape
    te = min(TE_WORDS, ne)
    return pl.pallas_call(
        _expert_words_kernel,
        grid=(ne // te,),
        in_specs=[pl.BlockSpec((te, d), lambda j: (j, 0)), pl.BlockSpec((te, d), lambda j: (j, 0))],
        out_specs=[pl.BlockSpec((te // 2, d), lambda j: (j, 0)), pl.BlockSpec((d // 2, te), lambda j: (0, j))],
        out_shape=[jax.ShapeDtypeStruct((ne // 2, d), jnp.uint32), jax.ShapeDtypeStruct((d // 2, ne), jnp.uint32)],
        compiler_params=pltpu.CompilerParams(vmem_limit_bytes=VMEM_LIMIT_BYTES),
        name="expert_words",
    )(u, v)


def _gelu_tanh_x2(z):
    c = 0.7978845608028654
    return z + z * jnp.tanh(z * (c + (c * 0.044715) * (z * z)))


def _peer_kernel(h2t_ref, u_ref, vt_ref, cnt_ref, rk2_ref, p1_ref, p2_ref, x_ref, ada_ref, npost_ref, o_ref,
                 acc, z_buf, a_buf, *, tm, te):
    j = pl.program_id(1)

    @pl.when(j == 0)
    def _():
        acc[...] = jnp.zeros(acc.shape, F32)

    z_buf[...] = jnp.dot(_unpacked(u_ref[...]), _unpacked(h2t_ref[...]), preferred_element_type=F32).astype(BF16)

    pk = (N_KEYS // BF16_ROWS, BF16_ROWS, LANES)
    for lc in range(tm // LANES):
        sl = slice(lc * LANES, (lc + 1) * LANES)
        for a0 in range(0, te // N_KEYS, PEER_GROUP):
            gates = [None] * PEER_GROUP
            for hd in range(P_HEADS):
                rk2 = _unpacked(rk2_ref[hd, lc]).reshape(pk)
                p2 = _unpacked(p2_ref[hd, lc]).reshape(pk)
                for g in range(PEER_GROUP):
                    al = a0 + g
                    cnt_a = pltpu.bitcast(jnp.broadcast_to(cnt_ref[hd, lc, al:al + 1, :], (8, LANES)), BF16)
                    p1_a = pltpu.bitcast(jnp.broadcast_to(p1_ref[hd, lc, al:al + 1, :], (8, LANES)), BF16)
                    term = jnp.where(rk2 < cnt_a[None], p2 * p1_a[None], jnp.zeros((), BF16))
                    gates[g] = term if gates[g] is None else gates[g] + term
            for g in range(PEER_GROUP):
                rows = slice((a0 + g) * N_KEYS, (a0 + g + 1) * N_KEYS)
                a_buf[rows, sl] = _gelu_tanh_x2(z_buf[rows, sl]) * gates[g].reshape(N_KEYS, LANES)

    acc[...] += jnp.dot(_unpacked(vt_ref[...]), a_buf[...], preferred_element_type=F32)

    @pl.when(j == pl.num_programs(1) - 1)
    def _():
        y = acc[...].T
        o_ref[...] = x_ref[...] + ada_ref[0][5:6, :] * _rms_norm(y, npost_ref[...])


def _peer(h2t, u_bf, vt_bf, cnt, rk2, p1, p2, x1, ada3, norm_post):
    t, d = x1.shape
    s = t // ada3.shape[0]
    ne = 2 * u_bf.shape[0]
    tm = min(TM_PEER, s)
    te = TE_PEER
    tab2_spec = pl.BlockSpec((P_HEADS, tm // LANES, N_KEYS // 2, LANES), lambda i, j: (0, i, 0, 0))
    tab1_spec = pl.BlockSpec((P_HEADS, tm // LANES, te // N_KEYS, LANES), lambda i, j: (0, i, j, 0))
    return pl.pallas_call(
        functools.partial(_peer_kernel, tm=tm, te=te),
        grid=(t // tm, ne // te),
        in_specs=[pl.BlockSpec((d // 2, tm), lambda i, j: (0, i)),
                  pl.BlockSpec((te // 2, d), lambda i, j: (j, 0)),
                  pl.BlockSpec((d // 2, te), lambda i, j: (0, j)),
                  tab1_spec, tab2_spec, tab1_spec, tab2_spec,
                  pl.BlockSpec((tm, d), lambda i, j: (i, 0)),
                  pl.BlockSpec((1, 6, d), lambda i, j: ((i * tm) // s, 0, 0)),
                  pl.BlockSpec((1, d), lambda i, j: (0, 0))],
        out_specs=pl.BlockSpec((tm, d), lambda i, j: (i, 0)),
        out_shape=jax.ShapeDtypeStruct((t, d), F32),
        scratch_shapes=[pltpu.VMEM((d, tm), F32),
                        pltpu.VMEM((te, tm), BF16),
                        pltpu.VMEM((te, tm), BF16)],
        compiler_params=pltpu.CompilerParams(dimension_semantics=("arbitrary", "arbitrary"),
                                             vmem_limit_bytes=VMEM_LIMIT_BYTES),
        name="peer",
    )(h2t, u_bf, vt_bf, cnt, rk2, p1, p2, x1, ada3, norm_post)


def _layer(x, c, w_ada, b_ada, norm1_pre, norm1_post, w_in, conv_a_w, conv_qk_w, b_igate, b_fgate, mh_norm_w,
           w_branch_a, w_branch_m, w_out, norm2_pre, norm2_post, peer_wq, peer_subkeys, peer_u, peer_v):
    bsz, s, d = x.shape
    ng = 2 * M_HEADS
    ada3 = _ada(c, w_ada, b_ada).reshape(bsz, 6, d)

    o_if = 3 * d + 2 * M_HEADS * M_DK + 2 * d
    w_main = jnp.concatenate([w_in[:, :o_if], w_in[:, o_if + ng:]], axis=1).astype(BF16)
    w_if = w_in[:, o_if:o_if + ng]
    wif_col = jnp.pad(w_if, ((0, 0), (0, LANES - ng))).astype(BF16)
    wif_row = w_if.T.astype(BF16)
    b_if = jnp.concatenate([b_igate, b_fgate])
    bif_col = jnp.pad(b_if, (0, LANES - ng)).reshape(1, LANES)
    bif_row = b_if.reshape(ng, 1)

    ya, q, kt, v, so, sga, sgm, gcol, grow = _inproj(
        x, ada3, norm1_pre.reshape(1, d), w_main, wif_col, wif_row, bif_col, bif_row, conv_a_w, conv_qk_w)
    x1 = _mixer(x, ada3, q, kt, v, so, sga, sgm, ya, gcol, grow, mh_norm_w.reshape(1, d),
                norm1_post.reshape(1, d), w_branch_a.astype(BF16), w_branch_m.astype(BF16), w_out.astype(BF16))

    x1f = x1.reshape(bsz * s, d)
    h2t, cnt, rk2, p1, p2 = _route(x1f, ada3, norm2_pre.reshape(1, d), peer_wq.T.astype(BF16),
                                   peer_subkeys.astype(BF16))
    u_words, vt_words = _expert_words(peer_u, peer_v)
    out = _peer(h2t, u_words, vt_words, cnt, rk2, p1, p2, x1f, ada3, norm2_post.reshape(1, d))
    return out.reshape(bsz, s, d)


def kernel(x, c, w_ada, b_ada, norm1_pre, norm1_post, w_in, conv_a_w, conv_qk_w, b_igate, b_fgate, mh_norm_w, w_branch_a, w_branch_m, w_out, norm2_pre, norm2_post, peer_wq, peer_subkeys, peer_u, peer_v):
    for l in range(w_ada.shape[0]):
        x = _layer(x, c, w_ada[l], b_ada[l], norm1_pre[l], norm1_post[l], w_in[l], conv_a_w[l], conv_qk_w[l],
                   b_igate[l], b_fgate[l], mh_norm_w[l], w_branch_a[l], w_branch_m[l], w_out[l], norm2_pre[l],
                   norm2_post[l], peer_wq[l], peer_subkeys[l], peer_u[l], peer_v[l])
    return x
```

```python
import functools

import jax
import jax.numpy as jnp
from jax import lax
from jax.experimental import pallas as pl
from jax.experimental.pallas import tpu as pltpu

F32 = jnp.float32
BF16 = jnp.bfloat16

EPS = 1e-6
M_HEADS = 8
M_DK = 64
M_DV = 128
CONV_K = 3
QK_CONV_K = 4
P_HEADS = 8
N_KEYS = 128
P_TOPK = 16
HALO = 8
LANES = 128
BF16_ROWS = 16
NOT_RANKED = 99.0
KNOCK = 2.0 ** 100
GELU_HALF = 0.5

VMEM_LIMIT_BYTES = 56 * 1024 * 1024

MLSTM_CHUNK = 128
TS_INPROJ = 256
TS_MIXER = 512
TM_ROUTE = 256
TM_PEER = 1024
TE_PEER = 1024
PEER_GROUP = 2
TE_WORDS = 1024


def _resident(shape):
    nd = len(shape)
    return pl.BlockSpec(shape, lambda *_: (0,) * nd, pipeline_mode=pl.Buffered(1))


def _rms_norm(x, w):
    return x * lax.rsqrt(jnp.mean(x * x, axis=-1, keepdims=True) + EPS) * w


def _sigmoid(x):
    return 1.0 / (1.0 + jnp.exp(-x))


def _packed_words(x):
    return pltpu.bitcast(x, jnp.uint32)


def _unpacked(words):
    return pltpu.bitcast(words, BF16)


def _ada_kernel(c_ref, w_ref, b_ref, o_ref):
    c = c_ref[...]
    s = c * _sigmoid(c)
    o_ref[...] = jnp.dot(s.astype(BF16), w_ref[...].astype(BF16), preferred_element_type=F32) + b_ref[...]


def _ada(c, w_ada, b_ada):
    bsz, d = c.shape
    n = w_ada.shape[1]
    return pl.pallas_call(
        _ada_kernel,
        grid=(n // d,),
        in_specs=[pl.BlockSpec((bsz, d), lambda j: (0, 0)),
                  pl.BlockSpec((d, d), lambda j: (0, j)),
                  pl.BlockSpec((1, d), lambda j: (0, j))],
        out_specs=pl.BlockSpec((bsz, d), lambda j: (0, j)),
        out_shape=jax.ShapeDtypeStruct((bsz, n), F32),
        compiler_params=pltpu.CompilerParams(vmem_limit_bytes=VMEM_LIMIT_BYTES),
        name="ada",
    )(c, w_ada, b_ada.reshape(1, n))


def _inproj_kernel(x_ref, ada_ref, npre_ref, w_ref, wifc_ref, wifr_ref, bifc_ref, bifr_ref, cwa_ref, cwqk_ref,
                   ya_ref, q_ref, kt_ref, v_ref, so_ref, sga_ref, sgm_ref, gcol_ref, grow_ref,
                   ubuf, qkbuf, *, ts, d):
    @pl.when(pl.program_id(1) == 0)
    def _():
        ubuf[0:HALO, :] = jnp.zeros((HALO, d), F32)
        qkbuf[0:HALO, :] = jnp.zeros((HALO, d), F32)

    x = x_ref[0]
    ada = ada_ref[0]
    h = _rms_norm(x, npre_ref[...]) * (1.0 + ada[1:2, :]) + ada[0:1, :]
    hb = h.astype(BF16)

    def proj(k):
        return jnp.dot(hb, w_ref[:, k * d:(k + 1) * d], preferred_element_type=F32)

    u = proj(0) * proj(2)
    ubuf[HALO:HALO + ts, :] = u
    conv = u * cwa_ref[CONV_K - 1:CONV_K, :]
    for j in range(CONV_K - 1):
        off = HALO - (CONV_K - 1) + j
        conv = conv + ubuf[off:off + ts, :] * cwa_ref[j:j + 1, :]
    ya_ref[0] = _packed_words((proj(1) * conv).astype(BF16))
    ubuf[0:HALO, :] = ubuf[ts:ts + HALO, :]

    qk = proj(3)
    qkbuf[HALO:HALO + ts, :] = qk
    cq = qk * cwqk_ref[QK_CONV_K - 1:QK_CONV_K, :]
    for j in range(QK_CONV_K - 1):
        off = HALO - (QK_CONV_K - 1) + j
        cq = cq + qkbuf[off:off + ts, :] * cwqk_ref[j:j + 1, :]
    qkbuf[0:HALO, :] = qkbuf[ts:ts + HALO, :]
    cq = cq * _sigmoid(cq)
    nqk = M_HEADS * M_DK
    q_ref[0] = _packed_words((cq[:, :nqk] * (M_DK ** -0.5)).astype(BF16))
    kt_ref[0] = _packed_words(cq[:, nqk:].T.astype(BF16))

    v_ref[0] = _packed_words(proj(4).astype(BF16))
    so_ref[0] = _sigmoid(proj(5)).astype(BF16)
    sga_ref[0] = _sigmoid(proj(6)).astype(BF16)
    sgm_ref[0] = _sigmoid(proj(7)).astype(BF16)

    gcol_ref[0] = jnp.dot(hb, wifc_ref[...], preferred_element_type=F32) + bifc_ref[...]
    grow_ref[0] = lax.dot_general(wifr_ref[...], hb, (((1,), (1,)), ((), ())),
                                  preferred_element_type=F32) + bifr_ref[...]


def _inproj(x, ada3, norm_pre, w_main, wif_col, wif_row, bif_col, bif_row, conv_a_w, conv_qk_w):
    bsz, s, d = x.shape
    ts = min(TS_INPROJ, s)
    assert s % ts == 0 and d == 2 * M_HEADS * M_DK == M_HEADS * M_DV, (s, ts, d)
    ng = 2 * M_HEADS
    nqk = M_HEADS * M_DK
    tok = lambda dt, w=d: jax.ShapeDtypeStruct((bsz, s, w), dt)
    tok_spec = lambda w=d: pl.BlockSpec((1, ts, w), lambda b, i: (b, i, 0))
    words = lambda w=d: jax.ShapeDtypeStruct((bsz, s // 2, w), jnp.uint32)
    words_spec = lambda w=d: pl.BlockSpec((1, ts // 2, w), lambda b, i: (b, i, 0))
    return pl.pallas_call(
        functools.partial(_inproj_kernel, ts=ts, d=d),
        grid=(bsz, s // ts),
        in_specs=[tok_spec(),
                  pl.BlockSpec((1, 6, d), lambda b, i: (b, 0, 0)),
                  _resident((1, d)),
                  _resident(w_main.shape),
                  _resident(wif_col.shape),
                  _resident(wif_row.shape),
                  _resident(bif_col.shape),
                  _resident(bif_row.shape),
                  _resident(conv_a_w.shape),
                  _resident(conv_qk_w.shape)],
        out_specs=[words_spec(), words_spec(nqk),
                   pl.BlockSpec((1, nqk // 2, ts), lambda b, i: (b, 0, i)),
                   words_spec(), tok_spec(), tok_spec(), tok_spec(),
                   tok_spec(LANES),
                   pl.BlockSpec((1, ng, ts), lambda b, i: (b, 0, i))],
        out_shape=[words(), words(nqk),
                   jax.ShapeDtypeStruct((bsz, nqk // 2, s), jnp.uint32),
                   words(), tok(BF16), tok(BF16), tok(BF16),
                   tok(F32, LANES),
                   jax.ShapeDtypeStruct((bsz, ng, s), F32)],
        scratch_shapes=[pltpu.VMEM((HALO + ts, d), F32), pltpu.VMEM((HALO + ts, d), F32)],
        compiler_params=pltpu.CompilerParams(dimension_semantics=("arbitrary", "arbitrary"),
                                             vmem_limit_bytes=VMEM_LIMIT_BYTES),
        name="inproj",
    )(x, ada3, norm_pre, w_main, wif_col, wif_row, bif_col, bif_row, conv_a_w, conv_qk_w)


def _log_sigmoid(x):
    return jnp.minimum(x, 0.0) - jnp.log(1.0 + jnp.exp(-jnp.abs(x)))


def _chunk_scan(x, axis, chunk, op, identity):
    pos = lax.broadcasted_iota(jnp.int32, x.shape, axis) % chunk
    k = 1
    while k < chunk:
        x = op(x, jnp.where(pos >= k, pltpu.roll(x, k, axis), identity))
        k *= 2
    return x


def _bf16_terms(x):
    hi = x.astype(BF16)
    r = x - hi.astype(F32)
    mid = r.astype(BF16)
    lo = (r - mid.astype(F32)).astype(BF16)
    return hi, mid, lo


def _spread_heads(x, sel):
    return sum(jnp.dot(t, sel, preferred_element_type=F32) for t in _bf16_terms(x))


def _mixer_kernel(x_ref, ada_ref, q_ref, kt_ref, v_ref, so_ref, sga_ref, sgm_ref, ya_ref, gcol_ref, grow_ref,
                  sel_ref, mhw_ref, npost_ref, wa_ref, wm_ref, wo_ref, o_ref,
                  c_st, mcol_st, mrow_st, ym_buf, *, ts, chunk):
    first = pl.program_id(1) == 0

    @pl.when(first)
    def _():
        c_st[...] = jnp.zeros(c_st.shape, F32)
        mcol_st[...] = jnp.zeros(mcol_st.shape, F32)
        mrow_st[...] = jnp.zeros(mrow_st.shape, F32)

    nc = ts // chunk
    H = M_HEADS
    neg_inf = -jnp.inf

    gcol = gcol_ref[0]
    b_col = _chunk_scan(_log_sigmoid(gcol), 0, chunk, jnp.add, 0.0)
    r_col = pltpu.roll(gcol, H, 1) - b_col
    cm_col = _chunk_scan(r_col, 0, chunk, jnp.maximum, neg_inf)
    m = mcol_st[0:1, :]
    m_rows, decay_rows = [], []
    for c in range(nc):
        last = (c + 1) * chunk - 1
        b_l, cm_l = b_col[last:last + 1, :], cm_col[last:last + 1, :]
        m_next = b_l + jnp.maximum(m, cm_l)
        m_rows.append(m)
        decay_rows.append(jnp.exp(b_l + m - m_next))
        m = m_next
    mcol_st[...] = jnp.broadcast_to(m, mcol_st.shape)
    pad = jnp.zeros((8 - (2 * nc) % 8, LANES), F32)
    spread = _spread_heads(jnp.concatenate([b_col, cm_col] + m_rows + decay_rows + [pad], axis=0), sel_ref[...])
    bc_all, cm_all, sc_all = spread[0:ts], spread[ts:2 * ts], spread[2 * ts:]

    grow = grow_ref[0]
    i_row = grow[0:H, :]
    b_row = _chunk_scan(_log_sigmoid(grow[H:2 * H, :]), 1, chunk, jnp.add, 0.0)
    r_row = i_row - b_row
    cm_row = _chunk_scan(r_row, 1, chunk, jnp.maximum, neg_inf)
    mr = mrow_st[:, 0:1]
    ws_rows = []
    for c in range(nc):
        last = (c + 1) * chunk - 1
        b_l, cm_l = b_row[:, last:last + 1], cm_row[:, last:last + 1]
        m_next = b_l + jnp.maximum(mr, cm_l)
        ws_rows.append(jnp.exp(b_l + r_row[:, c * chunk:(c + 1) * chunk] - m_next))
        mr = m_next
    mrow_st[...] = jnp.broadcast_to(mr, mrow_st.shape)

    tril = (lax.broadcasted_iota(jnp.int32, (chunk, chunk), 0)
            >= lax.broadcasted_iota(jnp.int32, (chunk, chunk), 1))
    ones = jnp.ones((chunk, M_DV), BF16)

    for hd in range(H):
        hl = slice(hd * M_DV, (hd + 1) * M_DV)
        c_aug = c_st[hd]
        for c in range(nc):
            rows = slice(c * chunk, (c + 1) * chunk)
            wrows = slice(c * chunk // 2, (c + 1) * chunk // 2)
            q = _unpacked(q_ref[0, wrows, hd * M_DK:(hd + 1) * M_DK])
            kt = _unpacked(kt_ref[0, hd * M_DK // 2:(hd + 1) * M_DK // 2, rows])
            v = _unpacked(v_ref[0, wrows, hl])
            bc = bc_all[rows, hl]
            m_prev = sc_all[c:c + 1, hl]
            decay = sc_all[nc + c:nc + c + 1, hl]
            m_t = bc + jnp.maximum(m_prev, cm_all[rows, hl])
            dlog = jnp.where(tril, bc[:, :chunk] + r_row[hd:hd + 1, rows], neg_inf)
            s_qk = jnp.dot(q, kt, preferred_element_type=F32) * jnp.exp(dlog - m_t[:, :chunk])
            a_inter = jnp.exp(bc + m_prev - m_t)
            qc = jnp.dot(q, c_aug.astype(BF16), preferred_element_type=F32)
            num = jnp.dot(s_qk.astype(BF16), v, preferred_element_type=F32) + a_inter * qc[:, :M_DV]
            den = jnp.sum(s_qk, axis=1, keepdims=True) + a_inter * qc[:, M_DV:]
            h = num / jnp.maximum(jnp.abs(den), jnp.exp(-m_t))
            hn = _rms_norm(h, mhw_ref[:, hl])
            ym_buf[rows, hl] = (so_ref[0, rows, hl].astype(F32) * hn).astype(BF16)
            ktw = (kt.astype(F32) * ws_rows[c][hd:hd + 1, :]).astype(BF16)
            upd = jnp.dot(ktw, jnp.concatenate([v, ones], axis=1), preferred_element_type=F32)
            c_aug = jnp.concatenate([decay, decay], axis=1) * c_aug + upd
        c_st[hd] = c_aug

    mix = (sga_ref[0].astype(F32) * jnp.dot(_unpacked(ya_ref[0]), wa_ref[...], preferred_element_type=F32)
           + sgm_ref[0].astype(F32) * jnp.dot(ym_buf[...], wm_ref[...], preferred_element_type=F32))
    y = jnp.dot(mix.astype(BF16), wo_ref[...], preferred_element_type=F32)
    o_ref[0] = x_ref[0] + ada_ref[0][2:3, :] * _rms_norm(y, npost_ref[...])


def _mixer(x, ada3, q, kt, v, so, sga, sgm, ya, gcol, grow, mh_norm_w, norm_post, wa, wm, wo):
    bsz, s, d = x.shape
    ts = min(TS_MIXER, s)
    chunk = min(MLSTM_CHUNK, ts)
    assert chunk <= LANES and ts % chunk == 0 and s % ts == 0, (s, ts, chunk)
    ng = 2 * M_HEADS
    nqk = M_HEADS * M_DK
    sel = (jnp.arange(LANES)[:, None] - M_HEADS == jnp.arange(M_HEADS * M_DV)[None, :] // M_DV).astype(BF16)
    tok_spec = lambda w=d: pl.BlockSpec((1, ts, w), lambda b, i: (b, i, 0))
    words_spec = lambda w=d: pl.BlockSpec((1, ts // 2, w), lambda b, i: (b, i, 0))
    return pl.pallas_call(
        functools.partial(_mixer_kernel, ts=ts, chunk=chunk),
        grid=(bsz, s // ts),
        in_specs=[tok_spec(),
                  pl.BlockSpec((1, 6, d), lambda b, i: (b, 0, 0)),
                  words_spec(nqk),
                  pl.BlockSpec((1, nqk // 2, ts), lambda b, i: (b, 0, i)),
                  words_spec(), tok_spec(), tok_spec(), tok_spec(), words_spec(),
                  tok_spec(LANES),
                  pl.BlockSpec((1, ng, ts), lambda b, i: (b, 0, i)),
                  _resident(sel.shape),
                  _resident((1, d)), _resident((1, d)),
                  _resident(wa.shape), _resident(wm.shape), _resident(wo.shape)],
        out_specs=tok_spec(),
        out_shape=jax.ShapeDtypeStruct((bsz, s, d), F32),
        scratch_shapes=[pltpu.VMEM((M_HEADS, M_DK, 2 * M_DV), F32),
                        pltpu.VMEM((8, LANES), F32),
                        pltpu.VMEM((M_HEADS, LANES), F32),
                        pltpu.VMEM((ts, d), BF16)],
        compiler_params=pltpu.CompilerParams(dimension_semantics=("arbitrary", "arbitrary"),
                                             vmem_limit_bytes=VMEM_LIMIT_BYTES),
        name="mixer",
    )(x, ada3, q, kt, v, so, sga, sgm, ya, gcol, grow, sel, mh_norm_w, norm_post, wa, wm, wo)


def _argmax_rows(s, exact):
    m = jnp.max(s, axis=0, keepdims=True)
    hit = s == m
    if exact:
        row = lax.broadcasted_iota(jnp.int32, s.shape, 0)
        hit = row == jnp.min(jnp.where(hit, row, s.shape[0]), axis=0, keepdims=True)
    return hit, m


def _topk_rows(arrays, k, exact):
    arrays = list(arrays)
    vals = [[] for _ in arrays]
    if exact:
        ranks = [jnp.full(s.shape, NOT_RANKED, F32) for s in arrays]
        for r in range(k):
            for i, s in enumerate(arrays):
                hit, m = _argmax_rows(s, exact)
                ranks[i] = jnp.where(hit, float(r), ranks[i])
                arrays[i] = jnp.where(hit, -jnp.inf, s)
                vals[i].append(m)
        return vals, ranks
    for r in range(k):
        for i, s in enumerate(arrays):
            hit, m = _argmax_rows(s, exact)
            arrays[i] = jnp.where(hit, -(r + 2.0) * KNOCK, s)
            vals[i].append(m)
    ranks = [jnp.where(s <= -1.5 * KNOCK, s * (-1.0 / KNOCK) - 2.0, NOT_RANKED) for s in arrays]
    return vals, ranks


def _route_tables(s1, s2, exact, topk=None):
    (v1, v2), (rank1, rank2) = topk if topk is not None else _topk_rows((s1, s2), P_TOPK, exact)
    v2_lo = jnp.concatenate(v2[0:8], axis=0)
    v2_all = jnp.concatenate(v2, axis=0)
    v1_hi = jnp.concatenate(v1[8:16], axis=0)
    cand = jnp.concatenate([v1[0] + v2_all] + [v1[j] + v2_lo for j in range(1, 8)] + [v1_hi + v2[0]], axis=0)
    cur = cand
    for _ in range(P_TOPK):
        hit, _ = _argmax_rows(cur, exact)
        cur = jnp.where(hit, -jnp.inf, cur)
    sel = jnp.where(cur == -jnp.inf, 1.0, 0.0)
    ranked = lambda rk: jnp.sum(jnp.where(rk < NOT_RANKED, 1.0, 0.0), axis=0, keepdims=True)
    picked = jnp.maximum(jnp.maximum(ranked(rank1), ranked(rank2)), jnp.sum(sel, axis=0, keepdims=True))
    lowest = jnp.minimum(jnp.min(s1, axis=0, keepdims=True), jnp.min(s2, axis=0, keepdims=True))
    tied = jnp.where((picked > float(P_TOPK)) | (lowest <= -KNOCK), 1.0, 0.0)
    top = v1[0] + v2[0]
    z = jnp.sum(sel * jnp.exp(cand - top), axis=0, keepdims=True)
    counts = [jnp.sum(sel[0:16], axis=0, keepdims=True)]
    counts += [jnp.sum(sel[8 + 8 * j:16 + 8 * j], axis=0, keepdims=True) for j in range(1, 8)]
    counts += [sel[72 + j:73 + j] for j in range(8)]
    pk = (N_KEYS // BF16_ROWS, BF16_ROWS, LANES)
    rank1b = rank1.astype(BF16).reshape(pk)
    cnt = jnp.zeros(pk, BF16)
    for j in range(P_TOPK):
        count_j = pltpu.bitcast(jnp.broadcast_to(_bf16_pair_words(counts[j]), (8, LANES)), BF16)
        cnt = jnp.where(rank1b == float(j), count_j[None], cnt)
    cnt = cnt.reshape(N_KEYS, LANES)
    p1 = GELU_HALF * jnp.exp(s1 - v1[0])
    p2 = jnp.exp(s2 - v2[0]) * (1.0 / z)
    return cnt, rank2, p1, p2, tied


def _bf16_pair_words(x):
    hi = pltpu.bitcast(x.astype(BF16).astype(F32), jnp.uint32)
    return hi | lax.shift_right_logical(hi, jnp.uint32(16))


def _route_kernel(x_ref, ada_ref, npre_ref, wqt_ref, sk_ref, h2t_ref, cnt_ref, rk2_ref, p1_ref, p2_ref, qt_buf,
                  *, tm):
    ada = ada_ref[0]
    h2 = _rms_norm(x_ref[...], npre_ref[...]) * (1.0 + ada[4:5, :]) + ada[3:4, :]
    h2t = h2.T.astype(BF16)
    h2t_ref[...] = _packed_words(h2t)
    qt_buf[...] = jnp.dot(wqt_ref[...], h2t, preferred_element_type=F32).astype(BF16)

    def head(hd, carry):
        r0 = pl.multiple_of(hd * (2 * N_KEYS), 2 * N_KEYS)
        s1 = jnp.dot(sk_ref[hd, 0], qt_buf[pl.ds(r0, N_KEYS), :], preferred_element_type=F32)
        s2 = jnp.dot(sk_ref[hd, 1], qt_buf[pl.ds(r0 + N_KEYS, N_KEYS), :], preferred_element_type=F32)

        def tables(lc, exact, topk=None):
            sl = slice(lc * LANES, (lc + 1) * LANES)
            cnt, rk2, p1, p2, tied = _route_tables(s1[:, sl], s2[:, sl], exact, topk)
            cnt_ref[hd, lc] = _bf16_pair_words(cnt)
            rk2_ref[hd, lc] = _packed_words(rk2.astype(BF16))
            p1_ref[hd, lc] = _bf16_pair_words(p1)
            p2_ref[hd, lc] = _packed_words(p2.astype(BF16))
            return tied

        nlc = tm // LANES
        slabs = [s[:, lc * LANES:(lc + 1) * LANES] for lc in range(nlc) for s in (s1, s2)]
        vals, ranks = _topk_rows(slabs, P_TOPK, False)
        for lc in range(nlc):
            tied = tables(lc, False, ((vals[2 * lc], vals[2 * lc + 1]), (ranks[2 * lc], ranks[2 * lc + 1])))

            @pl.when(jnp.max(tied) > 0.0)
            def _():
                tables(lc, exact=True)
        return carry

    lax.fori_loop(0, P_HEADS, head, 0)


def _route(x1, ada3, norm_pre, wqt, subkeys):
    t, d = x1.shape
    s = t // ada3.shape[0]
    tm = min(TM_ROUTE, s)
    assert s % tm == 0 and tm % LANES == 0, (s, tm)
    tab = lambda rows: jax.ShapeDtypeStruct((P_HEADS, t // LANES, rows, LANES), jnp.uint32)
    tab_spec = lambda rows: pl.BlockSpec((P_HEADS, tm // LANES, rows, LANES), lambda i: (0, i, 0, 0))
    k1, k2 = N_KEYS, N_KEYS // 2
    return pl.pallas_call(
        functools.partial(_route_kernel, tm=tm),
        grid=(t // tm,),
        in_specs=[pl.BlockSpec((tm, d), lambda i: (i, 0)),
                  pl.BlockSpec((1, 6, d), lambda i: ((i * tm) // s, 0, 0)),
                  _resident((1, d)),
                  _resident(wqt.shape),
                  _resident(subkeys.shape)],
        out_specs=[pl.BlockSpec((d // 2, tm), lambda i: (0, i)),
                   tab_spec(k1), tab_spec(k2), tab_spec(k1), tab_spec(k2)],
        out_shape=[jax.ShapeDtypeStruct((d // 2, t), jnp.uint32), tab(k1), tab(k2), tab(k1), tab(k2)],
        scratch_shapes=[pltpu.VMEM((wqt.shape[0], tm), BF16)],
        compiler_params=pltpu.CompilerParams(dimension_semantics=("arbitrary",),
                                             vmem_limit_bytes=VMEM_LIMIT_BYTES),
        name="route",
    )(x1, ada3, norm_pre, wqt, subkeys)


def _expert_words_kernel(u_ref, v_ref, uw_ref, vtw_ref):
    uw_ref[...] = _packed_words(u_ref[...].astype(BF16))
    vtw_ref[...] = _packed_words(v_ref[...].T.astype(BF16))


def _expert_words(u, v):
    ne, d = u.shape
    te = min(TE_WORDS, ne)
    return pl.pallas_call(
        _expert_words_kernel,
        grid=(ne // te,),
        in_specs=[pl.BlockSpec((te, d), lambda j: (j, 0)), pl.BlockSpec((te, d), lambda j: (j, 0))],
        out_specs=[pl.BlockSpec((te // 2, d), lambda j: (j, 0)), pl.BlockSpec((d // 2, te), lambda j: (0, j))],
        out_shape=[jax.ShapeDtypeStruct((ne // 2, d), jnp.uint32), jax.ShapeDtypeStruct((d // 2, ne), jnp.uint32)],
        compiler_params=pltpu.CompilerParams(vmem_limit_bytes=VMEM_LIMIT_BYTES),
        name="expert_words",
    )(u, v)


def _gelu_tanh_x2(z):
    c = 0.7978845608028654
    return z + z * jnp.tanh(z * (c + (c * 0.044715) * (z * z)))


def _peer_kernel(h2t_ref, u_ref, vt_ref, cnt_ref, rk2_ref, p1_ref, p2_ref, x_ref, ada_ref, npost_ref, o_ref,
                 acc, z_buf, a_buf, *, tm, te):
    j = pl.program_id(1)

    @pl.when(j == 0)
    def _():
        acc[...] = jnp.zeros(acc.shape, F32)

    z_buf[...] = jnp.dot(_unpacked(u_ref[...]), _unpacked(h2t_ref[...]), preferred_element_type=F32).astype(BF16)

    pk = (N_KEYS // BF16_ROWS, BF16_ROWS, LANES)
    for lc in range(tm // LANES):
        sl = slice(lc * LANES, (lc + 1) * LANES)
        for a0 in range(0, te // N_KEYS, PEER_GROUP):
            gates = [None] * PEER_GROUP
            for hd in range(P_HEADS):
                rk2 = _unpacked(rk2_ref[hd, lc]).reshape(pk)
                p2 = _unpacked(p2_ref[hd, lc]).reshape(pk)
                for g in range(PEER_GROUP):
                    al = a0 + g
                    cnt_a = pltpu.bitcast(jnp.broadcast_to(cnt_ref[hd, lc, al:al + 1, :], (8, LANES)), BF16)
                    p1_a = pltpu.bitcast(jnp.broadcast_to(p1_ref[hd, lc, al:al + 1, :], (8, LANES)), BF16)
                    term = jnp.where(rk2 < cnt_a[None], p2 * p1_a[None], jnp.zeros((), BF16))
                    gates[g] = term if gates[g] is None else gates[g] + term
            for g in range(PEER_GROUP):
                rows = slice((a0 + g) * N_KEYS, (a0 + g + 1) * N_KEYS)
                a_buf[rows, sl] = _gelu_tanh_x2(z_buf[rows, sl]) * gates[g].reshape(N_KEYS, LANES)

    acc[...] += jnp.dot(_unpacked(vt_ref[...]), a_buf[...], preferred_element_type=F32)

    @pl.when(j == pl.num_programs(1) - 1)
    def _():
        y = acc[...].T
        o_ref[...] = x_ref[...] + ada_ref[0][5:6, :] * _rms_norm(y, npost_ref[...])


def _peer(h2t, u_bf, vt_bf, cnt, rk2, p1, p2, x1, ada3, norm_post):
    t, d = x1.shape
    s = t // ada3.shape[0]
    ne = 2 * u_bf.shape[0]
    tm = min(TM_PEER, s)
    te = TE_PEER
    assert s % tm == 0 and tm % LANES == 0 and ne % te == 0, (s, tm, ne, te)
    assert te % (8 * N_KEYS) == 0 and (te // N_KEYS) % PEER_GROUP == 0, te
    tab2_spec = pl.BlockSpec((P_HEADS, tm // LANES, N_KEYS // 2, LANES), lambda i, j: (0, i, 0, 0))
    tab1_spec = pl.BlockSpec((P_HEADS, tm // LANES, te // N_KEYS, LANES), lambda i, j: (0, i, j, 0))
    return pl.pallas_call(
        functools.partial(_peer_kernel, tm=tm, te=te),
        grid=(t // tm, ne // te),
        in_specs=[pl.BlockSpec((d // 2, tm), lambda i, j: (0, i)),
                  pl.BlockSpec((te // 2, d), lambda i, j: (j, 0)),
                  pl.BlockSpec((d // 2, te), lambda i, j: (0, j)),
                  tab1_spec, tab2_spec, tab1_spec, tab2_spec,
                  pl.BlockSpec((tm, d), lambda i, j: (i, 0)),
                  pl.BlockSpec((1, 6, d), lambda i, j: ((i * tm) // s, 0, 0)),
                  pl.BlockSpec((1, d), lambda i, j: (0, 0))],
        out_specs=pl.BlockSpec((tm, d), lambda i, j: (i, 0)),
        out_shape=jax.ShapeDtypeStruct((t, d), F32),
        scratch_shapes=[pltpu.VMEM((d, tm), F32),
                        pltpu.VMEM((te, tm), BF16),
                        pltpu.VMEM((te, tm), BF16)],
        compiler_params=pltpu.CompilerParams(dimension_semantics=("arbitrary", "arbitrary"),
                                             vmem_limit_bytes=VMEM_LIMIT_BYTES),
        name="peer",
    )(h2t, u_bf, vt_bf, cnt, rk2, p1, p2, x1, ada3, norm_post)


def _layer(x, c, w_ada, b_ada, norm1_pre, norm1_post, w_in, conv_a_w, conv_qk_w, b_igate, b_fgate, mh_norm_w,
           w_branch_a, w_branch_m, w_out, norm2_pre, norm2_post, peer_wq, peer_subkeys, peer_u, peer_v):
    bsz, s, d = x.shape
    ng = 2 * M_HEADS
    ada3 = _ada(c, w_ada, b_ada).reshape(bsz, 6, d)

    o_if = 3 * d + 2 * M_HEADS * M_DK + 2 * d
    w_main = jnp.concatenate([w_in[:, :o_if], w_in[:, o_if + ng:]], axis=1).astype(BF16)
    w_if = w_in[:, o_if:o_if + ng]
    wif_col = jnp.pad(w_if, ((0, 0), (0, LANES - ng))).astype(BF16)
    wif_row = w_if.T.astype(BF16)
    b_if = jnp.concatenate([b_igate, b_fgate])
    bif_col = jnp.pad(b_if, (0, LANES - ng)).reshape(1, LANES)
    bif_row = b_if.reshape(ng, 1)

    ya, q, kt, v, so, sga, sgm, gcol, grow = _inproj(
        x, ada3, norm1_pre.reshape(1, d), w_main, wif_col, wif_row, bif_col, bif_row, conv_a_w, conv_qk_w)
    x1 = _mixer(x, ada3, q, kt, v, so, sga, sgm, ya, gcol, grow, mh_norm_w.reshape(1, d),
                norm1_post.reshape(1, d), w_branch_a.astype(BF16), w_branch_m.astype(BF16), w_out.astype(BF16))

    x1f = x1.reshape(bsz * s, d)
    h2t, cnt, rk2, p1, p2 = _route(x1f, ada3, norm2_pre.reshape(1, d), peer_wq.T.astype(BF16),
                                   peer_subkeys.astype(BF16))
    u_words, vt_words = _expert_words(peer_u, peer_v)
    out = _peer(h2t, u_words, vt_words, cnt, rk2, p1, p2, x1f, ada3, norm2_post.reshape(1, d))
    return out.reshape(bsz, s, d)


def kernel(x, c, w_ada, b_ada, norm1_pre, norm1_post, w_in, conv_a_w, conv_qk_w, b_igate, b_fgate, mh_norm_w, w_branch_a, w_branch_m, w_out, norm2_pre, norm2_post, peer_wq, peer_subkeys, peer_u, peer_v):
    for l in range(w_ada.shape[0]):
        x = _layer(x, c, w_ada[l], b_ada[l], norm1_pre[l], norm1_post[l], w_in[l], conv_a_w[l], conv_qk_w[l],
                   b_igate[l], b_fgate[l], mh_norm_w[l], w_branch_a[l], w_branch_m[l], w_out[l], norm2_pre[l],
                   norm2_post[l], peer_wq[l], peer_subkeys[l], peer_u[l], peer_v[l])
    return x
```

```python
import functools

import jax
import jax.numpy as jnp
from jax import lax
from jax.experimental import pallas as pl
from jax.experimental.pallas import tpu as pltpu

F32 = jnp.float32
BF16 = jnp.bfloat16

EPS = 1e-6
M_HEADS = 8
M_DK = 64
M_DV = 128
CONV_K = 3
QK_CONV_K = 4
P_HEADS = 8
N_KEYS = 128
P_TOPK = 16
HALO = 8
LANES = 128
BF16_ROWS = 16
NOT_RANKED = 99.0
KNOCK = 2.0 ** 100
GELU_HALF = 0.5

VMEM_LIMIT_BYTES = 56 * 1024 * 1024

MLSTM_CHUNK = 128
TS_INPROJ = 256
TS_MIXER = 512
TM_ROUTE = 256
TM_PEER = 1024
TE_PEER = 1024
TE_WORDS = 1024


def _resident(shape):
    nd = len(shape)
    return pl.BlockSpec(shape, lambda *_: (0,) * nd, pipeline_mode=pl.Buffered(1))


def _rms_norm(x, w):
    return x * lax.rsqrt(jnp.mean(x * x, axis=-1, keepdims=True) + EPS) * w


def _sigmoid(x):
    return 1.0 / (1.0 + jnp.exp(-x))


def _packed_words(x):
    return pltpu.bitcast(x, jnp.uint32)


def _unpacked(words):
    return pltpu.bitcast(words, BF16)


def _ada_kernel(c_ref, w_ref, b_ref, o_ref):
    c = c_ref[...]
    s = c * _sigmoid(c)
    o_ref[...] = jnp.dot(s.astype(BF16), w_ref[...].astype(BF16), preferred_element_type=F32) + b_ref[...]


def _ada(c, w_ada, b_ada):
    bsz, d = c.shape
    n = w_ada.shape[1]
    return pl.pallas_call(
        _ada_kernel,
        grid=(n // d,),
        in_specs=[pl.BlockSpec((bsz, d), lambda j: (0, 0)),
                  pl.BlockSpec((d, d), lambda j: (0, j)),
                  pl.BlockSpec((1, d), lambda j: (0, j))],
        out_specs=pl.BlockSpec((bsz, d), lambda j: (0, j)),
        out_shape=jax.ShapeDtypeStruct((bsz, n), F32),
        compiler_params=pltpu.CompilerParams(vmem_limit_bytes=VMEM_LIMIT_BYTES),
        name="ada",
    )(c, w_ada, b_ada.reshape(1, n))


def _inproj_kernel(x_ref, ada_ref, npre_ref, w_ref, wifc_ref, wifr_ref, bifc_ref, bifr_ref, cwa_ref, cwqk_ref,
                   ya_ref, q_ref, kt_ref, v_ref, so_ref, sga_ref, sgm_ref, gcol_ref, grow_ref,
                   ubuf, qkbuf, *, ts, d):
    @pl.when(pl.program_id(1) == 0)
    def _():
        ubuf[0:HALO, :] = jnp.zeros((HALO, d), F32)
        qkbuf[0:HALO, :] = jnp.zeros((HALO, d), F32)

    x = x_ref[0]
    ada = ada_ref[0]
    h = _rms_norm(x, npre_ref[...]) * (1.0 + ada[1:2, :]) + ada[0:1, :]
    hb = h.astype(BF16)

    def proj(k):
        return jnp.dot(hb, w_ref[:, k * d:(k + 1) * d], preferred_element_type=F32)

    u = proj(0) * proj(2)
    ubuf[HALO:HALO + ts, :] = u
    conv = u * cwa_ref[CONV_K - 1:CONV_K, :]
    for j in range(CONV_K - 1):
        off = HALO - (CONV_K - 1) + j
        conv = conv + ubuf[off:off + ts, :] * cwa_ref[j:j + 1, :]
    ya_ref[0] = _packed_words((proj(1) * conv).astype(BF16))
    ubuf[0:HALO, :] = ubuf[ts:ts + HALO, :]

    qk = proj(3)
    qkbuf[HALO:HALO + ts, :] = qk
    cq = qk * cwqk_ref[QK_CONV_K - 1:QK_CONV_K, :]
    for j in range(QK_CONV_K - 1):
        off = HALO - (QK_CONV_K - 1) + j
        cq = cq + qkbuf[off:off + ts, :] * cwqk_ref[j:j + 1, :]
    qkbuf[0:HALO, :] = qkbuf[ts:ts + HALO, :]
    cq = cq * _sigmoid(cq)
    nqk = M_HEADS * M_DK
    q_ref[0] = _packed_words((cq[:, :nqk] * (M_DK ** -0.5)).astype(BF16))
    kt_ref[0] = _packed_words(cq[:, nqk:].T.astype(BF16))

    v_ref[0] = _packed_words(proj(4).astype(BF16))
    so_ref[0] = _sigmoid(proj(5)).astype(BF16)
    sga_ref[0] = _sigmoid(proj(6)).astype(BF16)
    sgm_ref[0] = _sigmoid(proj(7)).astype(BF16)

    gcol_ref[0] = jnp.dot(hb, wifc_ref[...], preferred_element_type=F32) + bifc_ref[...]
    grow_ref[0] = lax.dot_general(wifr_ref[...], hb, (((1,), (1,)), ((), ())),
                                  preferred_element_type=F32) + bifr_ref[...]


def _inproj(x, ada3, norm_pre, w_main, wif_col, wif_row, bif_col, bif_row, conv_a_w, conv_qk_w):
    bsz, s, d = x.shape
    ts = min(TS_INPROJ, s)
    assert s % ts == 0 and d == 2 * M_HEADS * M_DK == M_HEADS * M_DV, (s, ts, d)
    ng = 2 * M_HEADS
    nqk = M_HEADS * M_DK
    tok = lambda dt, w=d: jax.ShapeDtypeStruct((bsz, s, w), dt)
    tok_spec = lambda w=d: pl.BlockSpec((1, ts, w), lambda b, i: (b, i, 0))
    words = lambda w=d: jax.ShapeDtypeStruct((bsz, s // 2, w), jnp.uint32)
    words_spec = lambda w=d: pl.BlockSpec((1, ts // 2, w), lambda b, i: (b, i, 0))
    return pl.pallas_call(
        functools.partial(_inproj_kernel, ts=ts, d=d),
        grid=(bsz, s // ts),
        in_specs=[tok_spec(),
                  pl.BlockSpec((1, 6, d), lambda b, i: (b, 0, 0)),
                  _resident((1, d)),
                  _resident(w_main.shape),
                  _resident(wif_col.shape),
                  _resident(wif_row.shape),
                  _resident(bif_col.shape),
                  _resident(bif_row.shape),
                  _resident(conv_a_w.shape),
                  _resident(conv_qk_w.shape)],
        out_specs=[words_spec(), words_spec(nqk),
                   pl.BlockSpec((1, nqk // 2, ts), lambda b, i: (b, 0, i)),
                   words_spec(), tok_spec(), tok_spec(), tok_spec(),
                   tok_spec(LANES),
                   pl.BlockSpec((1, ng, ts), lambda b, i: (b, 0, i))],
        out_shape=[words(), words(nqk),
                   jax.ShapeDtypeStruct((bsz, nqk // 2, s), jnp.uint32),
                   words(), tok(BF16), tok(BF16), tok(BF16),
                   tok(F32, LANES),
                   jax.ShapeDtypeStruct((bsz, ng, s), F32)],
        scratch_shapes=[pltpu.VMEM((HALO + ts, d), F32), pltpu.VMEM((HALO + ts, d), F32)],
        compiler_params=pltpu.CompilerParams(dimension_semantics=("arbitrary", "arbitrary"),
                                             vmem_limit_bytes=VMEM_LIMIT_BYTES),
        name="inproj",
    )(x, ada3, norm_pre, w_main, wif_col, wif_row, bif_col, bif_row, conv_a_w, conv_qk_w)


def _log_sigmoid(x):
    return jnp.minimum(x, 0.0) - jnp.log(1.0 + jnp.exp(-jnp.abs(x)))


def _chunk_scan(x, axis, chunk, op, identity):
    pos = lax.broadcasted_iota(jnp.int32, x.shape, axis) % chunk
    k = 1
    while k < chunk:
        x = op(x, jnp.where(pos >= k, pltpu.roll(x, k, axis), identity))
        k *= 2
    return x


def _bf16_terms(x):
    hi = x.astype(BF16)
    r = x - hi.astype(F32)
    mid = r.astype(BF16)
    lo = (r - mid.astype(F32)).astype(BF16)
    return hi, mid, lo


def _spread_heads(x, sel):
    return sum(jnp.dot(t, sel, preferred_element_type=F32) for t in _bf16_terms(x))


def _mixer_kernel(x_ref, ada_ref, q_ref, kt_ref, v_ref, so_ref, sga_ref, sgm_ref, ya_ref, gcol_ref, grow_ref,
                  sel_ref, mhw_ref, npost_ref, wa_ref, wm_ref, wo_ref, o_ref,
                  c_st, mcol_st, mrow_st, ym_buf, *, ts, chunk):
    first = pl.program_id(1) == 0

    @pl.when(first)
    def _():
        c_st[...] = jnp.zeros(c_st.shape, F32)
        mcol_st[...] = jnp.zeros(mcol_st.shape, F32)
        mrow_st[...] = jnp.zeros(mrow_st.shape, F32)

    nc = ts // chunk
    H = M_HEADS
    neg_inf = -jnp.inf

    gcol = gcol_ref[0]
    b_col = _chunk_scan(_log_sigmoid(gcol), 0, chunk, jnp.add, 0.0)
    r_col = pltpu.roll(gcol, H, 1) - b_col
    cm_col = _chunk_scan(r_col, 0, chunk, jnp.maximum, neg_inf)
    m = mcol_st[0:1, :]
    m_rows, decay_rows = [], []
    for c in range(nc):
        last = (c + 1) * chunk - 1
        b_l, cm_l = b_col[last:last + 1, :], cm_col[last:last + 1, :]
        m_next = b_l + jnp.maximum(m, cm_l)
        m_rows.append(m)
        decay_rows.append(jnp.exp(b_l + m - m_next))
        m = m_next
    mcol_st[...] = jnp.broadcast_to(m, mcol_st.shape)
    pad = jnp.zeros((8 - (2 * nc) % 8, LANES), F32)
    spread = _spread_heads(jnp.concatenate([b_col, cm_col] + m_rows + decay_rows + [pad], axis=0), sel_ref[...])
    bc_all, cm_all, sc_all = spread[0:ts], spread[ts:2 * ts], spread[2 * ts:]

    grow = grow_ref[0]
    i_row = grow[0:H, :]
    b_row = _chunk_scan(_log_sigmoid(grow[H:2 * H, :]), 1, chunk, jnp.add, 0.0)
    r_row = i_row - b_row
    cm_row = _chunk_scan(r_row, 1, chunk, jnp.maximum, neg_inf)
    mr = mrow_st[:, 0:1]
    ws_rows = []
    for c in range(nc):
        last = (c + 1) * chunk - 1
        b_l, cm_l = b_row[:, last:last + 1], cm_row[:, last:last + 1]
        m_next = b_l + jnp.maximum(mr, cm_l)
        ws_rows.append(jnp.exp(b_l + r_row[:, c * chunk:(c + 1) * chunk] - m_next))
        mr = m_next
    mrow_st[...] = jnp.broadcast_to(mr, mrow_st.shape)

    tril = (lax.broadcasted_iota(jnp.int32, (chunk, chunk), 0)
            >= lax.broadcasted_iota(jnp.int32, (chunk, chunk), 1))
    ones = jnp.ones((chunk, M_DV), BF16)

    for hd in range(H):
        hl = slice(hd * M_DV, (hd + 1) * M_DV)
        c_aug = c_st[hd]
        for c in range(nc):
            rows = slice(c * chunk, (c + 1) * chunk)
            wrows = slice(c * chunk // 2, (c + 1) * chunk // 2)
            q = _unpacked(q_ref[0, wrows, hd * M_DK:(hd + 1) * M_DK])
            kt = _unpacked(kt_ref[0, hd * M_DK // 2:(hd + 1) * M_DK // 2, rows])
            v = _unpacked(v_ref[0, wrows, hl])
            bc = bc_all[rows, hl]
            m_prev = sc_all[c:c + 1, hl]
            decay = sc_all[nc + c:nc + c + 1, hl]
            m_t = bc + jnp.maximum(m_prev, cm_all[rows, hl])
            dlog = jnp.where(tril, bc[:, :chunk] + r_row[hd:hd + 1, rows], neg_inf)
            s_qk = jnp.dot(q, kt, preferred_element_type=F32) * jnp.exp(dlog - m_t[:, :chunk])
            a_inter = jnp.exp(bc + m_prev - m_t)
            qc = jnp.dot(q, c_aug.astype(BF16), preferred_element_type=F32)
            num = jnp.dot(s_qk.astype(BF16), v, preferred_element_type=F32) + a_inter * qc[:, :M_DV]
            den = jnp.sum(s_qk, axis=1, keepdims=True) + a_inter * qc[:, M_DV:]
            h = num / jnp.maximum(jnp.abs(den), jnp.exp(-m_t))
            hn = _rms_norm(h, mhw_ref[:, hl])
            ym_buf[rows, hl] = (so_ref[0, rows, hl].astype(F32) * hn).astype(BF16)
            ktw = (kt.astype(F32) * ws_rows[c][hd:hd + 1, :]).astype(BF16)
            upd = jnp.dot(ktw, jnp.concatenate([v, ones], axis=1), preferred_element_type=F32)
            c_aug = jnp.concatenate([decay, decay], axis=1) * c_aug + upd
        c_st[hd] = c_aug

    mix = (sga_ref[0].astype(F32) * jnp.dot(_unpacked(ya_ref[0]), wa_ref[...], preferred_element_type=F32)
           + sgm_ref[0].astype(F32) * jnp.dot(ym_buf[...], wm_ref[...], preferred_element_type=F32))
    y = jnp.dot(mix.astype(BF16), wo_ref[...], preferred_element_type=F32)
    o_ref[0] = x_ref[0] + ada_ref[0][2:3, :] * _rms_norm(y, npost_ref[...])


def _mixer(x, ada3, q, kt, v, so, sga, sgm, ya, gcol, grow, mh_norm_w, norm_post, wa, wm, wo):
    bsz, s, d = x.shape
    ts = min(TS_MIXER, s)
    chunk = min(MLSTM_CHUNK, ts)
    assert chunk <= LANES and ts % chunk == 0 and s % ts == 0, (s, ts, chunk)
    ng = 2 * M_HEADS
    nqk = M_HEADS * M_DK
    sel = (jnp.arange(LANES)[:, None] - M_HEADS == jnp.arange(M_HEADS * M_DV)[None, :] // M_DV).astype(BF16)
    tok_spec = lambda w=d: pl.BlockSpec((1, ts, w), lambda b, i: (b, i, 0))
    words_spec = lambda w=d: pl.BlockSpec((1, ts // 2, w), lambda b, i: (b, i, 0))
    return pl.pallas_call(
        functools.partial(_mixer_kernel, ts=ts, chunk=chunk),
        grid=(bsz, s // ts),
        in_specs=[tok_spec(),
                  pl.BlockSpec((1, 6, d), lambda b, i: (b, 0, 0)),
                  words_spec(nqk),
                  pl.BlockSpec((1, nqk // 2, ts), lambda b, i: (b, 0, i)),
                  words_spec(), tok_spec(), tok_spec(), tok_spec(), words_spec(),
                  tok_spec(LANES),
                  pl.BlockSpec((1, ng, ts), lambda b, i: (b, 0, i)),
                  _resident(sel.shape),
                  _resident((1, d)), _resident((1, d)),
                  _resident(wa.shape), _resident(wm.shape), _resident(wo.shape)],
        out_specs=tok_spec(),
        out_shape=jax.ShapeDtypeStruct((bsz, s, d), F32),
        scratch_shapes=[pltpu.VMEM((M_HEADS, M_DK, 2 * M_DV), F32),
                        pltpu.VMEM((8, LANES), F32),
                        pltpu.VMEM((M_HEADS, LANES), F32),
                        pltpu.VMEM((ts, d), BF16)],
        compiler_params=pltpu.CompilerParams(dimension_semantics=("arbitrary", "arbitrary"),
                                             vmem_limit_bytes=VMEM_LIMIT_BYTES),
        name="mixer",
    )(x, ada3, q, kt, v, so, sga, sgm, ya, gcol, grow, sel, mh_norm_w, norm_post, wa, wm, wo)


def _argmax_rows(s, exact):
    m = jnp.max(s, axis=0, keepdims=True)
    hit = s == m
    if exact:
        row = lax.broadcasted_iota(jnp.int32, s.shape, 0)
        hit = row == jnp.min(jnp.where(hit, row, s.shape[0]), axis=0, keepdims=True)
    return hit, m


def _topk_rows(arrays, k, exact):
    arrays = list(arrays)
    vals = [[] for _ in arrays]
    if exact:
        ranks = [jnp.full(s.shape, NOT_RANKED, F32) for s in arrays]
        for r in range(k):
            for i, s in enumerate(arrays):
                hit, m = _argmax_rows(s, exact)
                ranks[i] = jnp.where(hit, float(r), ranks[i])
                arrays[i] = jnp.where(hit, -jnp.inf, s)
                vals[i].append(m)
        return vals, ranks
    for r in range(k):
        for i, s in enumerate(arrays):
            hit, m = _argmax_rows(s, exact)
            arrays[i] = jnp.where(hit, -(r + 2.0) * KNOCK, s)
            vals[i].append(m)
    ranks = [jnp.where(s <= -1.5 * KNOCK, s * (-1.0 / KNOCK) - 2.0, NOT_RANKED) for s in arrays]
    return vals, ranks


def _route_tables(s1, s2, exact, topk=None):
    (v1, v2), (rank1, rank2) = topk if topk is not None else _topk_rows((s1, s2), P_TOPK, exact)
    v2_lo = jnp.concatenate(v2[0:8], axis=0)
    v2_all = jnp.concatenate(v2, axis=0)
    v1_hi = jnp.concatenate(v1[8:16], axis=0)
    cand = jnp.concatenate([v1[0] + v2_all] + [v1[j] + v2_lo for j in range(1, 8)] + [v1_hi + v2[0]], axis=0)
    cur = cand
    for _ in range(P_TOPK):
        hit, _ = _argmax_rows(cur, exact)
        cur = jnp.where(hit, -jnp.inf, cur)
    sel = jnp.where(cur == -jnp.inf, 1.0, 0.0)
    ranked = lambda rk: jnp.sum(jnp.where(rk < NOT_RANKED, 1.0, 0.0), axis=0, keepdims=True)
    picked = jnp.maximum(jnp.maximum(ranked(rank1), ranked(rank2)), jnp.sum(sel, axis=0, keepdims=True))
    lowest = jnp.minimum(jnp.min(s1, axis=0, keepdims=True), jnp.min(s2, axis=0, keepdims=True))
    tied = jnp.where((picked > float(P_TOPK)) | (lowest <= -KNOCK), 1.0, 0.0)
    top = v1[0] + v2[0]
    z = jnp.sum(sel * jnp.exp(cand - top), axis=0, keepdims=True)
    counts = [jnp.sum(sel[0:16], axis=0, keepdims=True)]
    counts += [jnp.sum(sel[8 + 8 * j:16 + 8 * j], axis=0, keepdims=True) for j in range(1, 8)]
    counts += [sel[72 + j:73 + j] for j in range(8)]
    pk = (N_KEYS // BF16_ROWS, BF16_ROWS, LANES)
    rank1b = rank1.astype(BF16).reshape(pk)
    cnt = jnp.zeros(pk, BF16)
    for j in range(P_TOPK):
        count_j = pltpu.bitcast(jnp.broadcast_to(_bf16_pair_words(counts[j]), (8, LANES)), BF16)
        cnt = jnp.where(rank1b == float(j), count_j[None], cnt)
    cnt = cnt.reshape(N_KEYS, LANES)
    p1 = GELU_HALF * jnp.exp(s1 - v1[0])
    p2 = jnp.exp(s2 - v2[0]) * (1.0 / z)
    return cnt, rank2, p1, p2, tied


def _bf16_pair_words(x):
    hi = pltpu.bitcast(x.astype(BF16).astype(F32), jnp.uint32)
    return hi | lax.shift_right_logical(hi, jnp.uint32(16))


def _route_kernel(x_ref, ada_ref, npre_ref, wqt_ref, sk_ref, h2t_ref, cnt_ref, rk2_ref, p1_ref, p2_ref, qt_buf,
                  *, tm):
    ada = ada_ref[0]
    h2 = _rms_norm(x_ref[...], npre_ref[...]) * (1.0 + ada[4:5, :]) + ada[3:4, :]
    h2t = h2.T.astype(BF16)
    h2t_ref[...] = _packed_words(h2t)
    qt_buf[...] = jnp.dot(wqt_ref[...], h2t, preferred_element_type=F32).astype(BF16)

    def head(hd, carry):
        r0 = pl.multiple_of(hd * (2 * N_KEYS), 2 * N_KEYS)
        s1 = jnp.dot(sk_ref[hd, 0], qt_buf[pl.ds(r0, N_KEYS), :], preferred_element_type=F32)
        s2 = jnp.dot(sk_ref[hd, 1], qt_buf[pl.ds(r0 + N_KEYS, N_KEYS), :], preferred_element_type=F32)

        def tables(lc, exact, topk=None):
            sl = slice(lc * LANES, (lc + 1) * LANES)
            cnt, rk2, p1, p2, tied = _route_tables(s1[:, sl], s2[:, sl], exact, topk)
            cnt_ref[hd, lc] = _bf16_pair_words(cnt)
            rk2_ref[hd, lc] = _packed_words(rk2.astype(BF16))
            p1_ref[hd, lc] = _bf16_pair_words(p1)
            p2_ref[hd, lc] = _packed_words(p2.astype(BF16))
            return tied

        nlc = tm // LANES
        slabs = [s[:, lc * LANES:(lc + 1) * LANES] for lc in range(nlc) for s in (s1, s2)]
        vals, ranks = _topk_rows(slabs, P_TOPK, False)
        for lc in range(nlc):
            tied = tables(lc, False, ((vals[2 * lc], vals[2 * lc + 1]), (ranks[2 * lc], ranks[2 * lc + 1])))

            @pl.when(jnp.max(tied) > 0.0)
            def _():
                tables(lc, exact=True)
        return carry

    lax.fori_loop(0, P_HEADS, head, 0)


def _route(x1, ada3, norm_pre, wqt, subkeys):
    t, d = x1.shape
    s = t // ada3.shape[0]
    tm = min(TM_ROUTE, s)
    assert s % tm == 0 and tm % LANES == 0, (s, tm)
    tab = lambda rows: jax.ShapeDtypeStruct((P_HEADS, t // LANES, rows, LANES), jnp.uint32)
    tab_spec = lambda rows: pl.BlockSpec((P_HEADS, tm // LANES, rows, LANES), lambda i: (0, i, 0, 0))
    k1, k2 = N_KEYS, N_KEYS // 2
    return pl.pallas_call(
        functools.partial(_route_kernel, tm=tm),
        grid=(t // tm,),
        in_specs=[pl.BlockSpec((tm, d), lambda i: (i, 0)),
                  pl.BlockSpec((1, 6, d), lambda i: ((i * tm) // s, 0, 0)),
                  _resident((1, d)),
                  _resident(wqt.shape),
                  _resident(subkeys.shape)],
        out_specs=[pl.BlockSpec((d // 2, tm), lambda i: (0, i)),
                   tab_spec(k1), tab_spec(k2), tab_spec(k1), tab_spec(k2)],
        out_shape=[jax.ShapeDtypeStruct((d // 2, t), jnp.uint32), tab(k1), tab(k2), tab(k1), tab(k2)],
        scratch_shapes=[pltpu.VMEM((wqt.shape[0], tm), BF16)],
        compiler_params=pltpu.CompilerParams(dimension_semantics=("arbitrary",),
                                             vmem_limit_bytes=VMEM_LIMIT_BYTES),
        name="route",
    )(x1, ada3, norm_pre, wqt, subkeys)


def _expert_words_kernel(u_ref, v_ref, uw_ref, vtw_ref):
    uw_ref[...] = _packed_words(u_ref[...].astype(BF16))
    vtw_ref[...] = _packed_words(v_ref[...].T.astype(BF16))


def _expert_words(u, v):
    ne, d = u.shape
    te = min(TE_WORDS, ne)
    return pl.pallas_call(
        _expert_words_kernel,
        grid=(ne // te,),
        in_specs=[pl.BlockSpec((te, d), lambda j: (j, 0)), pl.BlockSpec((te, d), lambda j: (j, 0))],
        out_specs=[pl.BlockSpec((te // 2, d), lambda j: (j, 0)), pl.BlockSpec((d // 2, te), lambda j: (0, j))],
        out_shape=[jax.ShapeDtypeStruct((ne // 2, d), jnp.uint32), jax.ShapeDtypeStruct((d // 2, ne), jnp.uint32)],
        compiler_params=pltpu.CompilerParams(vmem_limit_bytes=VMEM_LIMIT_BYTES),
        name="expert_words",
    )(u, v)


def _gelu_tanh_x2(z):
    c = 0.7978845608028654
    return z + z * jnp.tanh(z * (c + (c * 0.044715) * (z * z)))


def _peer_kernel(h2t_ref, u_ref, vt_ref, cnt_ref, rk2_ref, p1_ref, p2_ref, x_ref, ada_ref, npost_ref, o_ref,
                 acc, z_buf, a_buf, *, tm, te):
    j = pl.program_id(1)

    @pl.when(j == 0)
    def _():
        acc[...] = jnp.zeros(acc.shape, F32)

    z_buf[...] = jnp.dot(_unpacked(u_ref[...]), _unpacked(h2t_ref[...]), preferred_element_type=F32).astype(BF16)

    pk = (N_KEYS // BF16_ROWS, BF16_ROWS, LANES)
    for lc in range(tm // LANES):
        sl = slice(lc * LANES, (lc + 1) * LANES)
        for al in range(te // N_KEYS):
            gate = None
            for hd in range(P_HEADS):
                cnt_a = pltpu.bitcast(jnp.broadcast_to(cnt_ref[hd, lc, al:al + 1, :], (8, LANES)), BF16)
                p1_a = pltpu.bitcast(jnp.broadcast_to(p1_ref[hd, lc, al:al + 1, :], (8, LANES)), BF16)
                hit = _unpacked(rk2_ref[hd, lc]).reshape(pk) < cnt_a[None]
                term = jnp.where(hit, _unpacked(p2_ref[hd, lc]).reshape(pk) * p1_a[None], jnp.zeros((), BF16))
                gate = term if gate is None else gate + term
            rows = slice(al * N_KEYS, (al + 1) * N_KEYS)
            a_buf[rows, sl] = _gelu_tanh_x2(z_buf[rows, sl]) * gate.reshape(N_KEYS, LANES)

    acc[...] += jnp.dot(_unpacked(vt_ref[...]), a_buf[...], preferred_element_type=F32)

    @pl.when(j == pl.num_programs(1) - 1)
    def _():
        y = acc[...].T
        o_ref[...] = x_ref[...] + ada_ref[0][5:6, :] * _rms_norm(y, npost_ref[...])


def _peer(h2t, u_bf, vt_bf, cnt, rk2, p1, p2, x1, ada3, norm_post):
    t, d = x1.shape
    s = t // ada3.shape[0]
    ne = 2 * u_bf.shape[0]
    tm = min(TM_PEER, s)
    te = TE_PEER
    assert s % tm == 0 and tm % LANES == 0 and ne % te == 0, (s, tm, ne, te)
    assert te % (8 * N_KEYS) == 0, te
    tab2_spec = pl.BlockSpec((P_HEADS, tm // LANES, N_KEYS // 2, LANES), lambda i, j: (0, i, 0, 0))
    tab1_spec = pl.BlockSpec((P_HEADS, tm // LANES, te // N_KEYS, LANES), lambda i, j: (0, i, j, 0))
    return pl.pallas_call(
        functools.partial(_peer_kernel, tm=tm, te=te),
        grid=(t // tm, ne // te),
        in_specs=[pl.BlockSpec((d // 2, tm), lambda i, j: (0, i)),
                  pl.BlockSpec((te // 2, d), lambda i, j: (j, 0)),
                  pl.BlockSpec((d // 2, te), lambda i, j: (0, j)),
                  tab1_spec, tab2_spec, tab1_spec, tab2_spec,
                  pl.BlockSpec((tm, d), lambda i, j: (i, 0)),
                  pl.BlockSpec((1, 6, d), lambda i, j: ((i * tm) // s, 0, 0)),
                  pl.BlockSpec((1, d), lambda i, j: (0, 0))],
        out_specs=pl.BlockSpec((tm, d), lambda i, j: (i, 0)),
        out_shape=jax.ShapeDtypeStruct((t, d), F32),
        scratch_shapes=[pltpu.VMEM((d, tm), F32),
                        pltpu.VMEM((te, tm), BF16),
                        pltpu.VMEM((te, tm), BF16)],
        compiler_params=pltpu.CompilerParams(dimension_semantics=("arbitrary", "arbitrary"),
                                             vmem_limit_bytes=VMEM_LIMIT_BYTES),
        name="peer",
    )(h2t, u_bf, vt_bf, cnt, rk2, p1, p2, x1, ada3, norm_post)


def _layer(x, c, w_ada, b_ada, norm1_pre, norm1_post, w_in, conv_a_w, conv_qk_w, b_igate, b_fgate, mh_norm_w,
           w_branch_a, w_branch_m, w_out, norm2_pre, norm2_post, peer_wq, peer_subkeys, peer_u, peer_v):
    bsz, s, d = x.shape
    ng = 2 * M_HEADS
    ada3 = _ada(c, w_ada, b_ada).reshape(bsz, 6, d)

    o_if = 3 * d + 2 * M_HEADS * M_DK + 2 * d
    w_main = jnp.concatenate([w_in[:, :o_if], w_in[:, o_if + ng:]], axis=1).astype(BF16)
    w_if = w_in[:, o_if:o_if + ng]
    wif_col = jnp.pad(w_if, ((0, 0), (0, LANES - ng))).astype(BF16)
    wif_row = w_if.T.astype(BF16)
    b_if = jnp.concatenate([b_igate, b_fgate])
    bif_col = jnp.pad(b_if, (0, LANES - ng)).reshape(1, LANES)
    bif_row = b_if.reshape(ng, 1)

    ya, q, kt, v, so, sga, sgm, gcol, grow = _inproj(
        x, ada3, norm1_pre.reshape(1, d), w_main, wif_col, wif_row, bif_col, bif_row, conv_a_w, conv_qk_w)
    x1 = _mixer(x, ada3, q, kt, v, so, sga, sgm, ya, gcol, grow, mh_norm_w.reshape(1, d),
                norm1_post.reshape(1, d), w_branch_a.astype(BF16), w_branch_m.astype(BF16), w_out.astype(BF16))

    x1f = x1.reshape(bsz * s, d)
    h2t, cnt, rk2, p1, p2 = _route(x1f, ada3, norm2_pre.reshape(1, d), peer_wq.T.astype(BF16),
                                   peer_subkeys.astype(BF16))
    u_words, vt_words = _expert_words(peer_u, peer_v)
    out = _peer(h2t, u_words, vt_words, cnt, rk2, p1, p2, x1f, ada3, norm2_post.reshape(1, d))
    return out.reshape(bsz, s, d)


def kernel(x, c, w_ada, b_ada, norm1_pre, norm1_post, w_in, conv_a_w, conv_qk_w, b_igate, b_fgate, mh_norm_w, w_branch_a, w_branch_m, w_out, norm2_pre, norm2_post, peer_wq, peer_subkeys, peer_u, peer_v):
    for l in range(w_ada.shape[0]):
        x = _layer(x, c, w_ada[l], b_ada[l], norm1_pre[l], norm1_post[l], w_in[l], conv_a_w[l], conv_qk_w[l],
                   b_igate[l], b_fgate[l], mh_norm_w[l], w_branch_a[l], w_branch_m[l], w_out[l], norm2_pre[l],
                   norm2_post[l], peer_wq[l], peer_subkeys[l], peer_u[l], peer_v[l])
    return x
```

```python
import functools

import jax
import jax.numpy as jnp
from jax import lax
from jax.experimental import pallas as pl
from jax.experimental.pallas import tpu as pltpu

F32 = jnp.float32
BF16 = jnp.bfloat16

EPS = 1e-6
M_HEADS = 8
M_DK = 64
M_DV = 128
CONV_K = 3
QK_CONV_K = 4
P_HEADS = 8
N_KEYS = 128
P_TOPK = 16
HALO = 8
LANES = 128
BF16_ROWS = 16
NOT_RANKED = 99.0
KNOCK = 2.0 ** 100
GELU_HALF = 0.5

VMEM_LIMIT_BYTES = 56 * 1024 * 1024

MLSTM_CHUNK = 128
TS_INPROJ = 256
TS_MIXER = 512
TM_ROUTE = 512
TM_PEER = 1024
TE_PEER = 1024
TE_WORDS = 1024


def _resident(shape):
    nd = len(shape)
    return pl.BlockSpec(shape, lambda *_: (0,) * nd, pipeline_mode=pl.Buffered(1))


def _rms_norm(x, w):
    return x * lax.rsqrt(jnp.mean(x * x, axis=-1, keepdims=True) + EPS) * w


def _sigmoid(x):
    return 1.0 / (1.0 + jnp.exp(-x))


def _packed_words(x):
    return pltpu.bitcast(x, jnp.uint32)


def _unpacked(words):
    return pltpu.bitcast(words, BF16)


def _ada_kernel(c_ref, w_ref, b_ref, o_ref):
    c = c_ref[...]
    s = c * _sigmoid(c)
    o_ref[...] = jnp.dot(s.astype(BF16), w_ref[...].astype(BF16), preferred_element_type=F32) + b_ref[...]


def _ada(c, w_ada, b_ada):
    bsz, d = c.shape
    n = w_ada.shape[1]
    return pl.pallas_call(
        _ada_kernel,
        grid=(n // d,),
        in_specs=[pl.BlockSpec((bsz, d), lambda j: (0, 0)),
                  pl.BlockSpec((d, d), lambda j: (0, j)),
                  pl.BlockSpec((1, d), lambda j: (0, j))],
        out_specs=pl.BlockSpec((bsz, d), lambda j: (0, j)),
        out_shape=jax.ShapeDtypeStruct((bsz, n), F32),
        compiler_params=pltpu.CompilerParams(vmem_limit_bytes=VMEM_LIMIT_BYTES),
        name="ada",
    )(c, w_ada, b_ada.reshape(1, n))


def _inproj_kernel(x_ref, ada_ref, npre_ref, w_ref, wifc_ref, wifr_ref, bifc_ref, bifr_ref, cwa_ref, cwqk_ref,
                   ya_ref, q_ref, kt_ref, v_ref, so_ref, sga_ref, sgm_ref, gcol_ref, grow_ref,
                   ubuf, qkbuf, *, ts, d):
    @pl.when(pl.program_id(1) == 0)
    def _():
        ubuf[0:HALO, :] = jnp.zeros((HALO, d), F32)
        qkbuf[0:HALO, :] = jnp.zeros((HALO, d), F32)

    x = x_ref[0]
    ada = ada_ref[0]
    h = _rms_norm(x, npre_ref[...]) * (1.0 + ada[1:2, :]) + ada[0:1, :]
    hb = h.astype(BF16)

    def proj(k):
        return jnp.dot(hb, w_ref[:, k * d:(k + 1) * d], preferred_element_type=F32)

    u = proj(0) * proj(2)
    ubuf[HALO:HALO + ts, :] = u
    conv = u * cwa_ref[CONV_K - 1:CONV_K, :]
    for j in range(CONV_K - 1):
        off = HALO - (CONV_K - 1) + j
        conv = conv + ubuf[off:off + ts, :] * cwa_ref[j:j + 1, :]
    ya_ref[0] = _packed_words((proj(1) * conv).astype(BF16))
    ubuf[0:HALO, :] = ubuf[ts:ts + HALO, :]

    qk = proj(3)
    qkbuf[HALO:HALO + ts, :] = qk
    cq = qk * cwqk_ref[QK_CONV_K - 1:QK_CONV_K, :]
    for j in range(QK_CONV_K - 1):
        off = HALO - (QK_CONV_K - 1) + j
        cq = cq + qkbuf[off:off + ts, :] * cwqk_ref[j:j + 1, :]
    qkbuf[0:HALO, :] = qkbuf[ts:ts + HALO, :]
    cq = cq * _sigmoid(cq)
    nqk = M_HEADS * M_DK
    q_ref[0] = _packed_words((cq[:, :nqk] * (M_DK ** -0.5)).astype(BF16))
    kt_ref[0] = _packed_words(cq[:, nqk:].T.astype(BF16))

    v_ref[0] = _packed_words(proj(4).astype(BF16))
    so_ref[0] = _sigmoid(proj(5)).astype(BF16)
    sga_ref[0] = _sigmoid(proj(6)).astype(BF16)
    sgm_ref[0] = _sigmoid(proj(7)).astype(BF16)

    gcol_ref[0] = jnp.dot(hb, wifc_ref[...], preferred_element_type=F32) + bifc_ref[...]
    grow_ref[0] = lax.dot_general(wifr_ref[...], hb, (((1,), (1,)), ((), ())),
                                  preferred_element_type=F32) + bifr_ref[...]


def _inproj(x, ada3, norm_pre, w_main, wif_col, wif_row, bif_col, bif_row, conv_a_w, conv_qk_w):
    bsz, s, d = x.shape
    ts = min(TS_INPROJ, s)
    assert s % ts == 0 and d == 2 * M_HEADS * M_DK == M_HEADS * M_DV, (s, ts, d)
    ng = 2 * M_HEADS
    nqk = M_HEADS * M_DK
    tok = lambda dt, w=d: jax.ShapeDtypeStruct((bsz, s, w), dt)
    tok_spec = lambda w=d: pl.BlockSpec((1, ts, w), lambda b, i: (b, i, 0))
    words = lambda w=d: jax.ShapeDtypeStruct((bsz, s // 2, w), jnp.uint32)
    words_spec = lambda w=d: pl.BlockSpec((1, ts // 2, w), lambda b, i: (b, i, 0))
    return pl.pallas_call(
        functools.partial(_inproj_kernel, ts=ts, d=d),
        grid=(bsz, s // ts),
        in_specs=[tok_spec(),
                  pl.BlockSpec((1, 6, d), lambda b, i: (b, 0, 0)),
                  _resident((1, d)),
                  _resident(w_main.shape),
                  _resident(wif_col.shape),
                  _resident(wif_row.shape),
                  _resident(bif_col.shape),
                  _resident(bif_row.shape),
                  _resident(conv_a_w.shape),
                  _resident(conv_qk_w.shape)],
        out_specs=[words_spec(), words_spec(nqk),
                   pl.BlockSpec((1, nqk // 2, ts), lambda b, i: (b, 0, i)),
                   words_spec(), tok_spec(), tok_spec(), tok_spec(),
                   tok_spec(LANES),
                   pl.BlockSpec((1, ng, ts), lambda b, i: (b, 0, i))],
        out_shape=[words(), words(nqk),
                   jax.ShapeDtypeStruct((bsz, nqk // 2, s), jnp.uint32),
                   words(), tok(BF16), tok(BF16), tok(BF16),
                   tok(F32, LANES),
                   jax.ShapeDtypeStruct((bsz, ng, s), F32)],
        scratch_shapes=[pltpu.VMEM((HALO + ts, d), F32), pltpu.VMEM((HALO + ts, d), F32)],
        compiler_params=pltpu.CompilerParams(dimension_semantics=("arbitrary", "arbitrary"),
                                             vmem_limit_bytes=VMEM_LIMIT_BYTES),
        name="inproj",
    )(x, ada3, norm_pre, w_main, wif_col, wif_row, bif_col, bif_row, conv_a_w, conv_qk_w)


def _log_sigmoid(x):
    return jnp.minimum(x, 0.0) - jnp.log(1.0 + jnp.exp(-jnp.abs(x)))


def _chunk_scan(x, axis, chunk, op, identity):
    pos = lax.broadcasted_iota(jnp.int32, x.shape, axis) % chunk
    k = 1
    while k < chunk:
        x = op(x, jnp.where(pos >= k, pltpu.roll(x, k, axis), identity))
        k *= 2
    return x


def _bf16_terms(x):
    hi = x.astype(BF16)
    r = x - hi.astype(F32)
    mid = r.astype(BF16)
    lo = (r - mid.astype(F32)).astype(BF16)
    return hi, mid, lo


def _spread_heads(x, sel):
    return sum(jnp.dot(t, sel, preferred_element_type=F32) for t in _bf16_terms(x))


def _mixer_kernel(x_ref, ada_ref, q_ref, kt_ref, v_ref, so_ref, sga_ref, sgm_ref, ya_ref, gcol_ref, grow_ref,
                  sel_ref, mhw_ref, npost_ref, wa_ref, wm_ref, wo_ref, o_ref,
                  c_st, mcol_st, mrow_st, ym_buf, *, ts, chunk):
    first = pl.program_id(1) == 0

    @pl.when(first)
    def _():
        c_st[...] = jnp.zeros(c_st.shape, F32)
        mcol_st[...] = jnp.zeros(mcol_st.shape, F32)
        mrow_st[...] = jnp.zeros(mrow_st.shape, F32)

    nc = ts // chunk
    H = M_HEADS
    neg_inf = -jnp.inf

    gcol = gcol_ref[0]
    b_col = _chunk_scan(_log_sigmoid(gcol), 0, chunk, jnp.add, 0.0)
    r_col = pltpu.roll(gcol, H, 1) - b_col
    cm_col = _chunk_scan(r_col, 0, chunk, jnp.maximum, neg_inf)
    m = mcol_st[0:1, :]
    m_rows, decay_rows = [], []
    for c in range(nc):
        last = (c + 1) * chunk - 1
        b_l, cm_l = b_col[last:last + 1, :], cm_col[last:last + 1, :]
        m_next = b_l + jnp.maximum(m, cm_l)
        m_rows.append(m)
        decay_rows.append(jnp.exp(b_l + m - m_next))
        m = m_next
    mcol_st[...] = jnp.broadcast_to(m, mcol_st.shape)
    pad = jnp.zeros((8 - (2 * nc) % 8, LANES), F32)
    spread = _spread_heads(jnp.concatenate([b_col, cm_col] + m_rows + decay_rows + [pad], axis=0), sel_ref[...])
    bc_all, cm_all, sc_all = spread[0:ts], spread[ts:2 * ts], spread[2 * ts:]

    grow = grow_ref[0]
    i_row = grow[0:H, :]
    b_row = _chunk_scan(_log_sigmoid(grow[H:2 * H, :]), 1, chunk, jnp.add, 0.0)
    r_row = i_row - b_row
    cm_row = _chunk_scan(r_row, 1, chunk, jnp.maximum, neg_inf)
    mr = mrow_st[:, 0:1]
    ws_rows = []
    for c in range(nc):
        last = (c + 1) * chunk - 1
        b_l, cm_l = b_row[:, last:last + 1], cm_row[:, last:last + 1]
        m_next = b_l + jnp.maximum(mr, cm_l)
        ws_rows.append(jnp.exp(b_l + r_row[:, c * chunk:(c + 1) * chunk] - m_next))
        mr = m_next
    mrow_st[...] = jnp.broadcast_to(mr, mrow_st.shape)

    tril = (lax.broadcasted_iota(jnp.int32, (chunk, chunk), 0)
            >= lax.broadcasted_iota(jnp.int32, (chunk, chunk), 1))
    ones = jnp.ones((chunk, M_DV), BF16)

    for hd in range(H):
        hl = slice(hd * M_DV, (hd + 1) * M_DV)
        c_aug = c_st[hd]
        for c in range(nc):
            rows = slice(c * chunk, (c + 1) * chunk)
            wrows = slice(c * chunk // 2, (c + 1) * chunk // 2)
            q = _unpacked(q_ref[0, wrows, hd * M_DK:(hd + 1) * M_DK])
            kt = _unpacked(kt_ref[0, hd * M_DK // 2:(hd + 1) * M_DK // 2, rows])
            v = _unpacked(v_ref[0, wrows, hl])
            bc = bc_all[rows, hl]
            m_prev = sc_all[c:c + 1, hl]
            decay = sc_all[nc + c:nc + c + 1, hl]
            m_t = bc + jnp.maximum(m_prev, cm_all[rows, hl])
            dlog = jnp.where(tril, bc[:, :chunk] + r_row[hd:hd + 1, rows], neg_inf)
            s_qk = jnp.dot(q, kt, preferred_element_type=F32) * jnp.exp(dlog - m_t[:, :chunk])
            a_inter = jnp.exp(bc + m_prev - m_t)
            qc = jnp.dot(q, c_aug.astype(BF16), preferred_element_type=F32)
            num = jnp.dot(s_qk.astype(BF16), v, preferred_element_type=F32) + a_inter * qc[:, :M_DV]
            den = jnp.sum(s_qk, axis=1, keepdims=True) + a_inter * qc[:, M_DV:]
            h = num / jnp.maximum(jnp.abs(den), jnp.exp(-m_t))
            hn = _rms_norm(h, mhw_ref[:, hl])
            ym_buf[rows, hl] = (so_ref[0, rows, hl].astype(F32) * hn).astype(BF16)
            ktw = (kt.astype(F32) * ws_rows[c][hd:hd + 1, :]).astype(BF16)
            upd = jnp.dot(ktw, jnp.concatenate([v, ones], axis=1), preferred_element_type=F32)
            c_aug = jnp.concatenate([decay, decay], axis=1) * c_aug + upd
        c_st[hd] = c_aug

    mix = (sga_ref[0].astype(F32) * jnp.dot(_unpacked(ya_ref[0]), wa_ref[...], preferred_element_type=F32)
           + sgm_ref[0].astype(F32) * jnp.dot(ym_buf[...], wm_ref[...], preferred_element_type=F32))
    y = jnp.dot(mix.astype(BF16), wo_ref[...], preferred_element_type=F32)
    o_ref[0] = x_ref[0] + ada_ref[0][2:3, :] * _rms_norm(y, npost_ref[...])


def _mixer(x, ada3, q, kt, v, so, sga, sgm, ya, gcol, grow, mh_norm_w, norm_post, wa, wm, wo):
    bsz, s, d = x.shape
    ts = min(TS_MIXER, s)
    chunk = min(MLSTM_CHUNK, ts)
    assert chunk <= LANES and ts % chunk == 0 and s % ts == 0, (s, ts, chunk)
    ng = 2 * M_HEADS
    nqk = M_HEADS * M_DK
    sel = (jnp.arange(LANES)[:, None] - M_HEADS == jnp.arange(M_HEADS * M_DV)[None, :] // M_DV).astype(BF16)
    tok_spec = lambda w=d: pl.BlockSpec((1, ts, w), lambda b, i: (b, i, 0))
    words_spec = lambda w=d: pl.BlockSpec((1, ts // 2, w), lambda b, i: (b, i, 0))
    return pl.pallas_call(
        functools.partial(_mixer_kernel, ts=ts, chunk=chunk),
        grid=(bsz, s // ts),
        in_specs=[tok_spec(),
                  pl.BlockSpec((1, 6, d), lambda b, i: (b, 0, 0)),
                  words_spec(nqk),
                  pl.BlockSpec((1, nqk // 2, ts), lambda b, i: (b, 0, i)),
                  words_spec(), tok_spec(), tok_spec(), tok_spec(), words_spec(),
                  tok_spec(LANES),
                  pl.BlockSpec((1, ng, ts), lambda b, i: (b, 0, i)),
                  _resident(sel.shape),
                  _resident((1, d)), _resident((1, d)),
                  _resident(wa.shape), _resident(wm.shape), _resident(wo.shape)],
        out_specs=tok_spec(),
        out_shape=jax.ShapeDtypeStruct((bsz, s, d), F32),
        scratch_shapes=[pltpu.VMEM((M_HEADS, M_DK, 2 * M_DV), F32),
                        pltpu.VMEM((8, LANES), F32),
                        pltpu.VMEM((M_HEADS, LANES), F32),
                        pltpu.VMEM((ts, d), BF16)],
        compiler_params=pltpu.CompilerParams(dimension_semantics=("arbitrary", "arbitrary"),
                                             vmem_limit_bytes=VMEM_LIMIT_BYTES),
        name="mixer",
    )(x, ada3, q, kt, v, so, sga, sgm, ya, gcol, grow, sel, mh_norm_w, norm_post, wa, wm, wo)


def _argmax_rows(s, exact):
    m = jnp.max(s, axis=0, keepdims=True)
    hit = s == m
    if exact:
        row = lax.broadcasted_iota(jnp.int32, s.shape, 0)
        hit = row == jnp.min(jnp.where(hit, row, s.shape[0]), axis=0, keepdims=True)
    return hit, m


def _topk_rows(arrays, k, exact):
    arrays = list(arrays)
    vals = [[] for _ in arrays]
    if exact:
        ranks = [jnp.full(s.shape, NOT_RANKED, F32) for s in arrays]
        for r in range(k):
            for i, s in enumerate(arrays):
                hit, m = _argmax_rows(s, exact)
                ranks[i] = jnp.where(hit, float(r), ranks[i])
                arrays[i] = jnp.where(hit, -jnp.inf, s)
                vals[i].append(m)
        return vals, ranks
    for r in range(k):
        for i, s in enumerate(arrays):
            hit, m = _argmax_rows(s, exact)
            arrays[i] = jnp.where(hit, -(r + 2.0) * KNOCK, s)
            vals[i].append(m)
    ranks = [jnp.where(s <= -1.5 * KNOCK, s * (-1.0 / KNOCK) - 2.0, NOT_RANKED) for s in arrays]
    return vals, ranks


def _route_tables(s1, s2, exact, topk=None):
    (v1, v2), (rank1, rank2) = topk if topk is not None else _topk_rows((s1, s2), P_TOPK, exact)
    v2_lo = jnp.concatenate(v2[0:8], axis=0)
    v2_all = jnp.concatenate(v2, axis=0)
    v1_hi = jnp.concatenate(v1[8:16], axis=0)
    cand = jnp.concatenate([v1[0] + v2_all] + [v1[j] + v2_lo for j in range(1, 8)] + [v1_hi + v2[0]], axis=0)
    cur = cand
    for _ in range(P_TOPK):
        hit, _ = _argmax_rows(cur, exact)
        cur = jnp.where(hit, -jnp.inf, cur)
    sel = jnp.where(cur == -jnp.inf, 1.0, 0.0)
    ranked = lambda rk: jnp.sum(jnp.where(rk < NOT_RANKED, 1.0, 0.0), axis=0, keepdims=True)
    picked = jnp.maximum(jnp.maximum(ranked(rank1), ranked(rank2)), jnp.sum(sel, axis=0, keepdims=True))
    lowest = jnp.minimum(jnp.min(s1, axis=0, keepdims=True), jnp.min(s2, axis=0, keepdims=True))
    tied = jnp.where((picked > float(P_TOPK)) | (lowest <= -KNOCK), 1.0, 0.0)
    top = v1[0] + v2[0]
    z = jnp.sum(sel * jnp.exp(cand - top), axis=0, keepdims=True)
    counts = [jnp.sum(sel[0:16], axis=0, keepdims=True)]
    counts += [jnp.sum(sel[8 + 8 * j:16 + 8 * j], axis=0, keepdims=True) for j in range(1, 8)]
    counts += [sel[72 + j:73 + j] for j in range(8)]
    pk = (N_KEYS // BF16_ROWS, BF16_ROWS, LANES)
    rank1b = rank1.astype(BF16).reshape(pk)
    cnt = jnp.zeros(pk, BF16)
    for j in range(P_TOPK):
        count_j = pltpu.bitcast(jnp.broadcast_to(_bf16_pair_words(counts[j]), (8, LANES)), BF16)
        cnt = jnp.where(rank1b == float(j), count_j[None], cnt)
    cnt = cnt.reshape(N_KEYS, LANES)
    p1 = GELU_HALF * jnp.exp(s1 - v1[0])
    p2 = jnp.exp(s2 - v2[0]) * (1.0 / z)
    return cnt, rank2, p1, p2, tied


def _bf16_pair_words(x):
    hi = pltpu.bitcast(x.astype(BF16).astype(F32), jnp.uint32)
    return hi | lax.shift_right_logical(hi, jnp.uint32(16))


def _route_kernel(x_ref, ada_ref, npre_ref, wqt_ref, sk_ref, h2t_ref, cnt_ref, rk2_ref, p1_ref, p2_ref, qt_buf,
                  *, tm):
    ada = ada_ref[0]
    h2 = _rms_norm(x_ref[...], npre_ref[...]) * (1.0 + ada[4:5, :]) + ada[3:4, :]
    h2t = h2.T.astype(BF16)
    h2t_ref[...] = _packed_words(h2t)
    qt_buf[...] = jnp.dot(wqt_ref[...], h2t, preferred_element_type=F32).astype(BF16)

    def head(hd, carry):
        r0 = pl.multiple_of(hd * (2 * N_KEYS), 2 * N_KEYS)
        s1 = jnp.dot(sk_ref[hd, 0], qt_buf[pl.ds(r0, N_KEYS), :], preferred_element_type=F32)
        s2 = jnp.dot(sk_ref[hd, 1], qt_buf[pl.ds(r0 + N_KEYS, N_KEYS), :], preferred_element_type=F32)

        def tables(lc, exact, topk=None):
            sl = slice(lc * LANES, (lc + 1) * LANES)
            cnt, rk2, p1, p2, tied = _route_tables(s1[:, sl], s2[:, sl], exact, topk)
            cnt_ref[hd, lc] = _bf16_pair_words(cnt)
            rk2_ref[hd, lc] = _packed_words(rk2.astype(BF16))
            p1_ref[hd, lc] = _bf16_pair_words(p1)
            p2_ref[hd, lc] = _packed_words(p2.astype(BF16))
            return tied

        nlc = tm // LANES
        slabs = [s[:, lc * LANES:(lc + 1) * LANES] for lc in range(nlc) for s in (s1, s2)]
        vals, ranks = _topk_rows(slabs, P_TOPK, False)
        for lc in range(nlc):
            tied = tables(lc, False, ((vals[2 * lc], vals[2 * lc + 1]), (ranks[2 * lc], ranks[2 * lc + 1])))

            @pl.when(jnp.max(tied) > 0.0)
            def _():
                tables(lc, exact=True)
        return carry

    lax.fori_loop(0, P_HEADS, head, 0)


def _route(x1, ada3, norm_pre, wqt, subkeys):
    t, d = x1.shape
    s = t // ada3.shape[0]
    tm = min(TM_ROUTE, s)
    assert s % tm == 0 and tm % LANES == 0, (s, tm)
    tab = lambda rows: jax.ShapeDtypeStruct((P_HEADS, t // LANES, rows, LANES), jnp.uint32)
    tab_spec = lambda rows: pl.BlockSpec((P_HEADS, tm // LANES, rows, LANES), lambda i: (0, i, 0, 0))
    k1, k2 = N_KEYS, N_KEYS // 2
    return pl.pallas_call(
        functools.partial(_route_kernel, tm=tm),
        grid=(t // tm,),
        in_specs=[pl.BlockSpec((tm, d), lambda i: (i, 0)),
                  pl.BlockSpec((1, 6, d), lambda i: ((i * tm) // s, 0, 0)),
                  _resident((1, d)),
                  _resident(wqt.shape),
                  _resident(subkeys.shape)],
        out_specs=[pl.BlockSpec((d // 2, tm), lambda i: (0, i)),
                   tab_spec(k1), tab_spec(k2), tab_spec(k1), tab_spec(k2)],
        out_shape=[jax.ShapeDtypeStruct((d // 2, t), jnp.uint32), tab(k1), tab(k2), tab(k1), tab(k2)],
        scratch_shapes=[pltpu.VMEM((wqt.shape[0], tm), BF16)],
        compiler_params=pltpu.CompilerParams(dimension_semantics=("arbitrary",),
                                             vmem_limit_bytes=VMEM_LIMIT_BYTES),
        name="route",
    )(x1, ada3, norm_pre, wqt, subkeys)


def _expert_words_kernel(u_ref, v_ref, uw_ref, vtw_ref):
    uw_ref[...] = _packed_words(u_ref[...].astype(BF16))
    vtw_ref[...] = _packed_words(v_ref[...].T.astype(BF16))


def _expert_words(u, v):
    ne, d = u.shape
    te = min(TE_WORDS, ne)
    return pl.pallas_call(
        _expert_words_kernel,
        grid=(ne // te,),
        in_specs=[pl.BlockSpec((te, d), lambda j: (j, 0)), pl.BlockSpec((te, d), lambda j: (j, 0))],
        out_specs=[pl.BlockSpec((te // 2, d), lambda j: (j, 0)), pl.BlockSpec((d // 2, te), lambda j: (0, j))],
        out_shape=[jax.ShapeDtypeStruct((ne // 2, d), jnp.uint32), jax.ShapeDtypeStruct((d // 2, ne), jnp.uint32)],
        compiler_params=pltpu.CompilerParams(vmem_limit_bytes=VMEM_LIMIT_BYTES),
        name="expert_words",
    )(u, v)


def _gelu_tanh_x2(z):
    c = 0.7978845608028654
    return z + z * jnp.tanh(z * (c + (c * 0.044715) * (z * z)))


def _peer_kernel(h2t_ref, u_ref, vt_ref, cnt_ref, rk2_ref, p1_ref, p2_ref, x_ref, ada_ref, npost_ref, o_ref,
                 acc, z_buf, a_buf, *, tm, te):
    j = pl.program_id(1)

    @pl.when(j == 0)
    def _():
        acc[...] = jnp.zeros(acc.shape, F32)

    z_buf[...] = jnp.dot(_unpacked(u_ref[...]), _unpacked(h2t_ref[...]), preferred_element_type=F32).astype(BF16)

    pk = (N_KEYS // BF16_ROWS, BF16_ROWS, LANES)
    for lc in range(tm // LANES):
        sl = slice(lc * LANES, (lc + 1) * LANES)
        for al in range(te // N_KEYS):
            gate = None
            for hd in range(P_HEADS):
                cnt_a = pltpu.bitcast(jnp.broadcast_to(cnt_ref[hd, lc, al:al + 1, :], (8, LANES)), BF16)
                p1_a = pltpu.bitcast(jnp.broadcast_to(p1_ref[hd, lc, al:al + 1, :], (8, LANES)), BF16)
                hit = _unpacked(rk2_ref[hd, lc]).reshape(pk) < cnt_a[None]
                term = jnp.where(hit, _unpacked(p2_ref[hd, lc]).reshape(pk) * p1_a[None], jnp.zeros((), BF16))
                gate = term if gate is None else gate + term
            rows = slice(al * N_KEYS, (al + 1) * N_KEYS)
            a_buf[rows, sl] = _gelu_tanh_x2(z_buf[rows, sl]) * gate.reshape(N_KEYS, LANES)

    acc[...] += jnp.dot(_unpacked(vt_ref[...]), a_buf[...], preferred_element_type=F32)

    @pl.when(j == pl.num_programs(1) - 1)
    def _():
        y = acc[...].T
        o_ref[...] = x_ref[...] + ada_ref[0][5:6, :] * _rms_norm(y, npost_ref[...])


def _peer(h2t, u_bf, vt_bf, cnt, rk2, p1, p2, x1, ada3, norm_post):
    t, d = x1.shape
    s = t // ada3.shape[0]
    ne = 2 * u_bf.shape[0]
    tm = min(TM_PEER, s)
    te = TE_PEER
    assert s % tm == 0 and tm % LANES == 0 and ne % te == 0, (s, tm, ne, te)
    assert te % (8 * N_KEYS) == 0, te
    tab2_spec = pl.BlockSpec((P_HEADS, tm // LANES, N_KEYS // 2, LANES), lambda i, j: (0, i, 0, 0))
    tab1_spec = pl.BlockSpec((P_HEADS, tm // LANES, te // N_KEYS, LANES), lambda i, j: (0, i, j, 0))
    return pl.pallas_call(
        functools.partial(_peer_kernel, tm=tm, te=te),
        grid=(t // tm, ne // te),
        in_specs=[pl.BlockSpec((d // 2, tm), lambda i, j: (0, i)),
                  pl.BlockSpec((te // 2, d), lambda i, j: (j, 0)),
                  pl.BlockSpec((d // 2, te), lambda i, j: (0, j)),
                  tab1_spec, tab2_spec, tab1_spec, tab2_spec,
                  pl.BlockSpec((tm, d), lambda i, j: (i, 0)),
                  pl.BlockSpec((1, 6, d), lambda i, j: ((i * tm) // s, 0, 0)),
                  pl.BlockSpec((1, d), lambda i, j: (0, 0))],
        out_specs=pl.BlockSpec((tm, d), lambda i, j: (i, 0)),
        out_shape=jax.ShapeDtypeStruct((t, d), F32),
        scratch_shapes=[pltpu.VMEM((d, tm), F32),
                        pltpu.VMEM((te, tm), BF16),
                        pltpu.VMEM((te, tm), BF16)],
        compiler_params=pltpu.CompilerParams(dimension_semantics=("arbitrary", "arbitrary"),
                                             vmem_limit_bytes=VMEM_LIMIT_BYTES),
        name="peer",
    )(h2t, u_bf, vt_bf, cnt, rk2, p1, p2, x1, ada3, norm_post)


def _layer(x, c, w_ada, b_ada, norm1_pre, norm1_post, w_in, conv_a_w, conv_qk_w, b_igate, b_fgate, mh_norm_w,
           w_branch_a, w_branch_m, w_out, norm2_pre, norm2_post, peer_wq, peer_subkeys, peer_u, peer_v):
    bsz, s, d = x.shape
    ng = 2 * M_HEADS
    ada3 = _ada(c, w_ada, b_ada).reshape(bsz, 6, d)

    o_if = 3 * d + 2 * M_HEADS * M_DK + 2 * d
    w_main = jnp.concatenate([w_in[:, :o_if], w_in[:, o_if + ng:]], axis=1).astype(BF16)
    w_if = w_in[:, o_if:o_if + ng]
    wif_col = jnp.pad(w_if, ((0, 0), (0, LANES - ng))).astype(BF16)
    wif_row = w_if.T.astype(BF16)
    b_if = jnp.concatenate([b_igate, b_fgate])
    bif_col = jnp.pad(b_if, (0, LANES - ng)).reshape(1, LANES)
    bif_row = b_if.reshape(ng, 1)

    ya, q, kt, v, so, sga, sgm, gcol, grow = _inproj(
        x, ada3, norm1_pre.reshape(1, d), w_main, wif_col, wif_row, bif_col, bif_row, conv_a_w, conv_qk_w)
    x1 = _mixer(x, ada3, q, kt, v, so, sga, sgm, ya, gcol, grow, mh_norm_w.reshape(1, d),
                norm1_post.reshape(1, d), w_branch_a.astype(BF16), w_branch_m.astype(BF16), w_out.astype(BF16))

    x1f = x1.reshape(bsz * s, d)
    h2t, cnt, rk2, p1, p2 = _route(x1f, ada3, norm2_pre.reshape(1, d), peer_wq.T.astype(BF16),
                                   peer_subkeys.astype(BF16))
    u_words, vt_words = _expert_words(peer_u, peer_v)
    out = _peer(h2t, u_words, vt_words, cnt, rk2, p1, p2, x1f, ada3, norm2_post.reshape(1, d))
    return out.reshape(bsz, s, d)


def kernel(x, c, w_ada, b_ada, norm1_pre, norm1_post, w_in, conv_a_w, conv_qk_w, b_igate, b_fgate, mh_norm_w, w_branch_a, w_branch_m, w_out, norm2_pre, norm2_post, peer_wq, peer_subkeys, peer_u, peer_v):
    for l in range(w_ada.shape[0]):
        x = _layer(x, c, w_ada[l], b_ada[l], norm1_pre[l], norm1_post[l], w_in[l], conv_a_w[l], conv_qk_w[l],
                   b_igate[l], b_fgate[l], mh_norm_w[l], w_branch_a[l], w_branch_m[l], w_out[l], norm2_pre[l],
                   norm2_post[l], peer_wq[l], peer_subkeys[l], peer_u[l], peer_v[l])
    return x
```

```python
import functools

import jax
import jax.numpy as jnp
from jax import lax
from jax.experimental import pallas as pl
from jax.experimental.pallas import tpu as pltpu

F32 = jnp.float32
BF16 = jnp.bfloat16

EPS = 1e-6
M_HEADS = 8
M_DK = 64
M_DV = 128
CONV_K = 3
QK_CONV_K = 4
P_HEADS = 8
N_KEYS = 128
P_TOPK = 16
HALO = 8
LANES = 128
BF16_ROWS = 16
NOT_RANKED = 99.0
KNOCK = 2.0 ** 100
GELU_HALF = 0.5

VMEM_LIMIT_BYTES = 56 * 1024 * 1024

MLSTM_CHUNK = 128
TS_INPROJ = 256
TS_MIXER = 512
TM_ROUTE = 512
TM_PEER = 1024
TE_PEER = 1024
TE_WORDS = 1024


def _resident(shape):
    nd = len(shape)
    return pl.BlockSpec(shape, lambda *_: (0,) * nd, pipeline_mode=pl.Buffered(1))


def _rms_norm(x, w):
    return x * lax.rsqrt(jnp.mean(x * x, axis=-1, keepdims=True) + EPS) * w


def _sigmoid(x):
    return 1.0 / (1.0 + jnp.exp(-x))


def _packed_words(x):
    return pltpu.bitcast(x, jnp.uint32)


def _unpacked(words):
    return pltpu.bitcast(words, BF16)


def _ada_kernel(c_ref, w_ref, b_ref, o_ref):
    c = c_ref[...]
    s = c * _sigmoid(c)
    o_ref[...] = jnp.dot(s.astype(BF16), w_ref[...].astype(BF16), preferred_element_type=F32) + b_ref[...]


def _ada(c, w_ada, b_ada):
    bsz, d = c.shape
    n = w_ada.shape[1]
    return pl.pallas_call(
        _ada_kernel,
        grid=(n // d,),
        in_specs=[pl.BlockSpec((bsz, d), lambda j: (0, 0)),
                  pl.BlockSpec((d, d), lambda j: (0, j)),
                  pl.BlockSpec((1, d), lambda j: (0, j))],
        out_specs=pl.BlockSpec((bsz, d), lambda j: (0, j)),
        out_shape=jax.ShapeDtypeStruct((bsz, n), F32),
        compiler_params=pltpu.CompilerParams(vmem_limit_bytes=VMEM_LIMIT_BYTES),
        name="ada",
    )(c, w_ada, b_ada.reshape(1, n))


def _inproj_kernel(x_ref, ada_ref, npre_ref, w_ref, wifc_ref, wifr_ref, bifc_ref, bifr_ref, cwa_ref, cwqk_ref,
                   ya_ref, q_ref, kt_ref, v_ref, so_ref, sga_ref, sgm_ref, gcol_ref, grow_ref,
                   ubuf, qkbuf, *, ts, d):
    @pl.when(pl.program_id(1) == 0)
    def _():
        ubuf[0:HALO, :] = jnp.zeros((HALO, d), F32)
        qkbuf[0:HALO, :] = jnp.zeros((HALO, d), F32)

    x = x_ref[0]
    ada = ada_ref[0]
    h = _rms_norm(x, npre_ref[...]) * (1.0 + ada[1:2, :]) + ada[0:1, :]
    hb = h.astype(BF16)

    def proj(k):
        return jnp.dot(hb, w_ref[:, k * d:(k + 1) * d], preferred_element_type=F32)

    u = proj(0) * proj(2)
    ubuf[HALO:HALO + ts, :] = u
    conv = u * cwa_ref[CONV_K - 1:CONV_K, :]
    for j in range(CONV_K - 1):
        off = HALO - (CONV_K - 1) + j
        conv = conv + ubuf[off:off + ts, :] * cwa_ref[j:j + 1, :]
    ya_ref[0] = _packed_words((proj(1) * conv).astype(BF16))
    ubuf[0:HALO, :] = ubuf[ts:ts + HALO, :]

    qk = proj(3)
    qkbuf[HALO:HALO + ts, :] = qk
    cq = qk * cwqk_ref[QK_CONV_K - 1:QK_CONV_K, :]
    for j in range(QK_CONV_K - 1):
        off = HALO - (QK_CONV_K - 1) + j
        cq = cq + qkbuf[off:off + ts, :] * cwqk_ref[j:j + 1, :]
    qkbuf[0:HALO, :] = qkbuf[ts:ts + HALO, :]
    cq = cq * _sigmoid(cq)
    nqk = M_HEADS * M_DK
    q_ref[0] = _packed_words((cq[:, :nqk] * (M_DK ** -0.5)).astype(BF16))
    kt_ref[0] = _packed_words(cq[:, nqk:].T.astype(BF16))

    v_ref[0] = _packed_words(proj(4).astype(BF16))
    so_ref[0] = _sigmoid(proj(5)).astype(BF16)
    sga_ref[0] = _sigmoid(proj(6)).astype(BF16)
    sgm_ref[0] = _sigmoid(proj(7)).astype(BF16)

    gcol_ref[0] = jnp.dot(hb, wifc_ref[...], preferred_element_type=F32) + bifc_ref[...]
    grow_ref[0] = lax.dot_general(wifr_ref[...], hb, (((1,), (1,)), ((), ())),
                                  preferred_element_type=F32) + bifr_ref[...]


def _inproj(x, ada3, norm_pre, w_main, wif_col, wif_row, bif_col, bif_row, conv_a_w, conv_qk_w):
    bsz, s, d = x.shape
    ts = min(TS_INPROJ, s)
    assert s % ts == 0 and d == 2 * M_HEADS * M_DK == M_HEADS * M_DV, (s, ts, d)
    ng = 2 * M_HEADS
    nqk = M_HEADS * M_DK
    tok = lambda dt, w=d: jax.ShapeDtypeStruct((bsz, s, w), dt)
    tok_spec = lambda w=d: pl.BlockSpec((1, ts, w), lambda b, i: (b, i, 0))
    words = lambda w=d: jax.ShapeDtypeStruct((bsz, s // 2, w), jnp.uint32)
    words_spec = lambda w=d: pl.BlockSpec((1, ts // 2, w), lambda b, i: (b, i, 0))
    return pl.pallas_call(
        functools.partial(_inproj_kernel, ts=ts, d=d),
        grid=(bsz, s // ts),
        in_specs=[tok_spec(),
                  pl.BlockSpec((1, 6, d), lambda b, i: (b, 0, 0)),
                  _resident((1, d)),
                  _resident(w_main.shape),
                  _resident(wif_col.shape),
                  _resident(wif_row.shape),
                  _resident(bif_col.shape),
                  _resident(bif_row.shape),
                  _resident(conv_a_w.shape),
                  _resident(conv_qk_w.shape)],
        out_specs=[words_spec(), words_spec(nqk),
                   pl.BlockSpec((1, nqk // 2, ts), lambda b, i: (b, 0, i)),
                   words_spec(), tok_spec(), tok_spec(), tok_spec(),
                   tok_spec(LANES),
                   pl.BlockSpec((1, ng, ts), lambda b, i: (b, 0, i))],
        out_shape=[words(), words(nqk),
                   jax.ShapeDtypeStruct((bsz, nqk // 2, s), jnp.uint32),
                   words(), tok(BF16), tok(BF16), tok(BF16),
                   tok(F32, LANES),
                   jax.ShapeDtypeStruct((bsz, ng, s), F32)],
        scratch_shapes=[pltpu.VMEM((HALO + ts, d), F32), pltpu.VMEM((HALO + ts, d), F32)],
        compiler_params=pltpu.CompilerParams(dimension_semantics=("arbitrary", "arbitrary"),
                                             vmem_limit_bytes=VMEM_LIMIT_BYTES),
        name="inproj",
    )(x, ada3, norm_pre, w_main, wif_col, wif_row, bif_col, bif_row, conv_a_w, conv_qk_w)


def _log_sigmoid(x):
    return jnp.minimum(x, 0.0) - jnp.log(1.0 + jnp.exp(-jnp.abs(x)))


def _chunk_scan(x, axis, chunk, op, identity):
    pos = lax.broadcasted_iota(jnp.int32, x.shape, axis) % chunk
    k = 1
    while k < chunk:
        x = op(x, jnp.where(pos >= k, pltpu.roll(x, k, axis), identity))
        k *= 2
    return x


def _bf16_terms(x):
    hi = x.astype(BF16)
    r = x - hi.astype(F32)
    mid = r.astype(BF16)
    lo = (r - mid.astype(F32)).astype(BF16)
    return hi, mid, lo


def _spread_heads(x, sel):
    return sum(jnp.dot(t, sel, preferred_element_type=F32) for t in _bf16_terms(x))


def _mixer_kernel(x_ref, ada_ref, q_ref, kt_ref, v_ref, so_ref, sga_ref, sgm_ref, ya_ref, gcol_ref, grow_ref,
                  sel_ref, mhw_ref, npost_ref, wa_ref, wm_ref, wo_ref, o_ref,
                  c_st, mcol_st, mrow_st, ym_buf, *, ts, chunk):
    first = pl.program_id(1) == 0

    @pl.when(first)
    def _():
        c_st[...] = jnp.zeros(c_st.shape, F32)
        mcol_st[...] = jnp.zeros(mcol_st.shape, F32)
        mrow_st[...] = jnp.zeros(mrow_st.shape, F32)

    nc = ts // chunk
    H = M_HEADS
    neg_inf = -jnp.inf

    gcol = gcol_ref[0]
    b_col = _chunk_scan(_log_sigmoid(gcol), 0, chunk, jnp.add, 0.0)
    r_col = pltpu.roll(gcol, H, 1) - b_col
    cm_col = _chunk_scan(r_col, 0, chunk, jnp.maximum, neg_inf)
    m = mcol_st[0:1, :]
    m_rows, decay_rows = [], []
    for c in range(nc):
        last = (c + 1) * chunk - 1
        b_l, cm_l = b_col[last:last + 1, :], cm_col[last:last + 1, :]
        m_next = b_l + jnp.maximum(m, cm_l)
        m_rows.append(m)
        decay_rows.append(jnp.exp(b_l + m - m_next))
        m = m_next
    mcol_st[...] = jnp.broadcast_to(m, mcol_st.shape)
    pad = jnp.zeros((8 - (2 * nc) % 8, LANES), F32)
    spread = _spread_heads(jnp.concatenate([b_col, cm_col] + m_rows + decay_rows + [pad], axis=0), sel_ref[...])
    bc_all, cm_all, sc_all = spread[0:ts], spread[ts:2 * ts], spread[2 * ts:]

    grow = grow_ref[0]
    i_row = grow[0:H, :]
    b_row = _chunk_scan(_log_sigmoid(grow[H:2 * H, :]), 1, chunk, jnp.add, 0.0)
    r_row = i_row - b_row
    cm_row = _chunk_scan(r_row, 1, chunk, jnp.maximum, neg_inf)
    mr = mrow_st[:, 0:1]
    ws_rows = []
    for c in range(nc):
        last = (c + 1) * chunk - 1
        b_l, cm_l = b_row[:, last:last + 1], cm_row[:, last:last + 1]
        m_next = b_l + jnp.maximum(mr, cm_l)
        ws_rows.append(jnp.exp(b_l + r_row[:, c * chunk:(c + 1) * chunk] - m_next))
        mr = m_next
    mrow_st[...] = jnp.broadcast_to(mr, mrow_st.shape)

    tril = (lax.broadcasted_iota(jnp.int32, (chunk, chunk), 0)
            >= lax.broadcasted_iota(jnp.int32, (chunk, chunk), 1))
    ones = jnp.ones((chunk, M_DV), BF16)

    for hd in range(H):
        hl = slice(hd * M_DV, (hd + 1) * M_DV)
        c_aug = c_st[hd]
        for c in range(nc):
            rows = slice(c * chunk, (c + 1) * chunk)
            wrows = slice(c * chunk // 2, (c + 1) * chunk // 2)
            q = _unpacked(q_ref[0, wrows, hd * M_DK:(hd + 1) * M_DK])
            kt = _unpacked(kt_ref[0, hd * M_DK // 2:(hd + 1) * M_DK // 2, rows])
            v = _unpacked(v_ref[0, wrows, hl])
            bc = bc_all[rows, hl]
            m_prev = sc_all[c:c + 1, hl]
            decay = sc_all[nc + c:nc + c + 1, hl]
            m_t = bc + jnp.maximum(m_prev, cm_all[rows, hl])
            dlog = jnp.where(tril, bc[:, :chunk] + r_row[hd:hd + 1, rows], neg_inf)
            s_qk = jnp.dot(q, kt, preferred_element_type=F32) * jnp.exp(dlog - m_t[:, :chunk])
            a_inter = jnp.exp(bc + m_prev - m_t)
            qc = jnp.dot(q, c_aug.astype(BF16), preferred_element_type=F32)
            num = jnp.dot(s_qk.astype(BF16), v, preferred_element_type=F32) + a_inter * qc[:, :M_DV]
            den = jnp.sum(s_qk, axis=1, keepdims=True) + a_inter * qc[:, M_DV:]
            h = num / jnp.maximum(jnp.abs(den), jnp.exp(-m_t))
            hn = _rms_norm(h, mhw_ref[:, hl])
            ym_buf[rows, hl] = (so_ref[0, rows, hl].astype(F32) * hn).astype(BF16)
            ktw = (kt.astype(F32) * ws_rows[c][hd:hd + 1, :]).astype(BF16)
            upd = jnp.dot(ktw, jnp.concatenate([v, ones], axis=1), preferred_element_type=F32)
            c_aug = jnp.concatenate([decay, decay], axis=1) * c_aug + upd
        c_st[hd] = c_aug

    mix = (sga_ref[0].astype(F32) * jnp.dot(_unpacked(ya_ref[0]), wa_ref[...], preferred_element_type=F32)
           + sgm_ref[0].astype(F32) * jnp.dot(ym_buf[...], wm_ref[...], preferred_element_type=F32))
    y = jnp.dot(mix.astype(BF16), wo_ref[...], preferred_element_type=F32)
    o_ref[0] = x_ref[0] + ada_ref[0][2:3, :] * _rms_norm(y, npost_ref[...])


def _mixer(x, ada3, q, kt, v, so, sga, sgm, ya, gcol, grow, mh_norm_w, norm_post, wa, wm, wo):
    bsz, s, d = x.shape
    ts = min(TS_MIXER, s)
    chunk = min(MLSTM_CHUNK, ts)
    assert chunk <= LANES and ts % chunk == 0 and s % ts == 0, (s, ts, chunk)
    ng = 2 * M_HEADS
    nqk = M_HEADS * M_DK
    sel = (jnp.arange(LANES)[:, None] - M_HEADS == jnp.arange(M_HEADS * M_DV)[None, :] // M_DV).astype(BF16)
    tok_spec = lambda w=d: pl.BlockSpec((1, ts, w), lambda b, i: (b, i, 0))
    words_spec = lambda w=d: pl.BlockSpec((1, ts // 2, w), lambda b, i: (b, i, 0))
    return pl.pallas_call(
        functools.partial(_mixer_kernel, ts=ts, chunk=chunk),
        grid=(bsz, s // ts),
        in_specs=[tok_spec(),
                  pl.BlockSpec((1, 6, d), lambda b, i: (b, 0, 0)),
                  words_spec(nqk),
                  pl.BlockSpec((1, nqk // 2, ts), lambda b, i: (b, 0, i)),
                  words_spec(), tok_spec(), tok_spec(), tok_spec(), words_spec(),
                  tok_spec(LANES),
                  pl.BlockSpec((1, ng, ts), lambda b, i: (b, 0, i)),
                  _resident(sel.shape),
                  _resident((1, d)), _resident((1, d)),
                  _resident(wa.shape), _resident(wm.shape), _resident(wo.shape)],
        out_specs=tok_spec(),
        out_shape=jax.ShapeDtypeStruct((bsz, s, d), F32),
        scratch_shapes=[pltpu.VMEM((M_HEADS, M_DK, 2 * M_DV), F32),
                        pltpu.VMEM((8, LANES), F32),
                        pltpu.VMEM((M_HEADS, LANES), F32),
                        pltpu.VMEM((ts, d), BF16)],
        compiler_params=pltpu.CompilerParams(dimension_semantics=("arbitrary", "arbitrary"),
                                             vmem_limit_bytes=VMEM_LIMIT_BYTES),
        name="mixer",
    )(x, ada3, q, kt, v, so, sga, sgm, ya, gcol, grow, sel, mh_norm_w, norm_post, wa, wm, wo)


def _argmax_rows(s, exact):
    m = jnp.max(s, axis=0, keepdims=True)
    hit = s == m
    if exact:
        row = lax.broadcasted_iota(jnp.int32, s.shape, 0)
        hit = row == jnp.min(jnp.where(hit, row, s.shape[0]), axis=0, keepdims=True)
    return hit, m


def _topk_rows(arrays, k, exact):
    arrays = list(arrays)
    vals = [[] for _ in arrays]
    if exact:
        ranks = [jnp.full(s.shape, NOT_RANKED, F32) for s in arrays]
        for r in range(k):
            for i, s in enumerate(arrays):
                hit, m = _argmax_rows(s, exact)
                ranks[i] = jnp.where(hit, float(r), ranks[i])
                arrays[i] = jnp.where(hit, -jnp.inf, s)
                vals[i].append(m)
        return vals, ranks
    for r in range(k):
        for i, s in enumerate(arrays):
            hit, m = _argmax_rows(s, exact)
            arrays[i] = jnp.where(hit, -(r + 2.0) * KNOCK, s)
            vals[i].append(m)
    ranks = [jnp.where(s <= -1.5 * KNOCK, s * (-1.0 / KNOCK) - 2.0, NOT_RANKED) for s in arrays]
    return vals, ranks


def _route_tables(s1, s2, exact, topk=None):
    (v1, v2), (rank1, rank2) = topk if topk is not None else _topk_rows((s1, s2), P_TOPK, exact)
    v2_lo = jnp.concatenate(v2[0:8], axis=0)
    v2_all = jnp.concatenate(v2, axis=0)
    v1_hi = jnp.concatenate(v1[8:16], axis=0)
    cand = jnp.concatenate([v1[0] + v2_all] + [v1[j] + v2_lo for j in range(1, 8)] + [v1_hi + v2[0]], axis=0)
    cur = cand
    for _ in range(P_TOPK):
        hit, _ = _argmax_rows(cur, exact)
        cur = jnp.where(hit, -jnp.inf, cur)
    sel = jnp.where(cur == -jnp.inf, 1.0, 0.0)
    ranked = lambda rk: jnp.sum(jnp.where(rk < NOT_RANKED, 1.0, 0.0), axis=0, keepdims=True)
    picked = jnp.maximum(jnp.maximum(ranked(rank1), ranked(rank2)), jnp.sum(sel, axis=0, keepdims=True))
    lowest = jnp.minimum(jnp.min(s1, axis=0, keepdims=True), jnp.min(s2, axis=0, keepdims=True))
    tied = jnp.where((picked > float(P_TOPK)) | (lowest <= -KNOCK), 1.0, 0.0)
    top = v1[0] + v2[0]
    z = jnp.sum(sel * jnp.exp(cand - top), axis=0, keepdims=True)
    counts = [jnp.sum(sel[0:16], axis=0, keepdims=True)]
    counts += [jnp.sum(sel[8 + 8 * j:16 + 8 * j], axis=0, keepdims=True) for j in range(1, 8)]
    counts += [sel[72 + j:73 + j] for j in range(8)]
    pk = (N_KEYS // BF16_ROWS, BF16_ROWS, LANES)
    rank1b = rank1.astype(BF16).reshape(pk)
    cnt = jnp.zeros(pk, BF16)
    for j in range(P_TOPK):
        count_j = pltpu.bitcast(jnp.broadcast_to(_bf16_pair_words(counts[j]), (8, LANES)), BF16)
        cnt = jnp.where(rank1b == float(j), count_j[None], cnt)
    cnt = cnt.reshape(N_KEYS, LANES)
    p1 = GELU_HALF * jnp.exp(s1 - v1[0])
    p2 = jnp.exp(s2 - v2[0]) * (1.0 / z)
    return cnt, rank2, p1, p2, tied


def _bf16_pair_words(x):
    hi = pltpu.bitcast(x.astype(BF16).astype(F32), jnp.uint32)
    return hi | lax.shift_right_logical(hi, jnp.uint32(16))


def _route_kernel(x_ref, ada_ref, npre_ref, wqt_ref, sk_ref, h2t_ref, cnt_ref, rk2_ref, p1_ref, p2_ref, qt_buf,
                  *, tm):
    ada = ada_ref[0]
    h2 = _rms_norm(x_ref[...], npre_ref[...]) * (1.0 + ada[4:5, :]) + ada[3:4, :]
    h2t = h2.T.astype(BF16)
    h2t_ref[...] = _packed_words(h2t)
    qt_buf[...] = jnp.dot(wqt_ref[...], h2t, preferred_element_type=F32).astype(BF16)

    def head(hd, carry):
        r0 = pl.multiple_of(hd * (2 * N_KEYS), 2 * N_KEYS)
        s1 = jnp.dot(sk_ref[hd, 0], qt_buf[pl.ds(r0, N_KEYS), :], preferred_element_type=F32)
        s2 = jnp.dot(sk_ref[hd, 1], qt_buf[pl.ds(r0 + N_KEYS, N_KEYS), :], preferred_element_type=F32)

        def tables(lc, exact, topk=None):
            sl = slice(lc * LANES, (lc + 1) * LANES)
            cnt, rk2, p1, p2, tied = _route_tables(s1[:, sl], s2[:, sl], exact, topk)
            cnt_ref[hd, lc] = _bf16_pair_words(cnt)
            rk2_ref[hd, lc] = _packed_words(rk2.astype(BF16))
            p1_ref[hd, lc] = _bf16_pair_words(p1)
            p2_ref[hd, lc] = _packed_words(p2.astype(BF16))
            return tied

        nlc = tm // LANES
        slabs = [s[:, lc * LANES:(lc + 1) * LANES] for lc in range(nlc) for s in (s1, s2)]
        vals, ranks = _topk_rows(slabs, P_TOPK, False)
        tied = [tables(lc, False, ((vals[2 * lc], vals[2 * lc + 1]), (ranks[2 * lc], ranks[2 * lc + 1])))
                for lc in range(nlc)]

        @pl.when(jnp.max(functools.reduce(jnp.maximum, tied)) > 0.0)
        def _():
            for lc in range(nlc):
                tables(lc, exact=True)
        return carry

    lax.fori_loop(0, P_HEADS, head, 0)


def _route(x1, ada3, norm_pre, wqt, subkeys):
    t, d = x1.shape
    s = t // ada3.shape[0]
    tm = min(TM_ROUTE, s)
    assert s % tm == 0 and tm % LANES == 0, (s, tm)
    tab = lambda rows: jax.ShapeDtypeStruct((P_HEADS, t // LANES, rows, LANES), jnp.uint32)
    tab_spec = lambda rows: pl.BlockSpec((P_HEADS, tm // LANES, rows, LANES), lambda i: (0, i, 0, 0))
    k1, k2 = N_KEYS, N_KEYS // 2
    return pl.pallas_call(
        functools.partial(_route_kernel, tm=tm),
        grid=(t // tm,),
        in_specs=[pl.BlockSpec((tm, d), lambda i: (i, 0)),
                  pl.BlockSpec((1, 6, d), lambda i: ((i * tm) // s, 0, 0)),
                  _resident((1, d)),
                  _resident(wqt.shape),
                  _resident(subkeys.shape)],
        out_specs=[pl.BlockSpec((d // 2, tm), lambda i: (0, i)),
                   tab_spec(k1), tab_spec(k2), tab_spec(k1), tab_spec(k2)],
        out_shape=[jax.ShapeDtypeStruct((d // 2, t), jnp.uint32), tab(k1), tab(k2), tab(k1), tab(k2)],
        scratch_shapes=[pltpu.VMEM((wqt.shape[0], tm), BF16)],
        compiler_params=pltpu.CompilerParams(dimension_semantics=("arbitrary",),
                                             vmem_limit_bytes=VMEM_LIMIT_BYTES),
        name="route",
    )(x1, ada3, norm_pre, wqt, subkeys)


def _expert_words_kernel(u_ref, v_ref, uw_ref, vtw_ref):
    uw_ref[...] = _packed_words(u_ref[...].astype(BF16))
    vtw_ref[...] = _packed_words(v_ref[...].T.astype(BF16))


def _expert_words(u, v):
    ne, d = u.shape
    te = min(TE_WORDS, ne)
    return pl.pallas_call(
        _expert_words_kernel,
        grid=(ne // te,),
        in_specs=[pl.BlockSpec((te, d), lambda j: (j, 0)), pl.BlockSpec((te, d), lambda j: (j, 0))],
        out_specs=[pl.BlockSpec((te // 2, d), lambda j: (j, 0)), pl.BlockSpec((d // 2, te), lambda j: (0, j))],
        out_shape=[jax.ShapeDtypeStruct((ne // 2, d), jnp.uint32), jax.ShapeDtypeStruct((d // 2, ne), jnp.uint32)],
        compiler_params=pltpu.CompilerParams(vmem_limit_bytes=VMEM_LIMIT_BYTES),
        name="expert_words",
    )(u, v)


def _gelu_tanh_x2(z):
    c = 0.7978845608028654
    return z + z * jnp.tanh(z * (c + (c * 0.044715) * (z * z)))


def _peer_kernel(h2t_ref, u_ref, vt_ref, cnt_ref, rk2_ref, p1_ref, p2_ref, x_ref, ada_ref, npost_ref, o_ref,
                 acc, z_buf, a_buf, *, tm, te):
    j = pl.program_id(1)

    @pl.when(j == 0)
    def _():
        acc[...] = jnp.zeros(acc.shape, F32)

    z_buf[...] = jnp.dot(_unpacked(u_ref[...]), _unpacked(h2t_ref[...]), preferred_element_type=F32).astype(BF16)

    pk = (N_KEYS // BF16_ROWS, BF16_ROWS, LANES)
    for lc in range(tm // LANES):
        sl = slice(lc * LANES, (lc + 1) * LANES)
        for al in range(te // N_KEYS):
            gate = None
            for hd in range(P_HEADS):
                cnt_a = pltpu.bitcast(jnp.broadcast_to(cnt_ref[hd, lc, al:al + 1, :], (8, LANES)), BF16)
                p1_a = pltpu.bitcast(jnp.broadcast_to(p1_ref[hd, lc, al:al + 1, :], (8, LANES)), BF16)
                hit = _unpacked(rk2_ref[hd, lc]).reshape(pk) < cnt_a[None]
                term = jnp.where(hit, _unpacked(p2_ref[hd, lc]).reshape(pk) * p1_a[None], jnp.zeros((), BF16))
                gate = term if gate is None else gate + term
            rows = slice(al * N_KEYS, (al + 1) * N_KEYS)
            a_buf[rows, sl] = _gelu_tanh_x2(z_buf[rows, sl]) * gate.reshape(N_KEYS, LANES)

    acc[...] += jnp.dot(_unpacked(vt_ref[...]), a_buf[...], preferred_element_type=F32)

    @pl.when(j == pl.num_programs(1) - 1)
    def _():
        y = acc[...].T
        o_ref[...] = x_ref[...] + ada_ref[0][5:6, :] * _rms_norm(y, npost_ref[...])


def _peer(h2t, u_bf, vt_bf, cnt, rk2, p1, p2, x1, ada3, norm_post):
    t, d = x1.shape
    s = t // ada3.shape[0]
    ne = 2 * u_bf.shape[0]
    tm = min(TM_PEER, s)
    te = TE_PEER
    assert s % tm == 0 and tm % LANES == 0 and ne % te == 0, (s, tm, ne, te)
    assert te % (8 * N_KEYS) == 0, te
    tab2_spec = pl.BlockSpec((P_HEADS, tm // LANES, N_KEYS // 2, LANES), lambda i, j: (0, i, 0, 0))
    tab1_spec = pl.BlockSpec((P_HEADS, tm // LANES, te // N_KEYS, LANES), lambda i, j: (0, i, j, 0))
    return pl.pallas_call(
        functools.partial(_peer_kernel, tm=tm, te=te),
        grid=(t // tm, ne // te),
        in_specs=[pl.BlockSpec((d // 2, tm), lambda i, j: (0, i)),
                  pl.BlockSpec((te // 2, d), lambda i, j: (j, 0)),
                  pl.BlockSpec((d // 2, te), lambda i, j: (0, j)),
                  tab1_spec, tab2_spec, tab1_spec, tab2_spec,
                  pl.BlockSpec((tm, d), lambda i, j: (i, 0)),
                  pl.BlockSpec((1, 6, d), lambda i, j: ((i * tm) // s, 0, 0)),
                  pl.BlockSpec((1, d), lambda i, j: (0, 0))],
        out_specs=pl.BlockSpec((tm, d), lambda i, j: (i, 0)),
        out_shape=jax.ShapeDtypeStruct((t, d), F32),
        scratch_shapes=[pltpu.VMEM((d, tm), F32),
                        pltpu.VMEM((te, tm), BF16),
                        pltpu.VMEM((te, tm), BF16)],
        compiler_params=pltpu.CompilerParams(dimension_semantics=("arbitrary", "arbitrary"),
                                             vmem_limit_bytes=VMEM_LIMIT_BYTES),
        name="peer",
    )(h2t, u_bf, vt_bf, cnt, rk2, p1, p2, x1, ada3, norm_post)


def _layer(x, c, w_ada, b_ada, norm1_pre, norm1_post, w_in, conv_a_w, conv_qk_w, b_igate, b_fgate, mh_norm_w,
           w_branch_a, w_branch_m, w_out, norm2_pre, norm2_post, peer_wq, peer_subkeys, peer_u, peer_v):
    bsz, s, d = x.shape
    ng = 2 * M_HEADS
    ada3 = _ada(c, w_ada, b_ada).reshape(bsz, 6, d)

    o_if = 3 * d + 2 * M_HEADS * M_DK + 2 * d
    w_main = jnp.concatenate([w_in[:, :o_if], w_in[:, o_if + ng:]], axis=1).astype(BF16)
    w_if = w_in[:, o_if:o_if + ng]
    wif_col = jnp.pad(w_if, ((0, 0), (0, LANES - ng))).astype(BF16)
    wif_row = w_if.T.astype(BF16)
    b_if = jnp.concatenate([b_igate, b_fgate])
    bif_col = jnp.pad(b_if, (0, LANES - ng)).reshape(1, LANES)
    bif_row = b_if.reshape(ng, 1)

    ya, q, kt, v, so, sga, sgm, gcol, grow = _inproj(
        x, ada3, norm1_pre.reshape(1, d), w_main, wif_col, wif_row, bif_col, bif_row, conv_a_w, conv_qk_w)
    x1 = _mixer(x, ada3, q, kt, v, so, sga, sgm, ya, gcol, grow, mh_norm_w.reshape(1, d),
                norm1_post.reshape(1, d), w_branch_a.astype(BF16), w_branch_m.astype(BF16), w_out.astype(BF16))

    x1f = x1.reshape(bsz * s, d)
    h2t, cnt, rk2, p1, p2 = _route(x1f, ada3, norm2_pre.reshape(1, d), peer_wq.T.astype(BF16),
                                   peer_subkeys.astype(BF16))
    u_words, vt_words = _expert_words(peer_u, peer_v)
    out = _peer(h2t, u_words, vt_words, cnt, rk2, p1, p2, x1f, ada3, norm2_post.reshape(1, d))
    return out.reshape(bsz, s, d)


def kernel(x, c, w_ada, b_ada, norm1_pre, norm1_post, w_in, conv_a_w, conv_qk_w, b_igate, b_fgate, mh_norm_w, w_branch_a, w_branch_m, w_out, norm2_pre, norm2_post, peer_wq, peer_subkeys, peer_u, peer_v):
    for l in range(w_ada.shape[0]):
        x = _layer(x, c, w_ada[l], b_ada[l], norm1_pre[l], norm1_post[l], w_in[l], conv_a_w[l], conv_qk_w[l],
                   b_igate[l], b_fgate[l], mh_norm_w[l], w_branch_a[l], w_branch_m[l], w_out[l], norm2_pre[l],
                   norm2_post[l], peer_wq[l], peer_subkeys[l], peer_u[l], peer_v[l])
    return x
```
